```python
import math
import jax
import jax.numpy as jnp
from jax import lax
import numpy as np

D_MODEL = 1024
BATCH = 8
SEQ = 2048
DEPTH = 2
DEC_BATCH = 128
DEC_SEQ = 4
PAST_LEN = 2048
PAGE_SIZE = 128

POOL_WIDTH = D_MODEL // 4
POOL_WINDOWS = (2, 4, 8, 16)
POOL_GROUPS = len(POOL_WINDOWS)
POOL_GROUP_WIDTH = POOL_WIDTH // POOL_GROUPS
POOL_BUF = max(POOL_WINDOWS) - 1

CONV_WIDTH = D_MODEL // 4
CONV_K = 31
CONV_BUF = CONV_K - 1

ATT_HEADS = 4
ATT_HEAD_DIM = D_MODEL // 8
ATT_WIDTH = ATT_HEADS * ATT_HEAD_DIM
ROT_DIM = ATT_HEAD_DIM // 4
ROPE_THETA = 500000.0
MOBA_BLOCK = 256
MOBA_TOPK = 3
MOBA_QUERY_ROWS = 128

MIX_WIDTH = POOL_WIDTH + CONV_WIDTH + ATT_WIDTH
IN_WIDTH = POOL_WIDTH + 2 * CONV_WIDTH + 3 * ATT_WIDTH

N_MEM = 256
MEM_HEADS = 4
MEM_HEAD_DIM = D_MODEL // MEM_HEADS

D_FF = ((8 * D_MODEL // 3 + 127) // 128) * 128
EPS = 1e-6

kernel_name = 'hybrid_pool_conv_moba_decoder_step'


def rms_norm(x, g):
    xf = x.astype(jnp.float32)
    y = xf * lax.rsqrt(jnp.mean(xf * xf, axis=-1, keepdims=True) + EPS)
    return (y * g.astype(jnp.float32)).astype(x.dtype)


def layer_norm(x, g, b):
    xf = x.astype(jnp.float32)
    xc = xf - jnp.mean(xf, axis=-1, keepdims=True)
    y = xc * lax.rsqrt(jnp.mean(xc * xc, axis=-1, keepdims=True) + EPS)
    return (y * g.astype(jnp.float32) + b.astype(jnp.float32)).astype(x.dtype)


def swiglu_ffn(x, w_in, w_out):
    gate, up = jnp.split(x @ w_in, 2, axis=-1)
    return (jax.nn.silu(gate) * up) @ w_out


def partial_rope(x, pos):
    half = ROT_DIM // 2
    inv_freq = jnp.power(jnp.float32(ROPE_THETA), -jnp.arange(half, dtype=jnp.float32) / half)
    ang = pos.astype(jnp.float32)[:, None] * inv_freq[None, :]
    cos = jnp.cos(ang)[None, :, None, :]
    sin = jnp.sin(ang)[None, :, None, :]
    xf = x.astype(jnp.float32)
    x1 = xf[..., :half]
    x2 = xf[..., half:ROT_DIM]
    out = jnp.concatenate([x1 * cos - x2 * sin, x2 * cos + x1 * sin, xf[..., ROT_DIM:]], axis=-1)
    return out.astype(x.dtype)


def pool_mix(u, buf, pos0, w, scale):
    bsz, t, _ = u.shape
    ext = jnp.concatenate([buf, u], axis=1).astype(jnp.float32)
    csum = jnp.concatenate([jnp.zeros_like(ext[:, :1]), jnp.cumsum(ext, axis=1)], axis=1)
    upto = csum[:, POOL_BUF + 1:]
    pos = pos0 + jnp.arange(t, dtype=jnp.int32)
    diffs = []
    for g, win in enumerate(POOL_WINDOWS):
        ch = slice(g * POOL_GROUP_WIDTH, (g + 1) * POOL_GROUP_WIDTH)
        before = csum[:, POOL_BUF + 1 - win:POOL_BUF + 1 - win + t, ch]
        count = jnp.minimum(pos + 1, win).astype(jnp.float32)[None, :, None]
        diffs.append((upto[..., ch] - before) / count - ext[:, POOL_BUF:, ch])
    d = jnp.stack(diffs, axis=2)
    y = jnp.einsum('btgc,gcd->btgd', d, w.astype(jnp.float32)).reshape(bsz, t, POOL_WIDTH)
    y = y * scale.astype(jnp.float32)
    return y.astype(u.dtype), ext[:, -POOL_BUF:].astype(u.dtype)


def conv_mix(a, gate, buf, dw_w, dw_b, ln_g, ln_b, pw_w, pw_b):
    h = a * jax.nn.sigmoid(gate)
    ext = jnp.concatenate([buf, h], axis=1)
    y = lax.conv_general_dilated(ext, dw_w[:, None, :], window_strides=(1,), padding='VALID',
                                 dimension_numbers=('NWC', 'WIO', 'NWC'),
                                 feature_group_count=CONV_WIDTH) + dw_b
    y = jax.nn.silu(layer_norm(y, ln_g, ln_b))
    return y @ pw_w + pw_b, ext[:, -CONV_BUF:]


def moba_attend(q, k_all, v_all, q_pos0):
    bsz, tq, nh, dh = q.shape
    klen = k_all.shape[1]
    nb = -(-klen // MOBA_BLOCK)
    pad = nb * MOBA_BLOCK - klen
    kb = jnp.pad(k_all, ((0, 0), (0, pad), (0, 0), (0, 0))).reshape(bsz, nb, MOBA_BLOCK, nh, dh).transpose(0, 3, 1, 2, 4)
    vb = jnp.pad(v_all, ((0, 0), (0, pad), (0, 0), (0, 0))).reshape(bsz, nb, MOBA_BLOCK, nh, dh).transpose(0, 3, 1, 2, 4)
    kmean = jnp.mean(kb.astype(jnp.float32), axis=3)
    topk = min(MOBA_TOPK, nb)
    cap = max(1, MOBA_QUERY_ROWS // bsz)
    qc = max(c for c in range(1, min(cap, tq) + 1) if tq % c == 0)
    nchunk = tq // qc
    qh = q.transpose(0, 2, 1, 3).reshape(bsz, nh, nchunk, qc, dh).transpose(2, 0, 1, 3, 4)
    qpos = (q_pos0 + jnp.arange(tq, dtype=jnp.int32)).reshape(nchunk, qc)
    blk_ids = jnp.arange(nb, dtype=jnp.int32)
    key_off = jnp.arange(MOBA_BLOCK, dtype=jnp.int32)
    b_idx = jnp.arange(bsz)[:, None, None, None]
    h_idx = jnp.arange(nh)[None, :, None, None]
    scale = dh ** -0.5

    def one_chunk(args):
        qx, p = args
        own = p // MOBA_BLOCK
        qf = qx.astype(jnp.float32)
        gate = jnp.einsum('bhqd,bhnd->bhqn', qf, kmean)
        gate = jnp.where(blk_ids[None, :] < own[:, None], gate, -jnp.inf)
        _, top = lax.top_k(gate, topk)
        top_ok = top < own[:, None]
        sel = jnp.concatenate([top, jnp.broadcast_to(own[:, None], (bsz, nh, qc, 1))], axis=-1)
        ok = jnp.concatenate([top_ok, jnp.ones((bsz, nh, qc, 1), dtype=bool)], axis=-1)
        kg = kb[b_idx, h_idx, sel]
        vg = vb[b_idx, h_idx, sel]
        s = jnp.einsum('bhqd,bhqsjd->bhqsj', qf, kg.astype(jnp.float32)) * scale
        kpos = sel[..., None] * MOBA_BLOCK + key_off
        mask = ok[..., None] & (kpos <= p[:, None, None])
        s = jnp.where(mask, s, -jnp.inf)
        pr = jax.nn.softmax(s.reshape(bsz, nh, qc, -1), axis=-1).reshape(s.shape)
        o = jnp.einsum('bhqsj,bhqsjd->bhqd', pr, vg.astype(jnp.float32))
        return o.astype(q.dtype)

    out = lax.map(one_chunk, (qh, qpos))
    return out.transpose(1, 0, 3, 2, 4).reshape(bsz, tq, nh, dh)


def mixer_groups(h, pos0, pool_buf, conv_buf, past_k, past_v, p):
    bsz, t, _ = h.shape
    z = h @ p['w_in']
    c1 = POOL_WIDTH
    c2 = c1 + CONV_WIDTH
    c3 = c2 + CONV_WIDTH
    c4 = c3 + ATT_WIDTH
    c5 = c4 + ATT_WIDTH
    y_pool, new_pool = pool_mix(z[..., :c1], pool_buf, pos0, p['pool_w'], p['pool_scale'])
    y_conv, new_conv = conv_mix(z[..., c1:c2], z[..., c2:c3], conv_buf, p['conv_dw_w'], p['conv_dw_b'],
                                p['conv_ln_g'], p['conv_ln_b'], p['conv_pw_w'], p['conv_pw_b'])
    pos = pos0 + jnp.arange(t, dtype=jnp.int32)
    q = partial_rope(rms_norm(z[..., c3:c4].reshape(bsz, t, ATT_HEADS, ATT_HEAD_DIM), p['att_q_norm']), pos)
    k = partial_rope(rms_norm(z[..., c4:c5].reshape(bsz, t, ATT_HEADS, ATT_HEAD_DIM), p['att_k_norm']), pos)
    v = z[..., c5:].reshape(bsz, t, ATT_HEADS, ATT_HEAD_DIM)
    if past_k is None:
        k_all, v_all = k, v
    else:
        k_all = jnp.concatenate([past_k, k], axis=1)
        v_all = jnp.concatenate([past_v, v], axis=1)
    y_att = moba_attend(q, k_all, v_all, pos0).reshape(bsz, t, ATT_WIDTH)
    y = jnp.concatenate([y_pool, y_conv, y_att], axis=-1) @ p['w_out']
    return y, new_pool, new_conv, k, v


def memory_kv(mem, p):
    bsz = mem.shape[0]
    kv = rms_norm(mem, p['ln_mem']) @ p['w_ckv']
    mk, mv = jnp.split(kv, 2, axis=-1)
    mk = rms_norm(mk.reshape(bsz, N_MEM, MEM_HEADS, MEM_HEAD_DIM), p['ck_norm'])
    mv = mv.reshape(bsz, N_MEM, MEM_HEADS, MEM_HEAD_DIM)
    return mk, mv


def memory_attend(h, mk, mv, p):
    bsz, t, _ = h.shape
    q = rms_norm((h @ p['w_cq']).reshape(bsz, t, MEM_HEADS, MEM_HEAD_DIM), p['cq_norm'])
    s = jnp.einsum('bthd,bmhd->bhtm', q.astype(jnp.float32), mk.astype(jnp.float32)) * (MEM_HEAD_DIM ** -0.5)
    pr = jax.nn.softmax(s, axis=-1)
    o = jnp.einsum('bhtm,bmhd->bthd', pr, mv.astype(jnp.float32)).astype(h.dtype)
    return o.reshape(bsz, t, D_MODEL) @ p['w_co']


def decoder_layer(x, pos0, pool_buf, conv_buf, past_k, past_v, mk, mv, p):
    x = x + 0.5 * swiglu_ffn(rms_norm(x, p['ln_ffn1']), p['w_ffn1_in'], p['w_ffn1_out'])
    y, new_pool, new_conv, k, v = mixer_groups(rms_norm(x, p['ln_mix']), pos0, pool_buf, conv_buf, past_k, past_v, p)
    x = x + y
    x = x + memory_attend(rms_norm(x, p['ln_cross']), mk, mv, p)
    x = x + 0.5 * swiglu_ffn(rms_norm(x, p['ln_ffn2']), p['w_ffn2_in'], p['w_ffn2_out'])
    return x, new_pool, new_conv, k, v


def setup_inputs(seed: int = 0) -> dict:
    key = jax.random.key(seed)
    ks = iter(jax.random.split(key, 64))
    f32 = jnp.float32
    L = DEPTH

    def nrm(shape, scale=1.0):
        return jax.random.normal(next(ks), shape, f32) * scale

    def gain(shape):
        return 1.0 + 0.05 * jax.random.normal(next(ks), shape, f32)

    n_pages = PAST_LEN // PAGE_SIZE
    n_phys = (DEC_BATCH * n_pages * 5) // 4
    inp = {}
    inp['x_prompt'] = nrm((BATCH, SEQ, D_MODEL))
    inp['x_sample'] = nrm((DEC_BATCH, DEC_SEQ, D_MODEL))
    inp['cache_att_k'] = nrm((L, n_phys, PAGE_SIZE, ATT_HEADS, ATT_HEAD_DIM))
    inp['cache_att_v'] = nrm((L, n_phys, PAGE_SIZE, ATT_HEADS, ATT_HEAD_DIM))
    inp['cache_mem_k'] = nrm((L, DEC_BATCH, N_MEM, MEM_HEADS, MEM_HEAD_DIM))
    inp['cache_mem_v'] = nrm((L, DEC_BATCH, N_MEM, MEM_HEADS, MEM_HEAD_DIM))
    inp['state_pool'] = nrm((L, DEC_BATCH, POOL_BUF, POOL_WIDTH))
    inp['state_conv'] = nrm((L, DEC_BATCH, CONV_BUF, CONV_WIDTH), 0.5)
    perm = jax.random.permutation(next(ks), n_phys)
    inp['page_table'] = perm[:DEC_BATCH * n_pages].reshape(DEC_BATCH, n_pages).astype(jnp.int32)
    inp['mem_prompt'] = nrm((BATCH, N_MEM, D_MODEL))
    inp['ln_ffn1'] = gain((L, D_MODEL))
    inp['w_ffn1_in'] = nrm((L, D_MODEL, 2 * D_FF), D_MODEL ** -0.5)
    inp['w_ffn1_out'] = nrm((L, D_FF, D_MODEL), D_FF ** -0.5)
    inp['ln_mix'] = gain((L, D_MODEL))
    inp['w_in'] = nrm((L, D_MODEL, IN_WIDTH), D_MODEL ** -0.5)
    inp['pool_w'] = nrm((L, POOL_GROUPS, POOL_GROUP_WIDTH, POOL_GROUP_WIDTH), POOL_GROUP_WIDTH ** -0.5)
    inp['pool_scale'] = 1.0 + 0.1 * nrm((L, POOL_WIDTH))
    inp['conv_dw_w'] = nrm((L, CONV_K, CONV_WIDTH), CONV_K ** -0.5)
    inp['conv_dw_b'] = nrm((L, CONV_WIDTH), 0.02)
    inp['conv_ln_g'] = gain((L, CONV_WIDTH))
    inp['conv_ln_b'] = nrm((L, CONV_WIDTH), 0.02)
    inp['conv_pw_w'] = nrm((L, CONV_WIDTH, CONV_WIDTH), CONV_WIDTH ** -0.5)
    inp['conv_pw_b'] = nrm((L, CONV_WIDTH), 0.02)
    inp['att_q_norm'] = gain((L, ATT_HEAD_DIM))
    inp['att_k_norm'] = gain((L, ATT_HEAD_DIM))
    inp['w_out'] = nrm((L, MIX_WIDTH, D_MODEL), MIX_WIDTH ** -0.5)
    inp['ln_cross'] = gain((L, D_MODEL))
    inp['ln_mem'] = gain((L, D_MODEL))
    inp['w_cq'] = nrm((L, D_MODEL, D_MODEL), D_MODEL ** -0.5)
    inp['w_ckv'] = nrm((L, D_MODEL, 2 * D_MODEL), D_MODEL ** -0.5)
    inp['cq_norm'] = gain((L, MEM_HEAD_DIM))
    inp['ck_norm'] = gain((L, MEM_HEAD_DIM))
    inp['w_co'] = nrm((L, D_MODEL, D_MODEL), D_MODEL ** -0.5)
    inp['ln_ffn2'] = gain((L, D_MODEL))
    inp['w_ffn2_in'] = nrm((L, D_MODEL, 2 * D_FF), D_MODEL ** -0.5)
    inp['w_ffn2_out'] = nrm((L, D_FF, D_MODEL), D_FF ** -0.5)
    return inp


def reference(x_prompt, x_sample, cache_att_k, cache_att_v, cache_mem_k, cache_mem_v, state_pool, state_conv,
              page_table, mem_prompt, ln_ffn1, w_ffn1_in, w_ffn1_out, ln_mix, w_in, pool_w, pool_scale,
              conv_dw_w, conv_dw_b, conv_ln_g, conv_ln_b, conv_pw_w, conv_pw_b, att_q_norm, att_k_norm, w_out,
              ln_cross, ln_mem, w_cq, w_ckv, cq_norm, ck_norm, w_co, ln_ffn2, w_ffn2_in, w_ffn2_out):
    weights = dict(ln_ffn1=ln_ffn1, w_ffn1_in=w_ffn1_in, w_ffn1_out=w_ffn1_out, ln_mix=ln_mix, w_in=w_in,
                   pool_w=pool_w, pool_scale=pool_scale, conv_dw_w=conv_dw_w, conv_dw_b=conv_dw_b,
                   conv_ln_g=conv_ln_g, conv_ln_b=conv_ln_b, conv_pw_w=conv_pw_w, conv_pw_b=conv_pw_b,
                   att_q_norm=att_q_norm, att_k_norm=att_k_norm, w_out=w_out, ln_cross=ln_cross, ln_mem=ln_mem,
                   w_cq=w_cq, w_ckv=w_ckv, cq_norm=cq_norm, ck_norm=ck_norm, w_co=w_co, ln_ffn2=ln_ffn2,
                   w_ffn2_in=w_ffn2_in, w_ffn2_out=w_ffn2_out)
    bsz = x_prompt.shape[0]
    dbsz = x_sample.shape[0]
    n_pages = page_table.shape[1]
    past_len = n_pages * PAGE_SIZE
    hp = x_prompt
    hs = x_sample
    p_k, p_v, p_mk, p_mv, p_pool, p_conv = [], [], [], [], [], []
    s_k, s_v, s_pool, s_conv = [], [], [], []
    for l in range(DEPTH):
        p = {name: w[l] for name, w in weights.items()}
        mk, mv = memory_kv(mem_prompt, p)
        zero_pool = jnp.zeros((bsz, POOL_BUF, POOL_WIDTH), hp.dtype)
        zero_conv = jnp.zeros((bsz, CONV_BUF, CONV_WIDTH), hp.dtype)
        hp, np_pool, np_conv, nk, nv = decoder_layer(hp, 0, zero_pool, zero_conv, None, None, mk, mv, p)
        p_k.append(nk)
        p_v.append(nv)
        p_mk.append(mk)
        p_mv.append(mv)
        p_pool.append(np_pool)
        p_conv.append(np_conv)
        past_k = cache_att_k[l][page_table].reshape(dbsz, past_len, ATT_HEADS, ATT_HEAD_DIM)
        past_v = cache_att_v[l][page_table].reshape(dbsz, past_len, ATT_HEADS, ATT_HEAD_DIM)
        hs, ns_pool, ns_conv, sk, sv = decoder_layer(hs, past_len, state_pool[l], state_conv[l], past_k, past_v,
                                                     cache_mem_k[l], cache_mem_v[l], p)
        s_k.append(sk)
        s_v.append(sv)
        s_pool.append(ns_pool)
        s_conv.append(ns_conv)
    return (hp, hs, jnp.stack(p_k), jnp.stack(p_v), jnp.stack(p_mk), jnp.stack(p_mv), jnp.stack(p_pool),
            jnp.stack(p_conv), jnp.stack(s_k), jnp.stack(s_v), jnp.stack(s_pool), jnp.stack(s_conv))
```

```python
import functools

import jax
import jax.numpy as jnp
import numpy as np
from jax import lax
from jax.experimental import pallas as pl
from jax.experimental.pallas import tpu as pltpu

D_MODEL = 1024
DEPTH = 2
PAGE_SIZE = 128

POOL_WIDTH = D_MODEL // 4
POOL_WINDOWS = (2, 4, 8, 16)
POOL_GROUP_WIDTH = POOL_WIDTH // len(POOL_WINDOWS)
POOL_BUF = max(POOL_WINDOWS) - 1

CONV_WIDTH = D_MODEL // 4
CONV_K = 31
CONV_BUF = CONV_K - 1

ATT_HEADS = 4
ATT_HEAD_DIM = D_MODEL // 8
ATT_WIDTH = ATT_HEADS * ATT_HEAD_DIM
ROT_DIM = ATT_HEAD_DIM // 4
ROPE_THETA = 500000.0
MOBA_BLOCK = 256
MOBA_TOPK = 3

IN_WIDTH = POOL_WIDTH + 2 * CONV_WIDTH + 3 * ATT_WIDTH

N_MEM = 256
MEM_HEADS = 4
MEM_HEAD_DIM = D_MODEL // MEM_HEADS

D_FF = ((8 * D_MODEL // 3 + 127) // 128) * 128
EPS = 1e-6

LANES = 128
SUBLANES = 8
VMEM_LIMIT_BYTES = 56 * 1024 * 1024

F32 = jnp.float32
BF16 = jnp.bfloat16
NEG_INF = float("-inf")

_NT = (((1,), (1,)), ((), ()))


def _params(n_axes):
    return pltpu.CompilerParams(dimension_semantics=("arbitrary",) * n_axes,
                                vmem_limit_bytes=VMEM_LIMIT_BYTES)


def _rms(x, g):
    return x * lax.rsqrt(jnp.mean(x * x, axis=-1, keepdims=True) + EPS) * g


def _layer_spec(shape, l, n_grid):
    zeros = (0,) * len(shape)
    if n_grid == 1:
        return pl.BlockSpec((None,) + tuple(shape), lambda i: (l,) + zeros)
    if n_grid == 2:
        return pl.BlockSpec((None,) + tuple(shape), lambda i, j: (l,) + zeros)
    return pl.BlockSpec((None,) + tuple(shape), lambda i, j, k: (l,) + zeros)


FFN_CHUNK = 256


def _ffn_kernel(x_ref, g_ref, wi_ref, wo_ref, o_ref):
    x = x_ref[...]
    xn = _rms(x, g_ref[...]).astype(BF16)
    acc = None
    for c in range(D_FF // FFN_CHUNK):
        lo = c * FFN_CHUNK
        gate = jnp.dot(xn, wi_ref[:, lo:lo + FFN_CHUNK], preferred_element_type=F32)
        up = jnp.dot(xn, wi_ref[:, D_FF + lo:D_FF + lo + FFN_CHUNK], preferred_element_type=F32)
        act = (gate * jax.nn.sigmoid(gate) * up).astype(BF16)
        part = jnp.dot(act, wo_ref[lo:lo + FFN_CHUNK, :], preferred_element_type=F32)
        acc = part if acc is None else acc + part
    o_ref[...] = x + 0.5 * acc


def _ffn(x, ln, w_in, w_out, l, tm=512):
    m = x.shape[0]
    return pl.pallas_call(
        _ffn_kernel,
        grid=(m // tm,),
        in_specs=[
            pl.BlockSpec((tm, D_MODEL), lambda i: (i, 0)),
            _layer_spec((1, D_MODEL), l, 1),
            _layer_spec((D_MODEL, 2 * D_FF), l, 1),
            _layer_spec((D_FF, D_MODEL), l, 1),
        ],
        out_specs=pl.BlockSpec((tm, D_MODEL), lambda i: (i, 0)),
        out_shape=jax.ShapeDtypeStruct((m, D_MODEL), F32),
        compiler_params=_params(1),
        name="ffn",
    )(x, ln, w_in, w_out)


def _norm_proj_kernel(*refs, segs, rope):
    x_ref, g_ref, w_ref, hg_ref = refs[:4]
    pos = 4
    if rope:
        cos_ref, sin_lo_ref, sin_hi_ref = refs[4:7]
        pos = 7
    out_refs = refs[pos:]
    xn = _rms(x_ref[...], g_ref[...]).astype(BF16)
    for (c0, width, head_dim, gain_row, use_rope), o_ref in zip(segs, out_refs):
        z = jnp.dot(xn, w_ref[:, c0:c0 + width], preferred_element_type=F32)
        if head_dim is None:
            o_ref[...] = z
            continue
        gain = hg_ref[gain_row:gain_row + 1, :]
        for h in range(width // head_dim):
            zh = _rms(z[:, h * head_dim:(h + 1) * head_dim], gain)
            if use_rope:
                half = ROT_DIM // 2
                zh = (zh * cos_ref[...]
                      + pltpu.roll(zh, half, 1) * sin_hi_ref[...]
                      + pltpu.roll(zh, LANES - half, 1) * sin_lo_ref[...])
            o_ref[:, h * head_dim:(h + 1) * head_dim] = zh


def _norm_proj(x, ln, w, l, head_gains, segs, tables=None, tm=512):
    m = x.shape[0]
    n_cols = w.shape[-1]
    rope = tables is not None
    in_specs = [
        pl.BlockSpec((tm, D_MODEL), lambda i: (i, 0)),
        _layer_spec((1, D_MODEL), l, 1),
        _layer_spec((D_MODEL, n_cols), l, 1),
        pl.BlockSpec(head_gains.shape, lambda i: (0, 0)),
    ]
    args = [x, ln, w, head_gains]
    if rope:
        period = tables[0].shape[0] // tm
        for t in tables:
            in_specs.append(pl.BlockSpec((tm, LANES), lambda i: (i % period, 0)))
            args.append(t)
    return pl.pallas_call(
        functools.partial(_norm_proj_kernel, segs=tuple(segs), rope=rope),
        grid=(m // tm,),
        in_specs=in_specs,
        out_specs=[pl.BlockSpec((tm, s[1]), lambda i: (i, 0)) for s in segs],
        out_shape=[jax.ShapeDtypeStruct((m, s[1]), F32) for s in segs],
        compiler_params=_params(1),
        name="norm_proj",
    )(*args)


def _rope_tables(positions):
    half = ROT_DIM // 2
    inv_freq = jnp.power(jnp.float32(ROPE_THETA), -jnp.arange(half, dtype=F32) / half)
    ang = positions.astype(F32)[:, None] * inv_freq[None, :]
    cos, sin = jnp.cos(ang), jnp.sin(ang)
    n = positions.shape[0]
    rest = ATT_HEAD_DIM - ROT_DIM
    cos_t = jnp.concatenate([cos, cos, jnp.ones((n, rest), F32)], axis=1)
    sin_lo = jnp.concatenate([-sin, jnp.zeros((n, half + rest), F32)], axis=1)
    sin_hi = jnp.concatenate([jnp.zeros((n, half), F32), sin, jnp.zeros((n, rest), F32)], axis=1)
    return cos_t, sin_lo, sin_hi


def _out_proj_kernel(*refs, n_terms):
    x_ref = refs[0]
    y_refs = refs[1:1 + n_terms]
    w_refs = refs[1 + n_terms:1 + 2 * n_terms]
    o_ref = refs[1 + 2 * n_terms]
    acc = x_ref[...]
    for y_ref, w_ref in zip(y_refs, w_refs):
        acc = acc + jnp.dot(y_ref[...].astype(BF16), w_ref[...], preferred_element_type=F32)
    o_ref[...] = acc


def _out_proj(x, ys, w, l, row_starts, tm=512):
    m = x.shape[0]
    in_specs = [pl.BlockSpec((tm, D_MODEL), lambda i: (i, 0))]
    for y in ys:
        in_specs.append(pl.BlockSpec((tm, y.shape[1]), lambda i: (i, 0)))
    for y, r0 in zip(ys, row_starts):
        width = y.shape[1]
        blk = r0 // width
        in_specs.append(pl.BlockSpec((None, width, D_MODEL), lambda i, blk=blk: (l, blk, 0)))
    return pl.pallas_call(
        functools.partial(_out_proj_kernel, n_terms=len(ys)),
        grid=(m // tm,),
        in_specs=in_specs,
        out_specs=pl.BlockSpec((tm, D_MODEL), lambda i: (i, 0)),
        out_shape=jax.ShapeDtypeStruct((m, D_MODEL), F32),
        compiler_params=_params(1),
        name="out_proj",
    )(x, *ys, *([w] * len(ys)))


def _select_blocks(gate, n_allowed):
    lane = lax.broadcasted_iota(jnp.int32, gate.shape, 1)
    allowed = lane < n_allowed
    g = jnp.where(allowed, gate, NEG_INF)
    rank = jnp.zeros(gate.shape, F32)
    for r in range(1, SUBLANES):
        lower = pltpu.roll(g, r, 1)
        higher = pltpu.roll(g, LANES - r, 1)
        rank = rank + jnp.where(lower >= g, 1.0, 0.0) + jnp.where(higher > g, 1.0, 0.0)
    return jnp.where(allowed, jnp.where(rank < MOBA_TOPK, 1.0, 0.0), 0.0)


def _block_expander(n_keys):
    blk = np.arange(n_keys) // MOBA_BLOCK
    return jnp.asarray((np.arange(LANES)[:, None] == blk[None, :]).astype(np.float32), dtype=BF16)


def _moba_prompt_kernel(q_ref, k_ref, v_ref, e_ref, o_ref, kb_ref, vb_ref, km_ref, sel_ref,
                        m_ref, l_ref, acc_ref, *, n_blocks):
    qi = pl.program_id(2)
    blk = MOBA_BLOCK

    @pl.when(qi == 0)
    def _():
        kb_ref[...] = k_ref[0].astype(BF16)
        vb_ref[...] = v_ref[0].astype(BF16)
        km_ref[...] = jnp.zeros(km_ref.shape, F32)
        for n in range(n_blocks):
            km_ref[n:n + 1, :] = jnp.mean(k_ref[0, n * blk:(n + 1) * blk, :], axis=0, keepdims=True)

    q = q_ref[0]
    gate = lax.dot_general(q, km_ref[...], _NT, precision=lax.Precision.HIGHEST,
                           preferred_element_type=F32)
    sel = _select_blocks(gate, qi)
    sel_ref[...] = jnp.dot(sel.astype(BF16), e_ref[...], preferred_element_type=F32)

    qs = (q * (ATT_HEAD_DIM ** -0.5)).astype(BF16)
    own = pl.multiple_of(qi * blk, blk)
    s = lax.dot_general(qs, kb_ref[pl.ds(own, blk), :], _NT, preferred_element_type=F32)
    row = lax.broadcasted_iota(jnp.int32, s.shape, 0)
    col = lax.broadcasted_iota(jnp.int32, s.shape, 1)
    s = jnp.where(row >= col, s, NEG_INF)
    m = jnp.max(s, axis=-1, keepdims=True)
    p = jnp.exp(s - m)
    m_ref[...] = m
    l_ref[...] = jnp.sum(p, axis=-1, keepdims=True)
    acc_ref[...] = jnp.dot(p.astype(BF16), vb_ref[pl.ds(own, blk), :], preferred_element_type=F32)

    for n in range(n_blocks - 1):
        @pl.when(n < qi)
        def _(n=n):
            s = lax.dot_general(qs, kb_ref[n * blk:(n + 1) * blk, :], _NT, preferred_element_type=F32)
            s = jnp.where(sel_ref[:, n * blk:(n + 1) * blk] > 0.5, s, NEG_INF)
            m_old = m_ref[...]
            m_new = jnp.maximum(m_old, jnp.max(s, axis=-1, keepdims=True))
            alpha = jnp.exp(m_old - m_new)
            p = jnp.exp(s - m_new)
            l_ref[...] = alpha * l_ref[...] + jnp.sum(p, axis=-1, keepdims=True)
            acc_ref[...] = alpha * acc_ref[...] + jnp.dot(
                p.astype(BF16), vb_ref[n * blk:(n + 1) * blk, :], preferred_element_type=F32)
            m_ref[...] = m_new

    o_ref[0] = acc_ref[...] / l_ref[...]


def _moba_prompt(q, k, v):
    bsz, t, _ = q.shape
    assert t % MOBA_BLOCK == 0 and t // MOBA_BLOCK <= SUBLANES
    n_blocks = t // MOBA_BLOCK
    dh = ATT_HEAD_DIM
    return pl.pallas_call(
        functools.partial(_moba_prompt_kernel, n_blocks=n_blocks),
        grid=(bsz, ATT_HEADS, n_blocks),
        in_specs=[
            pl.BlockSpec((1, MOBA_BLOCK, dh), lambda b, h, i: (b, i, h)),
            pl.BlockSpec((1, t, dh), lambda b, h, i: (b, 0, h)),
            pl.BlockSpec((1, t, dh), lambda b, h, i: (b, 0, h)),
            pl.BlockSpec((LANES, t), lambda b, h, i: (0, 0)),
        ],
        out_specs=pl.BlockSpec((1, MOBA_BLOCK, dh), lambda b, h, i: (b, i, h)),
        out_shape=jax.ShapeDtypeStruct((bsz, t, ATT_WIDTH), F32),
        scratch_shapes=[
            pltpu.VMEM((t, dh), BF16),
            pltpu.VMEM((t, dh), BF16),
            pltpu.VMEM((LANES, dh), F32),
            pltpu.VMEM((MOBA_BLOCK, t), F32),
            pltpu.VMEM((MOBA_BLOCK, 1), F32),
            pltpu.VMEM((MOBA_BLOCK, 1), F32),
            pltpu.VMEM((MOBA_BLOCK, dh), F32),
        ],
        compiler_params=_params(3),
        name="moba_prompt",
    )(q, k, v, _block_expander(t))


def _moba_sample_kernel(pt_ref, q_ref, kn_ref, vn_ref, e_ref, *refs, n_pages):
    del pt_ref
    k_refs = refs[:n_pages]
    v_refs = refs[n_pages:2 * n_pages]
    o_ref = refs[2 * n_pages]
    rows = q_ref.shape[1]
    pages_per_block = MOBA_BLOCK // PAGE_SIZE
    n_blocks = n_pages // pages_per_block

    q = q_ref[0]
    lane = lax.broadcasted_iota(jnp.int32, q.shape, 1)
    q_heads = jnp.concatenate(
        [jnp.where(lane // ATT_HEAD_DIM == h, q, 0.0) for h in range(ATT_HEADS)], axis=0)
    qs = (q_heads * (ATT_HEAD_DIM ** -0.5)).astype(BF16)

    mean_rows = []
    s_parts = []
    for n in range(n_blocks):
        total = None
        for j in range(n * pages_per_block, (n + 1) * pages_per_block):
            kp = k_refs[j][0]
            part = jnp.sum(kp, axis=0, keepdims=True)
            total = part if total is None else total + part
            s_parts.append(lax.dot_general(qs, kp.astype(BF16), _NT, preferred_element_type=F32))
        mean_rows.append(total / MOBA_BLOCK)
    mean_rows.append(jnp.zeros((LANES - n_blocks, ATT_WIDTH), F32))
    k_mean = jnp.concatenate(mean_rows, axis=0)
    gate = lax.dot_general(q_heads, k_mean, _NT, precision=lax.Precision.HIGHEST,
                           preferred_element_type=F32)
    sel = _select_blocks(gate, n_blocks)
    sel_keys = jnp.dot(sel.astype(BF16), e_ref[...], preferred_element_type=F32)
    s = jnp.where(sel_keys > 0.5, jnp.concatenate(s_parts, axis=1), NEG_INF)

    pad = jnp.zeros((LANES - rows, ATT_WIDTH), F32)
    k_new = jnp.concatenate([kn_ref[0], pad], axis=0).astype(BF16)
    v_new = jnp.concatenate([vn_ref[0], pad], axis=0).astype(BF16)
    s_own = lax.dot_general(qs, k_new, _NT, preferred_element_type=F32)
    t_row = lax.broadcasted_iota(jnp.int32, s_own.shape, 0) % rows
    key = lax.broadcasted_iota(jnp.int32, s_own.shape, 1)
    s_own = jnp.where(key <= t_row, s_own, NEG_INF)

    m = jnp.maximum(jnp.max(s, axis=-1, keepdims=True), jnp.max(s_own, axis=-1, keepdims=True))
    p = jnp.exp(s - m)
    p_own = jnp.exp(s_own - m)
    denom = jnp.sum(p, axis=-1, keepdims=True) + jnp.sum(p_own, axis=-1, keepdims=True)
    o = jnp.dot(p_own.astype(BF16), v_new, preferred_element_type=F32)
    pb = p.astype(BF16)
    for j in range(n_pages):
        o = o + jnp.dot(pb[:, j * PAGE_SIZE:(j + 1) * PAGE_SIZE], v_refs[j][0].astype(BF16),
                        preferred_element_type=F32)
    o = o / denom
    o_ref[0] = jnp.concatenate(
        [o[h * rows:(h + 1) * rows, h * ATT_HEAD_DIM:(h + 1) * ATT_HEAD_DIM] for h in range(ATT_HEADS)],
        axis=1)


def _moba_sample(q, k_new, v_new, cache_k, cache_v, page_table, l):
    bsz, rows, _ = q.shape
    n_pages = page_table.shape[1]
    n_phys = cache_k.shape[0] // DEPTH
    past_len = n_pages * PAGE_SIZE
    assert past_len % MOBA_BLOCK == 0 and past_len // MOBA_BLOCK <= SUBLANES
    base = l * n_phys
    tok_spec = pl.BlockSpec((1, rows, ATT_WIDTH), lambda b, pt: (b, 0, 0))
    page_specs = [
        pl.BlockSpec((1, PAGE_SIZE, ATT_WIDTH), lambda b, pt, j=j: (base + pt[b * n_pages + j], 0, 0))
        for j in range(n_pages)
    ]
    grid_spec = pltpu.PrefetchScalarGridSpec(
        num_scalar_prefetch=1,
        grid=(bsz,),
        in_specs=[tok_spec, tok_spec, tok_spec,
                  pl.BlockSpec((LANES, past_len), lambda b, pt: (0, 0))] + page_specs + page_specs,
        out_specs=tok_spec,
    )
    return pl.pallas_call(
        functools.partial(_moba_sample_kernel, n_pages=n_pages),
        grid_spec=grid_spec,
        out_shape=jax.ShapeDtypeStruct((bsz, rows, ATT_WIDTH), F32),
        compiler_params=_params(1),
        name="moba_sample",
    )(page_table.reshape(-1), q, k_new, v_new, _block_expander(past_len),
      *([cache_k] * n_pages), *([cache_v] * n_pages))


def _pool_windows_to_lanes(sums, counts, cur):
    lane = lax.broadcasted_iota(jnp.int32, cur.shape, 1)
    d = sums[-1] / counts[-1]
    for g in range(len(POOL_WINDOWS) - 2, -1, -1):
        d = jnp.where(lane < (g + 1) * POOL_GROUP_WIDTH, sums[g] / counts[g], d)
    return d - cur


def _conv_tail(y, ln_g, ln_b, pw_w, pw_b):
    yc = y - jnp.mean(y, axis=-1, keepdims=True)
    yn = yc * lax.rsqrt(jnp.mean(yc * yc, axis=-1, keepdims=True) + EPS) * ln_g + ln_b
    act = yn * jax.nn.sigmoid(yn)
    return jnp.dot(act.astype(BF16), pw_w, preferred_element_type=F32) + pw_b


POOL_HALO = 16
CONV_HALO = 32


def _poolconv_prompt_kernel(u_ref, ag_ref, pw_ref, ps_ref, dw_ref, db_ref, lg_ref, lb_ref, cw_ref, cb_ref,
                            yp_ref, yc_ref, ht_ref, e_ref, h_ref, *, tt):
    ti = pl.program_id(1)

    @pl.when(ti == 0)
    def _():
        e_ref[0:POOL_HALO, :] = jnp.zeros((POOL_HALO, POOL_WIDTH), F32)
        h_ref[0:CONV_HALO, :] = jnp.zeros((CONV_HALO, CONV_WIDTH), F32)

    u = u_ref[...]
    e_ref[POOL_HALO:POOL_HALO + tt, :] = u
    a = ag_ref[:, 0:CONV_WIDTH]
    gate = ag_ref[:, CONV_WIDTH:2 * CONV_WIDTH]
    h_ref[CONV_HALO:CONV_HALO + tt, :] = a * jax.nn.sigmoid(gate)

    pos = ti * tt + lax.broadcasted_iota(jnp.int32, (tt, 1), 0)
    sums, counts = [], []
    run = u
    taken = 1
    for win in POOL_WINDOWS:
        while taken < win:
            run = run + e_ref[POOL_HALO - taken:POOL_HALO - taken + tt, :]
            taken += 1
        sums.append(run)
        counts.append(jnp.minimum(pos + 1, win).astype(F32))
    d = _pool_windows_to_lanes(sums, counts, u)
    yp_ref[...] = jnp.dot(d.astype(BF16), pw_ref[...], preferred_element_type=F32) * ps_ref[...]

    y = jnp.zeros((tt, CONV_WIDTH), F32) + db_ref[...]
    for j in range(CONV_K):
        r0 = CONV_HALO - CONV_BUF + j
        y = y + h_ref[r0:r0 + tt, :] * dw_ref[j:j + 1, :]
    yc_ref[...] = _conv_tail(y, lg_ref[...], lb_ref[...], cw_ref[...], cb_ref[...])

    ht_ref[0] = h_ref[tt:tt + CONV_HALO, :]
    e_ref[0:POOL_HALO, :] = e_ref[tt:tt + POOL_HALO, :]
    h_ref[0:CONV_HALO, :] = h_ref[tt:tt + CONV_HALO, :]


def _poolconv_prompt(u, ag, bsz, t, weights, l, tt=512):
    pool_w, pool_s, dw_w, dw_b, ln_g, ln_b, pw_w, pw_b = weights
    nt = t // tt
    row = lambda b, i: (b * nt + i, 0)
    return pl.pallas_call(
        functools.partial(_poolconv_prompt_kernel, tt=tt),
        grid=(bsz, nt),
        in_specs=[
            pl.BlockSpec((tt, POOL_WIDTH), row),
            pl.BlockSpec((tt, 2 * CONV_WIDTH), row),
            _layer_spec((POOL_WIDTH, POOL_WIDTH), l, 2),
            _layer_spec((1, POOL_WIDTH), l, 2),
            _layer_spec((CONV_HALO, CONV_WIDTH), l, 2),
            _layer_spec((1, CONV_WIDTH), l, 2),
            _layer_spec((1, CONV_WIDTH), l, 2),
            _layer_spec((1, CONV_WIDTH), l, 2),
            _layer_spec((CONV_WIDTH, CONV_WIDTH), l, 2),
            _layer_spec((1, CONV_WIDTH), l, 2),
        ],
        out_specs=[
            pl.BlockSpec((tt, POOL_WIDTH), row),
            pl.BlockSpec((tt, CONV_WIDTH), row),
            pl.BlockSpec((1, CONV_HALO, CONV_WIDTH), lambda b, i: (b, 0, 0)),
        ],
        out_shape=[
            jax.ShapeDtypeStruct((bsz * t, POOL_WIDTH), F32),
            jax.ShapeDtypeStruct((bsz * t, CONV_WIDTH), F32),
            jax.ShapeDtypeStruct((bsz, CONV_HALO, CONV_WIDTH), F32),
        ],
        scratch_shapes=[
            pltpu.VMEM((POOL_HALO + tt, POOL_WIDTH), F32),
            pltpu.VMEM((CONV_HALO + tt, CONV_WIDTH), F32),
        ],
        compiler_params=_params(2),
        name="poolconv_prompt",
    )(u, ag, pool_w, pool_s, dw_w, dw_b, ln_g, ln_b, pw_w, pw_b)


def _poolconv_sample_kernel(sp_ref, u_ref, sc_ref, ag_ref, pw_ref, ps_ref, dw_ref, db_ref, lg_ref, lb_ref,
                            cw_ref, cb_ref, yp_ref, yc_ref, h_ref, *, n_new, pos0):
    def pool_row(r):
        return sp_ref[r] if r < POOL_BUF else u_ref[r - POOL_BUF]

    for t in range(n_new):
        h_ref[t] = ag_ref[t, :, 0:CONV_WIDTH] * jax.nn.sigmoid(ag_ref[t, :, CONV_WIDTH:2 * CONV_WIDTH])

    def conv_row(r):
        return sc_ref[r] if r < CONV_BUF else h_ref[r - CONV_BUF]

    for t in range(n_new):
        cur = u_ref[t]
        sums, counts = [], []
        run = cur
        taken = 1
        for win in POOL_WINDOWS:
            while taken < win:
                run = run + pool_row(POOL_BUF + t - taken)
                taken += 1
            sums.append(run)
            counts.append(float(min(pos0 + t + 1, win)))
        d = _pool_windows_to_lanes(sums, counts, cur)
        yp_ref[t] = jnp.dot(d.astype(BF16), pw_ref[...], preferred_element_type=F32) * ps_ref[...]

        y = jnp.zeros(cur.shape, F32) + db_ref[...]
        for j in range(CONV_K):
            y = y + conv_row(t + j) * dw_ref[j:j + 1, :]
        yc_ref[t] = _conv_tail(y, lg_ref[...], lb_ref[...], cw_ref[...], cb_ref[...])


def _poolconv_sample(state_pool_t, u_t, state_conv_t, ag_t, pos0, weights, l):
    pool_w, pool_s, dw_w, dw_b, ln_g, ln_b, pw_w, pw_b = weights
    n_new, bsz, _ = u_t.shape
    full = lambda a: pl.BlockSpec(a.shape, lambda i: (0,) * a.ndim)
    out = jax.ShapeDtypeStruct((n_new, bsz, CONV_WIDTH), F32)
    return pl.pallas_call(
        functools.partial(_poolconv_sample_kernel, n_new=n_new, pos0=pos0),
        grid=(1,),
        in_specs=[
            full(state_pool_t), full(u_t), full(state_conv_t), full(ag_t),
            _layer_spec((POOL_WIDTH, POOL_WIDTH), l, 1),
            _layer_spec((1, POOL_WIDTH), l, 1),
            _layer_spec((CONV_HALO, CONV_WIDTH), l, 1),
            _layer_spec((1, CONV_WIDTH), l, 1),
            _layer_spec((1, CONV_WIDTH), l, 1),
            _layer_spec((1, CONV_WIDTH), l, 1),
            _layer_spec((CONV_WIDTH, CONV_WIDTH), l, 1),
            _layer_spec((1, CONV_WIDTH), l, 1),
        ],
        out_specs=[pl.BlockSpec(out.shape, lambda i: (0, 0, 0))] * 3,
        out_shape=[out, out, out],
        compiler_params=_params(1),
        name="poolconv_sample",
    )(state_pool_t, u_t, state_conv_t, ag_t, pool_w, pool_s, dw_w, dw_b, ln_g, ln_b, pw_w, pw_b)


def _cross_kernel(q_ref, k_ref, v_ref, o_ref, *, n_seq):
    hd = MEM_HEAD_DIM
    for g in range(n_seq):
        for h in range(MEM_HEADS):
            cols = slice(h * hd, (h + 1) * hd)
            q = (q_ref[g, :, cols] * (hd ** -0.5)).astype(BF16)
            s = lax.dot_general(q, k_ref[g, :, cols].astype(BF16), _NT, preferred_element_type=F32)
            p = jnp.exp(s - jnp.max(s, axis=-1, keepdims=True))
            denom = jnp.sum(p, axis=-1, keepdims=True)
            o = jnp.dot(p.astype(BF16), v_ref[g, :, cols].astype(BF16), preferred_element_type=F32)
            o_ref[g, :, cols] = o / denom


def _cross(q, mem_k, mem_v, seq0, n_seq, tq):
    s_total, r, _ = q.shape
    kv_spec = pl.BlockSpec((n_seq, N_MEM, D_MODEL), lambda s, i: (seq0 // n_seq + s, 0, 0))
    q_spec = pl.BlockSpec((n_seq, tq, D_MODEL), lambda s, i: (s, i, 0))
    assert seq0 % n_seq == 0
    return pl.pallas_call(
        functools.partial(_cross_kernel, n_seq=n_seq),
        grid=(s_total // n_seq, r // tq),
        in_specs=[q_spec, kv_spec, kv_spec],
        out_specs=q_spec,
        out_shape=jax.ShapeDtypeStruct(q.shape, F32),
        compiler_params=_params(2),
        name="cross",
    )(q, mem_k, mem_v)


def _pad_rows(x, rows):
    return jnp.pad(x, ((0, 0), (0, rows - x.shape[1]), (0, 0)))


def kernel(x_prompt, x_sample, cache_att_k, cache_att_v, cache_mem_k, cache_mem_v, state_pool, state_conv, page_table, mem_prompt, ln_ffn1, w_ffn1_in, w_ffn1_out, ln_mix, w_in, pool_w, pool_scale, conv_dw_w, conv_dw_b, conv_ln_g, conv_ln_b, conv_pw_w, conv_pw_b, att_q_norm, att_k_norm, w_out, ln_cross, ln_mem, w_cq, w_ckv, cq_norm, ck_norm, w_co, ln_ffn2, w_ffn2_in, w_ffn2_out):
    bsz, seq, _ = x_prompt.shape
    dbsz, dseq, _ = x_sample.shape
    n_pages = page_table.shape[1]
    past_len = n_pages * PAGE_SIZE
    n_phys = cache_att_k.shape[1]

    bf = lambda w: w.astype(BF16)
    rowv = lambda v: v[:, None, :]
    w1i, w1o, w2i, w2o = bf(w_ffn1_in), bf(w_ffn1_out), bf(w_ffn2_in), bf(w_ffn2_out)
    w_in_b, w_out_b, w_cq_b, w_ckv_b, w_co_b = bf(w_in), bf(w_out), bf(w_cq), bf(w_ckv), bf(w_co)
    groups = len(POOL_WINDOWS)
    eye = jnp.eye(groups, dtype=F32)
    pool_bd = bf((pool_w[:, :, :, None, :] * eye[None, :, None, :, None])
                 .reshape(DEPTH, POOL_WIDTH, POOL_WIDTH))
    dw_pad = jnp.pad(conv_dw_w, ((0, 0), (0, CONV_HALO - CONV_K), (0, 0)))
    mixer_w = (pool_bd, rowv(pool_scale), dw_pad, rowv(conv_dw_b), rowv(conv_ln_g), rowv(conv_ln_b),
               bf(conv_pw_w), rowv(conv_pw_b))
    ln1, lnm, lnc, lnmem, ln2 = rowv(ln_ffn1), rowv(ln_mix), rowv(ln_cross), rowv(ln_mem), rowv(ln_ffn2)

    tables_p = _rope_tables(jnp.arange(seq, dtype=jnp.int32))
    tm_s = dbsz * dseq
    tables_s = _rope_tables(past_len + jnp.arange(tm_s, dtype=jnp.int32) % dseq)

    c1 = POOL_WIDTH
    c3 = c1 + 2 * CONV_WIDTH
    c4 = c3 + ATT_WIDTH
    c5 = c4 + ATT_WIDTH
    mix_segs = [(0, c1, None, 0, False), (c1, 2 * CONV_WIDTH, None, 0, False),
                (c3, ATT_WIDTH, ATT_HEAD_DIM, 0, True), (c4, ATT_WIDTH, ATT_HEAD_DIM, 1, True),
                (c5, ATT_WIDTH, None, 0, False)]
    mem_segs = [(0, D_MODEL, MEM_HEAD_DIM, 0, False), (D_MODEL, D_MODEL, None, 0, False)]
    cq_segs = [(0, D_MODEL, MEM_HEAD_DIM, 0, False)]
    out_rows = (0, c1, c1 + CONV_WIDTH)

    cache_k = cache_att_k.reshape(DEPTH * n_phys, PAGE_SIZE, ATT_WIDTH)
    cache_v = cache_att_v.reshape(DEPTH * n_phys, PAGE_SIZE, ATT_WIDTH)
    mem_k_s = cache_mem_k.reshape(DEPTH * dbsz, N_MEM, D_MODEL)
    mem_v_s = cache_mem_v.reshape(DEPTH * dbsz, N_MEM, D_MODEL)
    mem_rows = mem_prompt.reshape(bsz * N_MEM, D_MODEL)

    xp = x_prompt.reshape(bsz * seq, D_MODEL)
    xs = x_sample.reshape(dbsz * dseq, D_MODEL)
    rows_pad = SUBLANES
    outs = {name: [] for name in ("p_k", "p_v", "p_mk", "p_mv", "p_pool", "p_conv",
                                  "s_k", "s_v", "s_pool", "s_conv")}

    for l in range(DEPTH):
        att_gains = jnp.stack([att_q_norm[l], att_k_norm[l]])

        mk, mv = _norm_proj(mem_rows, lnmem, w_ckv_b, l, ck_norm[l][None], mem_segs)
        xp = _ffn(xp, ln1, w1i, w1o, l)
        u, ag, q, k, v = _norm_proj(xp, lnm, w_in_b, l, att_gains, mix_segs, tables=tables_p)
        y_att = _moba_prompt(q.reshape(bsz, seq, ATT_WIDTH), k.reshape(bsz, seq, ATT_WIDTH),
                             v.reshape(bsz, seq, ATT_WIDTH)).reshape(bsz * seq, ATT_WIDTH)
        y_pool, y_conv, h_tail = _poolconv_prompt(u, ag, bsz, seq, mixer_w, l)
        xp = _out_proj(xp, [y_pool, y_conv, y_att], w_out_b, l, out_rows)
        (qc,) = _norm_proj(xp, lnc, w_cq_b, l, cq_norm[l][None], cq_segs)
        oc = _cross(qc.reshape(bsz, seq, D_MODEL), mk.reshape(bsz, N_MEM, D_MODEL),
                    mv.reshape(bsz, N_MEM, D_MODEL), 0, 1, 512)
        xp = _out_proj(xp, [oc.reshape(bsz * seq, D_MODEL)], w_co_b, l, (0,))
        xp = _ffn(xp, ln2, w2i, w2o, l)
        outs["p_k"].append(k.reshape(bsz, seq, ATT_HEADS, ATT_HEAD_DIM))
        outs["p_v"].append(v.reshape(bsz, seq, ATT_HEADS, ATT_HEAD_DIM))
        outs["p_mk"].append(mk.reshape(bsz, N_MEM, MEM_HEADS, MEM_HEAD_DIM))
        outs["p_mv"].append(mv.reshape(bsz, N_MEM, MEM_HEADS, MEM_HEAD_DIM))
        outs["p_pool"].append(u.reshape(bsz, seq, POOL_WIDTH)[:, seq - POOL_BUF:])
        outs["p_conv"].append(h_tail[:, CONV_HALO - CONV_BUF:])

        xs = _ffn(xs, ln1, w1i, w1o, l, tm=tm_s)
        u, ag, q, k, v = _norm_proj(xs, lnm, w_in_b, l, att_gains, mix_segs, tables=tables_s, tm=tm_s)
        pad3 = lambda a: _pad_rows(a.reshape(dbsz, dseq, ATT_WIDTH), rows_pad)
        y_att = _moba_sample(pad3(q), pad3(k), pad3(v), cache_k, cache_v, page_table, l)
        y_att = y_att[:, :dseq].reshape(dbsz * dseq, ATT_WIDTH)
        tmaj = lambda a: jnp.swapaxes(a, 0, 1)
        u3 = u.reshape(dbsz, dseq, POOL_WIDTH)
        y_pool, y_conv, h_new = _poolconv_sample(
            tmaj(state_pool[l]), tmaj(u3), tmaj(state_conv[l]), tmaj(ag.reshape(dbsz, dseq, 2 * CONV_WIDTH)),
            past_len, mixer_w, l)
        bmaj = lambda a: jnp.swapaxes(a, 0, 1).reshape(dbsz * dseq, a.shape[-1])
        xs = _out_proj(xs, [bmaj(y_pool), bmaj(y_conv), y_att], w_out_b, l, out_rows, tm=tm_s)
        (qc,) = _norm_proj(xs, lnc, w_cq_b, l, cq_norm[l][None], cq_segs, tm=tm_s)
        n_seq = 4
        oc = _cross(_pad_rows(qc.reshape(dbsz, dseq, D_MODEL), rows_pad), mem_k_s, mem_v_s,
                    l * dbsz, n_seq, rows_pad)
        xs = _out_proj(xs, [oc[:, :dseq].reshape(dbsz * dseq, D_MODEL)], w_co_b, l, (0,), tm=tm_s)
        xs = _ffn(xs, ln2, w2i, w2o, l, tm=tm_s)
        outs["s_k"].append(k.reshape(dbsz, dseq, ATT_HEADS, ATT_HEAD_DIM))
        outs["s_v"].append(v.reshape(dbsz, dseq, ATT_HEADS, ATT_HEAD_DIM))
        outs["s_pool"].append(jnp.concatenate([state_pool[l][:, dseq:], u3], axis=1))
        outs["s_conv"].append(jnp.concatenate([state_conv[l][:, dseq:], jnp.swapaxes(h_new, 0, 1)], axis=1))

    st = lambda name: jnp.stack(outs[name])
    return (xp.reshape(bsz, seq, D_MODEL), xs.reshape(dbsz, dseq, D_MODEL),
            st("p_k"), st("p_v"), st("p_mk"), st("p_mv"), st("p_pool"), st("p_conv"),
            st("s_k"), st("s_v"), st("s_pool"), st("s_conv"))
```

```python
import functools

import jax
import jax.numpy as jnp
import numpy as np
from jax import lax
from jax.experimental import pallas as pl
from jax.experimental.pallas import tpu as pltpu

D_MODEL = 1024
DEPTH = 2
PAGE_SIZE = 128

POOL_WIDTH = D_MODEL // 4
POOL_WINDOWS = (2, 4, 8, 16)
POOL_GROUP_WIDTH = POOL_WIDTH // len(POOL_WINDOWS)
POOL_BUF = max(POOL_WINDOWS) - 1

CONV_WIDTH = D_MODEL // 4
CONV_K = 31
CONV_BUF = CONV_K - 1

ATT_HEADS = 4
ATT_HEAD_DIM = D_MODEL // 8
ATT_WIDTH = ATT_HEADS * ATT_HEAD_DIM
ROT_DIM = ATT_HEAD_DIM // 4
ROPE_THETA = 500000.0
MOBA_BLOCK = 256
MOBA_TOPK = 3

IN_WIDTH = POOL_WIDTH + 2 * CONV_WIDTH + 3 * ATT_WIDTH

N_MEM = 256
MEM_HEADS = 4
MEM_HEAD_DIM = D_MODEL // MEM_HEADS

D_FF = ((8 * D_MODEL // 3 + 127) // 128) * 128
EPS = 1e-6

LANES = 128
SUBLANES = 8
VMEM_LIMIT_BYTES = 56 * 1024 * 1024

F32 = jnp.float32
BF16 = jnp.bfloat16
NEG_INF = float("-inf")

_NT = (((1,), (1,)), ((), ()))


def _params(n_axes):
    return pltpu.CompilerParams(dimension_semantics=("arbitrary",) * n_axes,
                                vmem_limit_bytes=VMEM_LIMIT_BYTES)


def _rms(x, g):
    return x * lax.rsqrt(jnp.mean(x * x, axis=-1, keepdims=True) + EPS) * g


def _layer_spec(shape, l, n_grid):
    zeros = (0,) * len(shape)
    if n_grid == 1:
        return pl.BlockSpec((None,) + tuple(shape), lambda i: (l,) + zeros)
    if n_grid == 2:
        return pl.BlockSpec((None,) + tuple(shape), lambda i, j: (l,) + zeros)
    return pl.BlockSpec((None,) + tuple(shape), lambda i, j, k: (l,) + zeros)


FFN_CHUNK = 256


def _ffn_kernel(x_ref, g_ref, wi_ref, wo_ref, o_ref):
    x = x_ref[...]
    xn = _rms(x, g_ref[...]).astype(BF16)
    acc = None
    for c in range(D_FF // FFN_CHUNK):
        lo = c * FFN_CHUNK
        gate = jnp.dot(xn, wi_ref[:, lo:lo + FFN_CHUNK], preferred_element_type=F32)
        up = jnp.dot(xn, wi_ref[:, D_FF + lo:D_FF + lo + FFN_CHUNK], preferred_element_type=F32)
        act = (gate * jax.nn.sigmoid(gate) * up).astype(BF16)
        part = jnp.dot(act, wo_ref[lo:lo + FFN_CHUNK, :], preferred_element_type=F32)
        acc = part if acc is None else acc + part
    o_ref[...] = x + 0.5 * acc


def _ffn(x, ln, w_in, w_out, l, tm=512):
    m = x.shape[0]
    return pl.pallas_call(
        _ffn_kernel,
        grid=(m // tm,),
        in_specs=[
            pl.BlockSpec((tm, D_MODEL), lambda i: (i, 0)),
            _layer_spec((1, D_MODEL), l, 1),
            _layer_spec((D_MODEL, 2 * D_FF), l, 1),
            _layer_spec((D_FF, D_MODEL), l, 1),
        ],
        out_specs=pl.BlockSpec((tm, D_MODEL), lambda i: (i, 0)),
        out_shape=jax.ShapeDtypeStruct((m, D_MODEL), F32),
        compiler_params=_params(1),
        name="ffn",
    )(x, ln, w_in, w_out)


def _norm_proj_kernel(*refs, segs, rope):
    x_ref, g_ref, w_ref, hg_ref = refs[:4]
    pos = 4
    if rope:
        cos_ref, sin_lo_ref, sin_hi_ref = refs[4:7]
        pos = 7
    out_refs = refs[pos:]
    xn = _rms(x_ref[...], g_ref[...]).astype(BF16)
    for (c0, width, head_dim, gain_row, use_rope), o_ref in zip(segs, out_refs):
        z = jnp.dot(xn, w_ref[:, c0:c0 + width], preferred_element_type=F32)
        if head_dim is None:
            o_ref[...] = z
            continue
        gain = hg_ref[gain_row:gain_row + 1, :]
        for h in range(width // head_dim):
            zh = _rms(z[:, h * head_dim:(h + 1) * head_dim], gain)
            if use_rope:
                half = ROT_DIM // 2
                zh = (zh * cos_ref[...]
                      + pltpu.roll(zh, half, 1) * sin_hi_ref[...]
                      + pltpu.roll(zh, LANES - half, 1) * sin_lo_ref[...])
            o_ref[:, h * head_dim:(h + 1) * head_dim] = zh


def _norm_proj(x, ln, w, l, head_gains, segs, tables=None, tm=512):
    m = x.shape[0]
    n_cols = w.shape[-1]
    rope = tables is not None
    in_specs = [
        pl.BlockSpec((tm, D_MODEL), lambda i: (i, 0)),
        _layer_spec((1, D_MODEL), l, 1),
        _layer_spec((D_MODEL, n_cols), l, 1),
        pl.BlockSpec(head_gains.shape, lambda i: (0, 0)),
    ]
    args = [x, ln, w, head_gains]
    if rope:
        period = tables[0].shape[0] // tm
        for t in tables:
            in_specs.append(pl.BlockSpec((tm, LANES), lambda i: (i % period, 0)))
            args.append(t)
    return pl.pallas_call(
        functools.partial(_norm_proj_kernel, segs=tuple(segs), rope=rope),
        grid=(m // tm,),
        in_specs=in_specs,
        out_specs=[pl.BlockSpec((tm, s[1]), lambda i: (i, 0)) for s in segs],
        out_shape=[jax.ShapeDtypeStruct((m, s[1]), F32) for s in segs],
        compiler_params=_params(1),
        name="norm_proj",
    )(*args)


def _rope_tables(positions):
    half = ROT_DIM // 2
    inv_freq = jnp.power(jnp.float32(ROPE_THETA), -jnp.arange(half, dtype=F32) / half)
    ang = positions.astype(F32)[:, None] * inv_freq[None, :]
    cos, sin = jnp.cos(ang), jnp.sin(ang)
    n = positions.shape[0]
    rest = ATT_HEAD_DIM - ROT_DIM
    cos_t = jnp.concatenate([cos, cos, jnp.ones((n, rest), F32)], axis=1)
    sin_lo = jnp.concatenate([-sin, jnp.zeros((n, half + rest), F32)], axis=1)
    sin_hi = jnp.concatenate([jnp.zeros((n, half), F32), sin, jnp.zeros((n, rest), F32)], axis=1)
    return cos_t, sin_lo, sin_hi


def _out_proj_kernel(*refs, n_terms):
    x_ref = refs[0]
    y_refs = refs[1:1 + n_terms]
    w_refs = refs[1 + n_terms:1 + 2 * n_terms]
    o_ref = refs[1 + 2 * n_terms]
    acc = x_ref[...]
    for y_ref, w_ref in zip(y_refs, w_refs):
        acc = acc + jnp.dot(y_ref[...].astype(BF16), w_ref[...], preferred_element_type=F32)
    o_ref[...] = acc


def _out_proj(x, ys, w, l, row_starts, tm=512):
    m = x.shape[0]
    in_specs = [pl.BlockSpec((tm, D_MODEL), lambda i: (i, 0))]
    for y in ys:
        in_specs.append(pl.BlockSpec((tm, y.shape[1]), lambda i: (i, 0)))
    for y, r0 in zip(ys, row_starts):
        width = y.shape[1]
        blk = r0 // width
        in_specs.append(pl.BlockSpec((None, width, D_MODEL), lambda i, blk=blk: (l, blk, 0)))
    return pl.pallas_call(
        functools.partial(_out_proj_kernel, n_terms=len(ys)),
        grid=(m // tm,),
        in_specs=in_specs,
        out_specs=pl.BlockSpec((tm, D_MODEL), lambda i: (i, 0)),
        out_shape=jax.ShapeDtypeStruct((m, D_MODEL), F32),
        compiler_params=_params(1),
        name="out_proj",
    )(x, *ys, *([w] * len(ys)))


def _select_blocks(gate, n_allowed):
    lane = lax.broadcasted_iota(jnp.int32, gate.shape, 1)
    allowed = lane < n_allowed
    g = jnp.where(allowed, gate, NEG_INF)
    rank = jnp.zeros(gate.shape, F32)
    for r in range(1, SUBLANES):
        lower = pltpu.roll(g, r, 1)
        higher = pltpu.roll(g, LANES - r, 1)
        rank = rank + jnp.where(lower >= g, 1.0, 0.0) + jnp.where(higher > g, 1.0, 0.0)
    return jnp.where(allowed, jnp.where(rank < MOBA_TOPK, 1.0, 0.0), 0.0)


def _block_expander(n_keys):
    blk = np.arange(n_keys) // MOBA_BLOCK
    return jnp.asarray((np.arange(LANES)[:, None] == blk[None, :]).astype(np.float32), dtype=BF16)


MASK_BIAS = -1e30


def _select_blocks_t(gate_t, n_allowed):
    blk = lax.broadcasted_iota(jnp.int32, gate_t.shape, 0)
    allowed = blk < n_allowed
    g = jnp.where(allowed, gate_t, NEG_INF)
    rank = jnp.zeros(gate_t.shape, F32)
    for m in range(SUBLANES):
        gm = g[m:m + 1, :]
        tie = jnp.where(gm == g, jnp.where(blk > m, 1.0, 0.0), 0.0)
        rank = rank + jnp.where(gm > g, 1.0, tie)
    return jnp.where(allowed, jnp.where(rank < MOBA_TOPK, 1.0, 0.0), 0.0)


def _moba_prompt_kernel(q_ref, k_ref, v_ref, kbias_ref, o_ref, ka_ref, vb_ref, km_ref, qa_ref, s_ref,
                        mrun_ref, lrun_ref, acc_ref, *, n_blocks):
    qi = pl.program_id(1)
    blk = MOBA_BLOCK
    dh = ATT_HEAD_DIM
    heads = range(ATT_HEADS)
    own_slot = n_blocks * blk

    @pl.when(qi == 0)
    def _():
        vb_ref[...] = v_ref[...].astype(BF16)
        for h in heads:
            ka_ref[h, :, 0:dh] = k_ref[:, h * dh:(h + 1) * dh].astype(BF16)
            ka_ref[h, :, dh:2 * dh] = kbias_ref[...]
            for n in range(n_blocks):
                km_ref[h, n:n + 1, :] = jnp.mean(k_ref[n * blk:(n + 1) * blk, h * dh:(h + 1) * dh],
                                                 axis=0, keepdims=True)

    own = pl.multiple_of(qi * blk, blk)
    row = lax.broadcasted_iota(jnp.int32, (blk, blk), 0)
    col = lax.broadcasted_iota(jnp.int32, (blk, blk), 1)
    for h in heads:
        q = q_ref[:, h * dh:(h + 1) * dh]
        gate_t = lax.dot_general(km_ref[h], q, _NT, precision=lax.Precision.HIGHEST,
                                 preferred_element_type=F32)
        unchosen_t = 1.0 - _select_blocks_t(gate_t, qi)
        unchosen = jnp.concatenate([unchosen_t, jnp.zeros((LANES - SUBLANES, blk), F32)], axis=0).T
        qs = (q * (dh ** -0.5)).astype(BF16)
        qa_ref[h, :, 0:dh] = qs
        qa_ref[h, :, dh:2 * dh] = unchosen.astype(BF16)
        s = lax.dot_general(qs, ka_ref[h, pl.ds(own, blk), 0:dh], _NT, preferred_element_type=F32)
        s = jnp.where(row >= col, s, NEG_INF)
        s_ref[h, :, own_slot:own_slot + blk] = s
        mrun_ref[h] = jnp.maximum(s[:, 0:LANES], s[:, LANES:blk])

    for n in range(n_blocks - 1):
        @pl.when(n < qi)
        def _(n=n):
            for h in heads:
                s = lax.dot_general(qa_ref[h], ka_ref[h, n * blk:(n + 1) * blk, :], _NT,
                                    preferred_element_type=F32)
                s_ref[h, :, n * blk:(n + 1) * blk] = s
                mrun_ref[h] = jnp.maximum(mrun_ref[h], jnp.maximum(s[:, 0:LANES], s[:, LANES:blk]))

    for h in heads:
        m = jnp.broadcast_to(jnp.max(mrun_ref[h], axis=-1, keepdims=True), (blk, LANES))
        mrun_ref[h] = m
        p0 = jnp.exp(s_ref[h, :, own_slot:own_slot + LANES] - m)
        p1 = jnp.exp(s_ref[h, :, own_slot + LANES:own_slot + blk] - m)
        lrun_ref[h] = p0 + p1
        p = jnp.concatenate([p0, p1], axis=1).astype(BF16)
        acc_ref[h] = jnp.dot(p, vb_ref[pl.ds(own, blk), h * dh:(h + 1) * dh], preferred_element_type=F32)

    for n in range(n_blocks - 1):
        @pl.when(n < qi)
        def _(n=n):
            for h in heads:
                m = mrun_ref[h]
                p0 = jnp.exp(s_ref[h, :, n * blk:n * blk + LANES] - m)
                p1 = jnp.exp(s_ref[h, :, n * blk + LANES:(n + 1) * blk] - m)
                lrun_ref[h] = lrun_ref[h] + (p0 + p1)
                p = jnp.concatenate([p0, p1], axis=1).astype(BF16)
                acc_ref[h] = acc_ref[h] + jnp.dot(p, vb_ref[n * blk:(n + 1) * blk, h * dh:(h + 1) * dh],
                                                  preferred_element_type=F32)

    for h in heads:
        o_ref[:, h * dh:(h + 1) * dh] = acc_ref[h] / jnp.sum(lrun_ref[h], axis=-1, keepdims=True)


def _moba_prompt(q, k, v, bsz, t):
    assert t % MOBA_BLOCK == 0 and t // MOBA_BLOCK <= SUBLANES
    n_blocks = t // MOBA_BLOCK
    dh = ATT_HEAD_DIM
    key_blk = np.arange(t) // MOBA_BLOCK
    kbias = jnp.asarray(np.where(key_blk[:, None] == np.arange(LANES)[None, :], MASK_BIAS, 0.0), dtype=BF16)
    return pl.pallas_call(
        functools.partial(_moba_prompt_kernel, n_blocks=n_blocks),
        grid=(bsz, n_blocks),
        in_specs=[
            pl.BlockSpec((MOBA_BLOCK, ATT_WIDTH), lambda b, i: (b * n_blocks + i, 0)),
            pl.BlockSpec((t, ATT_WIDTH), lambda b, i: (b, 0)),
            pl.BlockSpec((t, ATT_WIDTH), lambda b, i: (b, 0)),
            pl.BlockSpec((t, LANES), lambda b, i: (0, 0)),
        ],
        out_specs=pl.BlockSpec((MOBA_BLOCK, ATT_WIDTH), lambda b, i: (b * n_blocks + i, 0)),
        out_shape=jax.ShapeDtypeStruct((bsz * t, ATT_WIDTH), F32),
        scratch_shapes=[
            pltpu.VMEM((ATT_HEADS, t, 2 * dh), BF16),
            pltpu.VMEM((t, ATT_WIDTH), BF16),
            pltpu.VMEM((ATT_HEADS, SUBLANES, dh), F32),
            pltpu.VMEM((ATT_HEADS, MOBA_BLOCK, 2 * dh), BF16),
            pltpu.VMEM((ATT_HEADS, MOBA_BLOCK, (n_blocks + 1) * MOBA_BLOCK), F32),
            pltpu.VMEM((ATT_HEADS, MOBA_BLOCK, LANES), F32),
            pltpu.VMEM((ATT_HEADS, MOBA_BLOCK, LANES), F32),
            pltpu.VMEM((ATT_HEADS, MOBA_BLOCK, dh), F32),
        ],
        compiler_params=_params(2),
        name="moba_prompt",
    )(q, k, v, kbias)


def _moba_sample_kernel(pt_ref, q_ref, kn_ref, vn_ref, e_ref, *refs, n_pages):
    del pt_ref
    k_refs = refs[:n_pages]
    v_refs = refs[n_pages:2 * n_pages]
    o_ref = refs[2 * n_pages]
    rows = q_ref.shape[1]
    pages_per_block = MOBA_BLOCK // PAGE_SIZE
    n_blocks = n_pages // pages_per_block

    dh = ATT_HEAD_DIM
    heads = range(ATT_HEADS)

    def head_rows(page_ref, h):
        return page_ref[0, pl.ds(h, PAGE_SIZE, stride=ATT_HEADS), :]

    qs, gate_rows, s_rows = [], [], []
    for h in heads:
        q = q_ref[0, :, h * dh:(h + 1) * dh]
        qs.append((q * (dh ** -0.5)).astype(BF16))
        mean_rows, s_parts = [], []
        for n in range(n_blocks):
            total = None
            for j in range(n * pages_per_block, (n + 1) * pages_per_block):
                kp = head_rows(k_refs[j], h)
                part = jnp.sum(kp, axis=0, keepdims=True)
                total = part if total is None else total + part
                s_parts.append(lax.dot_general(qs[h], kp.astype(BF16), _NT, preferred_element_type=F32))
            mean_rows.append(total / MOBA_BLOCK)
        mean_rows.append(jnp.zeros((LANES - n_blocks, dh), F32))
        gate_rows.append(lax.dot_general(q, jnp.concatenate(mean_rows, axis=0), _NT,
                                         precision=lax.Precision.HIGHEST, preferred_element_type=F32))
        s_rows.append(jnp.concatenate(s_parts, axis=1))
    sel = _select_blocks(jnp.concatenate(gate_rows, axis=0), n_blocks)
    sel_keys = jnp.dot(sel.astype(BF16), e_ref[...], preferred_element_type=F32)
    s = jnp.where(sel_keys > 0.5, jnp.concatenate(s_rows, axis=0), NEG_INF)

    pad = jnp.zeros((LANES - rows, dh), F32)
    k_new = [jnp.concatenate([kn_ref[0, :, h * dh:(h + 1) * dh], pad], axis=0).astype(BF16) for h in heads]
    v_new = [jnp.concatenate([vn_ref[0, :, h * dh:(h + 1) * dh], pad], axis=0).astype(BF16) for h in heads]
    s_own = jnp.concatenate(
        [lax.dot_general(qs[h], k_new[h], _NT, preferred_element_type=F32) for h in heads], axis=0)
    t_row = lax.broadcasted_iota(jnp.int32, s_own.shape, 0) % rows
    key = lax.broadcasted_iota(jnp.int32, s_own.shape, 1)
    s_own = jnp.where(key <= t_row, s_own, NEG_INF)

    m = jnp.maximum(jnp.max(s, axis=-1, keepdims=True), jnp.max(s_own, axis=-1, keepdims=True))
    p = jnp.exp(s - m)
    p_own = jnp.exp(s_own - m)
    denom = jnp.sum(p, axis=-1, keepdims=True) + jnp.sum(p_own, axis=-1, keepdims=True)
    pb = p.astype(BF16)
    pb_own = p_own.astype(BF16)
    for h in heads:
        r0 = h * rows
        o = jnp.dot(pb_own[r0:r0 + rows], v_new[h], preferred_element_type=F32)
        for j in range(n_pages):
            o = o + jnp.dot(pb[r0:r0 + rows, j * PAGE_SIZE:(j + 1) * PAGE_SIZE],
                            head_rows(v_refs[j], h).astype(BF16), preferred_element_type=F32)
        o_ref[0, :, h * dh:(h + 1) * dh] = o / denom[r0:r0 + rows]


def _moba_sample(q, k_new, v_new, cache_k, cache_v, page_table, l):
    bsz, rows, _ = q.shape
    n_pages = page_table.shape[1]
    n_phys = cache_k.shape[0] // DEPTH
    past_len = n_pages * PAGE_SIZE
    assert past_len % MOBA_BLOCK == 0 and past_len // MOBA_BLOCK <= SUBLANES
    base = l * n_phys
    tok_spec = pl.BlockSpec((1, rows, ATT_WIDTH), lambda b, pt: (b, 0, 0))
    page_specs = [
        pl.BlockSpec((1, PAGE_SIZE * ATT_HEADS, ATT_HEAD_DIM),
                     lambda b, pt, j=j: (base + pt[b * n_pages + j], 0, 0))
        for j in range(n_pages)
    ]
    grid_spec = pltpu.PrefetchScalarGridSpec(
        num_scalar_prefetch=1,
        grid=(bsz,),
        in_specs=[tok_spec, tok_spec, tok_spec,
                  pl.BlockSpec((LANES, past_len), lambda b, pt: (0, 0))] + page_specs + page_specs,
        out_specs=tok_spec,
    )
    return pl.pallas_call(
        functools.partial(_moba_sample_kernel, n_pages=n_pages),
        grid_spec=grid_spec,
        out_shape=jax.ShapeDtypeStruct((bsz, rows, ATT_WIDTH), F32),
        compiler_params=_params(1),
        name="moba_sample",
    )(page_table.reshape(-1), q, k_new, v_new, _block_expander(past_len),
      *([cache_k] * n_pages), *([cache_v] * n_pages))


def _pool_windows_to_lanes(sums, counts, cur):
    lane = lax.broadcasted_iota(jnp.int32, cur.shape, 1)
    d = sums[-1] / counts[-1]
    for g in range(len(POOL_WINDOWS) - 2, -1, -1):
        d = jnp.where(lane < (g + 1) * POOL_GROUP_WIDTH, sums[g] / counts[g], d)
    return d - cur


def _conv_tail(y, ln_g, ln_b, pw_w, pw_b):
    yc = y - jnp.mean(y, axis=-1, keepdims=True)
    yn = yc * lax.rsqrt(jnp.mean(yc * yc, axis=-1, keepdims=True) + EPS) * ln_g + ln_b
    act = yn * jax.nn.sigmoid(yn)
    return jnp.dot(act.astype(BF16), pw_w, preferred_element_type=F32) + pw_b


POOL_HALO = 16
CONV_HALO = 32


def _poolconv_prompt_kernel(u_ref, ag_ref, pw_ref, ps_ref, dw_ref, db_ref, lg_ref, lb_ref, cw_ref, cb_ref,
                            yp_ref, yc_ref, ht_ref, e_ref, h_ref, *, tt):
    ti = pl.program_id(1)

    @pl.when(ti == 0)
    def _():
        e_ref[0:POOL_HALO, :] = jnp.zeros((POOL_HALO, POOL_WIDTH), F32)
        h_ref[0:CONV_HALO, :] = jnp.zeros((CONV_HALO, CONV_WIDTH), F32)

    u = u_ref[...]
    e_ref[POOL_HALO:POOL_HALO + tt, :] = u
    a = ag_ref[:, 0:CONV_WIDTH]
    gate = ag_ref[:, CONV_WIDTH:2 * CONV_WIDTH]
    h_ref[CONV_HALO:CONV_HALO + tt, :] = a * jax.nn.sigmoid(gate)

    pos = ti * tt + lax.broadcasted_iota(jnp.int32, (tt, 1), 0)
    sums, counts = [], []
    run = u
    taken = 1
    for win in POOL_WINDOWS:
        while taken < win:
            run = run + e_ref[POOL_HALO - taken:POOL_HALO - taken + tt, :]
            taken += 1
        sums.append(run)
        counts.append(jnp.minimum(pos + 1, win).astype(F32))
    d = _pool_windows_to_lanes(sums, counts, u)
    yp_ref[...] = jnp.dot(d.astype(BF16), pw_ref[...], preferred_element_type=F32) * ps_ref[...]

    y = jnp.zeros((tt, CONV_WIDTH), F32) + db_ref[...]
    for j in range(CONV_K):
        r0 = CONV_HALO - CONV_BUF + j
        y = y + h_ref[r0:r0 + tt, :] * dw_ref[j:j + 1, :]
    yc_ref[...] = _conv_tail(y, lg_ref[...], lb_ref[...], cw_ref[...], cb_ref[...])

    ht_ref[0] = h_ref[tt:tt + CONV_HALO, :]
    e_ref[0:POOL_HALO, :] = e_ref[tt:tt + POOL_HALO, :]
    h_ref[0:CONV_HALO, :] = h_ref[tt:tt + CONV_HALO, :]


def _poolconv_prompt(u, ag, bsz, t, weights, l, tt=512):
    pool_w, pool_s, dw_w, dw_b, ln_g, ln_b, pw_w, pw_b = weights
    nt = t // tt
    row = lambda b, i: (b * nt + i, 0)
    return pl.pallas_call(
        functools.partial(_poolconv_prompt_kernel, tt=tt),
        grid=(bsz, nt),
        in_specs=[
            pl.BlockSpec((tt, POOL_WIDTH), row),
            pl.BlockSpec((tt, 2 * CONV_WIDTH), row),
            _layer_spec((POOL_WIDTH, POOL_WIDTH), l, 2),
            _layer_spec((1, POOL_WIDTH), l, 2),
            _layer_spec((CONV_HALO, CONV_WIDTH), l, 2),
            _layer_spec((1, CONV_WIDTH), l, 2),
            _layer_spec((1, CONV_WIDTH), l, 2),
            _layer_spec((1, CONV_WIDTH), l, 2),
            _layer_spec((CONV_WIDTH, CONV_WIDTH), l, 2),
            _layer_spec((1, CONV_WIDTH), l, 2),
        ],
        out_specs=[
            pl.BlockSpec((tt, POOL_WIDTH), row),
            pl.BlockSpec((tt, CONV_WIDTH), row),
            pl.BlockSpec((1, CONV_HALO, CONV_WIDTH), lambda b, i: (b, 0, 0)),
        ],
        out_shape=[
            jax.ShapeDtypeStruct((bsz * t, POOL_WIDTH), F32),
            jax.ShapeDtypeStruct((bsz * t, CONV_WIDTH), F32),
            jax.ShapeDtypeStruct((bsz, CONV_HALO, CONV_WIDTH), F32),
        ],
        scratch_shapes=[
            pltpu.VMEM((POOL_HALO + tt, POOL_WIDTH), F32),
            pltpu.VMEM((CONV_HALO + tt, CONV_WIDTH), F32),
        ],
        compiler_params=_params(2),
        name="poolconv_prompt",
    )(u, ag, pool_w, pool_s, dw_w, dw_b, ln_g, ln_b, pw_w, pw_b)


def _poolconv_sample_kernel(sp_ref, u_ref, sc_ref, ag_ref, pw_ref, ps_ref, dw_ref, db_ref, lg_ref, lb_ref,
                            cw_ref, cb_ref, yp_ref, yc_ref, h_ref, *, n_new, pos0):
    def pool_row(r):
        return sp_ref[r] if r < POOL_BUF else u_ref[r - POOL_BUF]

    for t in range(n_new):
        h_ref[t] = ag_ref[t, :, 0:CONV_WIDTH] * jax.nn.sigmoid(ag_ref[t, :, CONV_WIDTH:2 * CONV_WIDTH])

    def conv_row(r):
        return sc_ref[r] if r < CONV_BUF else h_ref[r - CONV_BUF]

    for t in range(n_new):
        cur = u_ref[t]
        sums, counts = [], []
        run = cur
        taken = 1
        for win in POOL_WINDOWS:
            while taken < win:
                run = run + pool_row(POOL_BUF + t - taken)
                taken += 1
            sums.append(run)
            counts.append(float(min(pos0 + t + 1, win)))
        d = _pool_windows_to_lanes(sums, counts, cur)
        yp_ref[t] = jnp.dot(d.astype(BF16), pw_ref[...], preferred_element_type=F32) * ps_ref[...]

        y = jnp.zeros(cur.shape, F32) + db_ref[...]
        for j in range(CONV_K):
            y = y + conv_row(t + j) * dw_ref[j:j + 1, :]
        yc_ref[t] = _conv_tail(y, lg_ref[...], lb_ref[...], cw_ref[...], cb_ref[...])


def _poolconv_sample(state_pool_t, u_t, state_conv_t, ag_t, pos0, weights, l):
    pool_w, pool_s, dw_w, dw_b, ln_g, ln_b, pw_w, pw_b = weights
    n_new, bsz, _ = u_t.shape
    full = lambda a: pl.BlockSpec(a.shape, lambda i: (0,) * a.ndim)
    out = jax.ShapeDtypeStruct((n_new, bsz, CONV_WIDTH), F32)
    return pl.pallas_call(
        functools.partial(_poolconv_sample_kernel, n_new=n_new, pos0=pos0),
        grid=(1,),
        in_specs=[
            full(state_pool_t), full(u_t), full(state_conv_t), full(ag_t),
            _layer_spec((POOL_WIDTH, POOL_WIDTH), l, 1),
            _layer_spec((1, POOL_WIDTH), l, 1),
            _layer_spec((CONV_HALO, CONV_WIDTH), l, 1),
            _layer_spec((1, CONV_WIDTH), l, 1),
            _layer_spec((1, CONV_WIDTH), l, 1),
            _layer_spec((1, CONV_WIDTH), l, 1),
            _layer_spec((CONV_WIDTH, CONV_WIDTH), l, 1),
            _layer_spec((1, CONV_WIDTH), l, 1),
        ],
        out_specs=[pl.BlockSpec(out.shape, lambda i: (0, 0, 0))] * 3,
        out_shape=[out, out, out],
        compiler_params=_params(1),
        name="poolconv_sample",
    )(state_pool_t, u_t, state_conv_t, ag_t, pool_w, pool_s, dw_w, dw_b, ln_g, ln_b, pw_w, pw_b)


def _cross_kernel(q_ref, k_ref, v_ref, o_ref, *, n_seq, interleaved):
    hd = MEM_HEAD_DIM
    for g in range(n_seq):
        for h in range(MEM_HEADS):
            cols = slice(h * hd, (h + 1) * hd)
            if interleaved:
                halves = range(hd // LANES)
                pick = lambda ref, half: ref[g, pl.ds(half * MEM_HEADS + h, N_MEM, stride=len(halves) * MEM_HEADS), :]
                k = jnp.concatenate([pick(k_ref, half) for half in halves], axis=1)
                v = jnp.concatenate([pick(v_ref, half) for half in halves], axis=1)
            else:
                k = k_ref[g, :, cols]
                v = v_ref[g, :, cols]
            q = (q_ref[g, :, cols] * (hd ** -0.5)).astype(BF16)
            s = lax.dot_general(q, k.astype(BF16), _NT, preferred_element_type=F32)
            p = jnp.exp(s - jnp.max(s, axis=-1, keepdims=True))
            denom = jnp.sum(p, axis=-1, keepdims=True)
            o = jnp.dot(p.astype(BF16), v.astype(BF16), preferred_element_type=F32)
            o_ref[g, :, cols] = o / denom


def _interleaved_memory(cache):
    depth, s, n_mem, heads, hd = cache.shape
    halves = hd // LANES
    c = cache.reshape(depth, s, n_mem, heads, halves, LANES).transpose(0, 1, 2, 4, 3, 5)
    return c.reshape(depth * s, n_mem * halves * heads, LANES)


def _cross(q, mem_k, mem_v, n_seq, tq, seq0=None):
    s_total, r, _ = q.shape
    interleaved = seq0 is not None
    if interleaved:
        assert seq0 % n_seq == 0
        kv_spec = pl.BlockSpec((n_seq,) + mem_k.shape[1:], lambda s, i: (seq0 // n_seq + s, 0, 0))
    else:
        kv_spec = pl.BlockSpec((n_seq, N_MEM, D_MODEL), lambda s, i: (s, 0, 0))
    q_spec = pl.BlockSpec((n_seq, tq, D_MODEL), lambda s, i: (s, i, 0))
    return pl.pallas_call(
        functools.partial(_cross_kernel, n_seq=n_seq, interleaved=interleaved),
        grid=(s_total // n_seq, r // tq),
        in_specs=[q_spec, kv_spec, kv_spec],
        out_specs=q_spec,
        out_shape=jax.ShapeDtypeStruct(q.shape, F32),
        compiler_params=_params(2),
        name="cross",
    )(q, mem_k, mem_v)


def _pad_rows(x, rows):
    return jnp.pad(x, ((0, 0), (0, rows - x.shape[1]), (0, 0)))


def kernel(x_prompt, x_sample, cache_att_k, cache_att_v, cache_mem_k, cache_mem_v, state_pool, state_conv, page_table, mem_prompt, ln_ffn1, w_ffn1_in, w_ffn1_out, ln_mix, w_in, pool_w, pool_scale, conv_dw_w, conv_dw_b, conv_ln_g, conv_ln_b, conv_pw_w, conv_pw_b, att_q_norm, att_k_norm, w_out, ln_cross, ln_mem, w_cq, w_ckv, cq_norm, ck_norm, w_co, ln_ffn2, w_ffn2_in, w_ffn2_out):
    bsz, seq, _ = x_prompt.shape
    dbsz, dseq, _ = x_sample.shape
    n_pages = page_table.shape[1]
    past_len = n_pages * PAGE_SIZE
    n_phys = cache_att_k.shape[1]

    bf = lambda w: w.astype(BF16)
    rowv = lambda v: v[:, None, :]
    w1i, w1o, w2i, w2o = bf(w_ffn1_in), bf(w_ffn1_out), bf(w_ffn2_in), bf(w_ffn2_out)
    w_in_b, w_out_b, w_cq_b, w_ckv_b, w_co_b = bf(w_in), bf(w_out), bf(w_cq), bf(w_ckv), bf(w_co)
    groups = len(POOL_WINDOWS)
    eye = jnp.eye(groups, dtype=F32)
    pool_bd = bf((pool_w[:, :, :, None, :] * eye[None, :, None, :, None])
                 .reshape(DEPTH, POOL_WIDTH, POOL_WIDTH))
    dw_pad = jnp.pad(conv_dw_w, ((0, 0), (0, CONV_HALO - CONV_K), (0, 0)))
    mixer_w = (pool_bd, rowv(pool_scale), dw_pad, rowv(conv_dw_b), rowv(conv_ln_g), rowv(conv_ln_b),
               bf(conv_pw_w), rowv(conv_pw_b))
    ln1, lnm, lnc, lnmem, ln2 = rowv(ln_ffn1), rowv(ln_mix), rowv(ln_cross), rowv(ln_mem), rowv(ln_ffn2)

    tables_p = _rope_tables(jnp.arange(seq, dtype=jnp.int32))
    tm_s = dbsz * dseq
    tables_s = _rope_tables(past_len + jnp.arange(tm_s, dtype=jnp.int32) % dseq)

    c1 = POOL_WIDTH
    c3 = c1 + 2 * CONV_WIDTH
    c4 = c3 + ATT_WIDTH
    c5 = c4 + ATT_WIDTH
    mix_segs = [(0, c1, None, 0, False), (c1, 2 * CONV_WIDTH, None, 0, False),
                (c3, ATT_WIDTH, ATT_HEAD_DIM, 0, True), (c4, ATT_WIDTH, ATT_HEAD_DIM, 1, True),
                (c5, ATT_WIDTH, None, 0, False)]
    mem_segs = [(0, D_MODEL, MEM_HEAD_DIM, 0, False), (D_MODEL, D_MODEL, None, 0, False)]
    cq_segs = [(0, D_MODEL, MEM_HEAD_DIM, 0, False)]
    out_rows = (0, c1, c1 + CONV_WIDTH)

    cache_k = cache_att_k.reshape(DEPTH * n_phys, PAGE_SIZE * ATT_HEADS, ATT_HEAD_DIM)
    cache_v = cache_att_v.reshape(DEPTH * n_phys, PAGE_SIZE * ATT_HEADS, ATT_HEAD_DIM)
    mem_k_s = _interleaved_memory(cache_mem_k)
    mem_v_s = _interleaved_memory(cache_mem_v)
    mem_rows = mem_prompt.reshape(bsz * N_MEM, D_MODEL)

    xp = x_prompt.reshape(bsz * seq, D_MODEL)
    xs = x_sample.reshape(dbsz * dseq, D_MODEL)
    rows_pad = SUBLANES
    outs = {name: [] for name in ("p_k", "p_v", "p_mk", "p_mv", "p_pool", "p_conv",
                                  "s_k", "s_v", "s_pool", "s_conv")}

    for l in range(DEPTH):
        att_gains = jnp.stack([att_q_norm[l], att_k_norm[l]])

        mk, mv = _norm_proj(mem_rows, lnmem, w_ckv_b, l, ck_norm[l][None], mem_segs)
        xp = _ffn(xp, ln1, w1i, w1o, l)
        u, ag, q, k, v = _norm_proj(xp, lnm, w_in_b, l, att_gains, mix_segs, tables=tables_p)
        y_att = _moba_prompt(q, k, v, bsz, seq)
        y_pool, y_conv, h_tail = _poolconv_prompt(u, ag, bsz, seq, mixer_w, l)
        xp = _out_proj(xp, [y_pool, y_conv, y_att], w_out_b, l, out_rows)
        (qc,) = _norm_proj(xp, lnc, w_cq_b, l, cq_norm[l][None], cq_segs)
        oc = _cross(qc.reshape(bsz, seq, D_MODEL), mk.reshape(bsz, N_MEM, D_MODEL),
                    mv.reshape(bsz, N_MEM, D_MODEL), 1, 512)
        xp = _out_proj(xp, [oc.reshape(bsz * seq, D_MODEL)], w_co_b, l, (0,))
        xp = _ffn(xp, ln2, w2i, w2o, l)
        outs["p_k"].append(k.reshape(bsz, seq, ATT_HEADS, ATT_HEAD_DIM))
        outs["p_v"].append(v.reshape(bsz, seq, ATT_HEADS, ATT_HEAD_DIM))
        outs["p_mk"].append(mk.reshape(bsz, N_MEM, MEM_HEADS, MEM_HEAD_DIM))
        outs["p_mv"].append(mv.reshape(bsz, N_MEM, MEM_HEADS, MEM_HEAD_DIM))
        outs["p_pool"].append(u.reshape(bsz, seq, POOL_WIDTH)[:, seq - POOL_BUF:])
        outs["p_conv"].append(h_tail[:, CONV_HALO - CONV_BUF:])

        xs = _ffn(xs, ln1, w1i, w1o, l, tm=tm_s)
        u, ag, q, k, v = _norm_proj(xs, lnm, w_in_b, l, att_gains, mix_segs, tables=tables_s, tm=tm_s)
        pad3 = lambda a: _pad_rows(a.reshape(dbsz, dseq, ATT_WIDTH), 2 * SUBLANES)
        y_att = _moba_sample(pad3(q), pad3(k), pad3(v), cache_k, cache_v, page_table, l)
        y_att = y_att[:, :dseq].reshape(dbsz * dseq, ATT_WIDTH)
        tmaj = lambda a: jnp.swapaxes(a, 0, 1)
        u3 = u.reshape(dbsz, dseq, POOL_WIDTH)
        y_pool, y_conv, h_new = _poolconv_sample(
            tmaj(state_pool[l]), tmaj(u3), tmaj(state_conv[l]), tmaj(ag.reshape(dbsz, dseq, 2 * CONV_WIDTH)),
            past_len, mixer_w, l)
        bmaj = lambda a: jnp.swapaxes(a, 0, 1).reshape(dbsz * dseq, a.shape[-1])
        xs = _out_proj(xs, [bmaj(y_pool), bmaj(y_conv), y_att], w_out_b, l, out_rows, tm=tm_s)
        (qc,) = _norm_proj(xs, lnc, w_cq_b, l, cq_norm[l][None], cq_segs, tm=tm_s)
        n_seq = 4
        oc = _cross(_pad_rows(qc.reshape(dbsz, dseq, D_MODEL), rows_pad), mem_k_s, mem_v_s,
                    n_seq, rows_pad, seq0=l * dbsz)
        xs = _out_proj(xs, [oc[:, :dseq].reshape(dbsz * dseq, D_MODEL)], w_co_b, l, (0,), tm=tm_s)
        xs = _ffn(xs, ln2, w2i, w2o, l, tm=tm_s)
        outs["s_k"].append(k.reshape(dbsz, dseq, ATT_HEADS, ATT_HEAD_DIM))
        outs["s_v"].append(v.reshape(dbsz, dseq, ATT_HEADS, ATT_HEAD_DIM))
        outs["s_pool"].append(jnp.concatenate([state_pool[l][:, dseq:], u3], axis=1))
        outs["s_conv"].append(jnp.concatenate([state_conv[l][:, dseq:], jnp.swapaxes(h_new, 0, 1)], axis=1))

    st = lambda name: jnp.stack(outs[name])
    return (xp.reshape(bsz, seq, D_MODEL), xs.reshape(dbsz, dseq, D_MODEL),
            st("p_k"), st("p_v"), st("p_mk"), st("p_mv"), st("p_pool"), st("p_conv"),
            st("s_k"), st("s_v"), st("s_pool"), st("s_conv"))
```

```python
import functools

import jax
import jax.numpy as jnp
import numpy as np
from jax import lax
from jax.experimental import pallas as pl
from jax.experimental.pallas import tpu as pltpu

D_MODEL = 1024
DEPTH = 2
PAGE_SIZE = 128

POOL_WIDTH = D_MODEL // 4
POOL_WINDOWS = (2, 4, 8, 16)
POOL_GROUP_WIDTH = POOL_WIDTH // len(POOL_WINDOWS)
POOL_BUF = max(POOL_WINDOWS) - 1

CONV_WIDTH = D_MODEL // 4
CONV_K = 31
CONV_BUF = CONV_K - 1

ATT_HEADS = 4
ATT_HEAD_DIM = D_MODEL // 8
ATT_WIDTH = ATT_HEADS * ATT_HEAD_DIM
ROT_DIM = ATT_HEAD_DIM // 4
ROPE_THETA = 500000.0
MOBA_BLOCK = 256
MOBA_TOPK = 3

IN_WIDTH = POOL_WIDTH + 2 * CONV_WIDTH + 3 * ATT_WIDTH

N_MEM = 256
MEM_HEADS = 4
MEM_HEAD_DIM = D_MODEL // MEM_HEADS

D_FF = ((8 * D_MODEL // 3 + 127) // 128) * 128
EPS = 1e-6

LANES = 128
SUBLANES = 8
VMEM_LIMIT_BYTES = 56 * 1024 * 1024

F32 = jnp.float32
BF16 = jnp.bfloat16
NEG_INF = float("-inf")

_NT = (((1,), (1,)), ((), ()))


def _params(n_axes):
    return pltpu.CompilerParams(dimension_semantics=("arbitrary",) * n_axes,
                                vmem_limit_bytes=VMEM_LIMIT_BYTES)


def _rms(x, g):
    return x * lax.rsqrt(jnp.mean(x * x, axis=-1, keepdims=True) + EPS) * g


def _layer_spec(shape, l, n_grid):
    zeros = (0,) * len(shape)
    if n_grid == 1:
        return pl.BlockSpec((None,) + tuple(shape), lambda i: (l,) + zeros)
    if n_grid == 2:
        return pl.BlockSpec((None,) + tuple(shape), lambda i, j: (l,) + zeros)
    return pl.BlockSpec((None,) + tuple(shape), lambda i, j, k: (l,) + zeros)


FFN_CHUNK = 256


def _ffn_kernel(x_ref, g_ref, wi_ref, wo_ref, o_ref):
    x = x_ref[...]
    xn = _rms(x, g_ref[...]).astype(BF16)
    acc = None
    for c in range(D_FF // FFN_CHUNK):
        lo = c * FFN_CHUNK
        gate = jnp.dot(xn, wi_ref[:, lo:lo + FFN_CHUNK], preferred_element_type=F32)
        up = jnp.dot(xn, wi_ref[:, D_FF + lo:D_FF + lo + FFN_CHUNK], preferred_element_type=F32)
        act = (gate * jax.nn.sigmoid(gate) * up).astype(BF16)
        part = jnp.dot(act, wo_ref[lo:lo + FFN_CHUNK, :], preferred_element_type=F32)
        acc = part if acc is None else acc + part
    o_ref[...] = x + 0.5 * acc


def _ffn(x, ln, w_in, w_out, l, tm=512):
    m = x.shape[0]
    return pl.pallas_call(
        _ffn_kernel,
        grid=(m // tm,),
        in_specs=[
            pl.BlockSpec((tm, D_MODEL), lambda i: (i, 0)),
            _layer_spec((1, D_MODEL), l, 1),
            _layer_spec((D_MODEL, 2 * D_FF), l, 1),
            _layer_spec((D_FF, D_MODEL), l, 1),
        ],
        out_specs=pl.BlockSpec((tm, D_MODEL), lambda i: (i, 0)),
        out_shape=jax.ShapeDtypeStruct((m, D_MODEL), F32),
        compiler_params=_params(1),
        name="ffn",
    )(x, ln, w_in, w_out)


def _norm_proj_kernel(*refs, segs, rope):
    x_ref, g_ref, w_ref, hg_ref = refs[:4]
    if rope:
        cos_ref, sin_lo_ref, sin_hi_ref = refs[4:7]
    out_refs = refs[len(refs) - len(segs):]
    tm = x_ref.shape[0]
    xn = _rms(x_ref[...], g_ref[...]).astype(BF16)
    for (c0, width, head_dim, gain_row, use_rope, head_rows), o_ref in zip(segs, out_refs):
        z = jnp.dot(xn, w_ref[:, c0:c0 + width], preferred_element_type=F32)
        if head_dim is None:
            o_ref[...] = z
            continue
        n_heads = width // head_dim
        for h in range(n_heads):
            zh = z[:, h * head_dim:(h + 1) * head_dim]
            if gain_row is not None:
                zh = _rms(zh, hg_ref[gain_row:gain_row + 1, :])
            if use_rope:
                half = ROT_DIM // 2
                zh = (zh * cos_ref[...]
                      + pltpu.roll(zh, half, 1) * sin_hi_ref[...]
                      + pltpu.roll(zh, LANES - half, 1) * sin_lo_ref[...])
            if not head_rows:
                o_ref[:, h * head_dim:(h + 1) * head_dim] = zh
                continue
            rows = pl.ds(h, tm, stride=n_heads)
            if len(o_ref.shape) == 2:
                o_ref[rows, :] = zh
            else:
                for d in range(o_ref.shape[0]):
                    o_ref[d, rows, :] = zh


def _norm_proj(x, ln, w, l, head_gains, segs, tables=None, tm=512, stacked=None):
    m = x.shape[0]
    n_cols = w.shape[-1]
    rope = tables is not None
    in_specs = [
        pl.BlockSpec((tm, D_MODEL), lambda i: (i, 0)),
        _layer_spec((1, D_MODEL), l, 1),
        _layer_spec((D_MODEL, n_cols), l, 1),
        pl.BlockSpec(head_gains.shape, lambda i: (0, 0)),
    ]
    args = [x, ln, w, head_gains]
    if rope:
        period = tables[0].shape[0] // tm
        for t in tables:
            in_specs.append(pl.BlockSpec((tm, LANES), lambda i: (i % period, 0)))
            args.append(t)
    out_specs, out_shape, aliases = [], [], {}
    for k, s in enumerate(segs):
        width, head_dim, head_rows = s[1], s[2], s[5]
        if not head_rows:
            out_specs.append(pl.BlockSpec((tm, width), lambda i: (i, 0)))
            out_shape.append(jax.ShapeDtypeStruct((m, width), F32))
            continue
        heads = width // head_dim
        out_shape.append(jax.ShapeDtypeStruct((DEPTH, m * heads, head_dim), F32))
        if stacked is None:
            out_specs.append(pl.BlockSpec((DEPTH, tm * heads, head_dim), lambda i: (0, i, 0)))
        else:
            out_specs.append(pl.BlockSpec((None, tm * heads, head_dim), lambda i: (l, i, 0)))
            aliases[len(args)] = k
            in_specs.append(pl.BlockSpec(memory_space=pl.ANY))
            args.append(stacked[len(aliases) - 1])
    return pl.pallas_call(
        functools.partial(_norm_proj_kernel, segs=tuple(segs), rope=rope),
        grid=(m // tm,),
        in_specs=in_specs,
        out_specs=out_specs,
        out_shape=out_shape,
        input_output_aliases=aliases,
        compiler_params=_params(1),
        name="norm_proj",
    )(*args)


def _rope_tables(positions):
    half = ROT_DIM // 2
    inv_freq = jnp.power(jnp.float32(ROPE_THETA), -jnp.arange(half, dtype=F32) / half)
    ang = positions.astype(F32)[:, None] * inv_freq[None, :]
    cos, sin = jnp.cos(ang), jnp.sin(ang)
    n = positions.shape[0]
    rest = ATT_HEAD_DIM - ROT_DIM
    cos_t = jnp.concatenate([cos, cos, jnp.ones((n, rest), F32)], axis=1)
    sin_lo = jnp.concatenate([-sin, jnp.zeros((n, half + rest), F32)], axis=1)
    sin_hi = jnp.concatenate([jnp.zeros((n, half), F32), sin, jnp.zeros((n, rest), F32)], axis=1)
    return cos_t, sin_lo, sin_hi


def _out_proj_kernel(*refs, n_terms):
    x_ref = refs[0]
    y_refs = refs[1:1 + n_terms]
    w_refs = refs[1 + n_terms:1 + 2 * n_terms]
    o_ref = refs[1 + 2 * n_terms]
    acc = x_ref[...]
    for y_ref, w_ref in zip(y_refs, w_refs):
        acc = acc + jnp.dot(y_ref[...].astype(BF16), w_ref[...], preferred_element_type=F32)
    o_ref[...] = acc


def _out_proj(x, ys, w, l, row_starts, tm=512):
    m = x.shape[0]
    in_specs = [pl.BlockSpec((tm, D_MODEL), lambda i: (i, 0))]
    for y in ys:
        in_specs.append(pl.BlockSpec((tm, y.shape[1]), lambda i: (i, 0)))
    for y, r0 in zip(ys, row_starts):
        width = y.shape[1]
        blk = r0 // width
        in_specs.append(pl.BlockSpec((None, width, D_MODEL), lambda i, blk=blk: (l, blk, 0)))
    return pl.pallas_call(
        functools.partial(_out_proj_kernel, n_terms=len(ys)),
        grid=(m // tm,),
        in_specs=in_specs,
        out_specs=pl.BlockSpec((tm, D_MODEL), lambda i: (i, 0)),
        out_shape=jax.ShapeDtypeStruct((m, D_MODEL), F32),
        compiler_params=_params(1),
        name="out_proj",
    )(x, *ys, *([w] * len(ys)))


def _select_blocks(gate, n_allowed):
    lane = lax.broadcasted_iota(jnp.int32, gate.shape, 1)
    allowed = lane < n_allowed
    g = jnp.where(allowed, gate, NEG_INF)
    rank = jnp.zeros(gate.shape, F32)
    for r in range(1, SUBLANES):
        lower = pltpu.roll(g, r, 1)
        higher = pltpu.roll(g, LANES - r, 1)
        rank = rank + jnp.where(lower >= g, 1.0, 0.0) + jnp.where(higher > g, 1.0, 0.0)
    return jnp.where(allowed, jnp.where(rank < MOBA_TOPK, 1.0, 0.0), 0.0)


def _block_expander(n_keys):
    blk = np.arange(n_keys) // MOBA_BLOCK
    return jnp.asarray((np.arange(LANES)[:, None] == blk[None, :]).astype(np.float32), dtype=BF16)


MASK_BIAS = -1e30


def _select_blocks_t(gate_t, n_allowed):
    blk = lax.broadcasted_iota(jnp.int32, gate_t.shape, 0)
    allowed = blk < n_allowed
    g = jnp.where(allowed, gate_t, NEG_INF)
    rank = jnp.zeros(gate_t.shape, F32)
    for m in range(SUBLANES):
        gm = g[m:m + 1, :]
        tie = jnp.where(gm == g, jnp.where(blk > m, 1.0, 0.0), 0.0)
        rank = rank + jnp.where(gm > g, 1.0, tie)
    return jnp.where(allowed, jnp.where(rank < MOBA_TOPK, 1.0, 0.0), 0.0)


def _moba_prompt_kernel(q_ref, k_ref, v_ref, kbias_ref, o_ref, ka_ref, vb_ref, km_ref, qa_ref, s_ref,
                        mrun_ref, lrun_ref, acc_ref, *, n_blocks):
    qi = pl.program_id(1)
    blk = MOBA_BLOCK
    dh = ATT_HEAD_DIM
    heads = range(ATT_HEADS)
    own_slot = n_blocks * blk

    @pl.when(qi == 0)
    def _():
        for h in heads:
            vb_ref[:, h * dh:(h + 1) * dh] = v_ref[pl.ds(h, n_blocks * blk, stride=ATT_HEADS), :].astype(BF16)
            ka_ref[h, :, dh:2 * dh] = kbias_ref[...]
            for n in range(n_blocks):
                kb = k_ref[pl.ds(n * blk * ATT_HEADS + h, blk, stride=ATT_HEADS), :]
                ka_ref[h, n * blk:(n + 1) * blk, 0:dh] = kb.astype(BF16)
                km_ref[h, n:n + 1, :] = jnp.mean(kb, axis=0, keepdims=True)

    own = pl.multiple_of(qi * blk, blk)
    row = lax.broadcasted_iota(jnp.int32, (blk, blk), 0)
    col = lax.broadcasted_iota(jnp.int32, (blk, blk), 1)
    for h in heads:
        q = q_ref[:, h * dh:(h + 1) * dh]
        gate_t = lax.dot_general(km_ref[h], q, _NT, precision=lax.Precision.HIGHEST,
                                 preferred_element_type=F32)
        unchosen_t = 1.0 - _select_blocks_t(gate_t, qi)
        unchosen = jnp.concatenate([unchosen_t, jnp.zeros((LANES - SUBLANES, blk), F32)], axis=0).T
        qs = (q * (dh ** -0.5)).astype(BF16)
        qa_ref[h, :, 0:dh] = qs
        qa_ref[h, :, dh:2 * dh] = unchosen.astype(BF16)
        s = lax.dot_general(qs, ka_ref[h, pl.ds(own, blk), 0:dh], _NT, preferred_element_type=F32)
        s = jnp.where(row >= col, s, NEG_INF)
        s_ref[h, :, own_slot:own_slot + blk] = s
        mrun_ref[h] = jnp.maximum(s[:, 0:LANES], s[:, LANES:blk])

    for n in range(n_blocks - 1):
        @pl.when(n < qi)
        def _(n=n):
            for h in heads:
                s = lax.dot_general(qa_ref[h], ka_ref[h, n * blk:(n + 1) * blk, :], _NT,
                                    preferred_element_type=F32)
                s_ref[h, :, n * blk:(n + 1) * blk] = s
                mrun_ref[h] = jnp.maximum(mrun_ref[h], jnp.maximum(s[:, 0:LANES], s[:, LANES:blk]))

    for h in heads:
        m = jnp.broadcast_to(jnp.max(mrun_ref[h], axis=-1, keepdims=True), (blk, LANES))
        mrun_ref[h] = m
        p0 = jnp.exp(s_ref[h, :, own_slot:own_slot + LANES] - m)
        p1 = jnp.exp(s_ref[h, :, own_slot + LANES:own_slot + blk] - m)
        lrun_ref[h] = p0 + p1
        p = jnp.concatenate([p0, p1], axis=1).astype(BF16)
        acc_ref[h] = jnp.dot(p, vb_ref[pl.ds(own, blk), h * dh:(h + 1) * dh], preferred_element_type=F32)

    for n in range(n_blocks - 1):
        @pl.when(n < qi)
        def _(n=n):
            for h in heads:
                m = mrun_ref[h]
                p0 = jnp.exp(s_ref[h, :, n * blk:n * blk + LANES] - m)
                p1 = jnp.exp(s_ref[h, :, n * blk + LANES:(n + 1) * blk] - m)
                lrun_ref[h] = lrun_ref[h] + (p0 + p1)
                p = jnp.concatenate([p0, p1], axis=1).astype(BF16)
                acc_ref[h] = acc_ref[h] + jnp.dot(p, vb_ref[n * blk:(n + 1) * blk, h * dh:(h + 1) * dh],
                                                  preferred_element_type=F32)

    for h in heads:
        o_ref[:, h * dh:(h + 1) * dh] = acc_ref[h] / jnp.sum(lrun_ref[h], axis=-1, keepdims=True)


def _moba_prompt(q, k, v, bsz, t, l):
    assert t % MOBA_BLOCK == 0 and t // MOBA_BLOCK <= SUBLANES
    n_blocks = t // MOBA_BLOCK
    dh = ATT_HEAD_DIM
    key_blk = np.arange(t) // MOBA_BLOCK
    kbias = jnp.asarray(np.where(key_blk[:, None] == np.arange(LANES)[None, :], MASK_BIAS, 0.0), dtype=BF16)
    return pl.pallas_call(
        functools.partial(_moba_prompt_kernel, n_blocks=n_blocks),
        grid=(bsz, n_blocks),
        in_specs=[
            pl.BlockSpec((MOBA_BLOCK, ATT_WIDTH), lambda b, i: (b * n_blocks + i, 0)),
            pl.BlockSpec((None, t * ATT_HEADS, dh), lambda b, i: (l, b, 0)),
            pl.BlockSpec((None, t * ATT_HEADS, dh), lambda b, i: (l, b, 0)),
            pl.BlockSpec((t, LANES), lambda b, i: (0, 0)),
        ],
        out_specs=pl.BlockSpec((MOBA_BLOCK, ATT_WIDTH), lambda b, i: (b * n_blocks + i, 0)),
        out_shape=jax.ShapeDtypeStruct((bsz * t, ATT_WIDTH), F32),
        scratch_shapes=[
            pltpu.VMEM((ATT_HEADS, t, 2 * dh), BF16),
            pltpu.VMEM((t, ATT_WIDTH), BF16),
            pltpu.VMEM((ATT_HEADS, SUBLANES, dh), F32),
            pltpu.VMEM((ATT_HEADS, MOBA_BLOCK, 2 * dh), BF16),
            pltpu.VMEM((ATT_HEADS, MOBA_BLOCK, (n_blocks + 1) * MOBA_BLOCK), F32),
            pltpu.VMEM((ATT_HEADS, MOBA_BLOCK, LANES), F32),
            pltpu.VMEM((ATT_HEADS, MOBA_BLOCK, LANES), F32),
            pltpu.VMEM((ATT_HEADS, MOBA_BLOCK, dh), F32),
        ],
        compiler_params=_params(2),
        name="moba_prompt",
    )(q, k, v, kbias)


def _moba_sample_kernel(pt_ref, q_ref, kn_ref, vn_ref, e_ref, *refs, n_pages, n_seq):
    del pt_ref
    k_refs = refs[:n_seq * n_pages]
    v_refs = refs[n_seq * n_pages:2 * n_seq * n_pages]
    o_ref = refs[2 * n_seq * n_pages]
    rows = q_ref.shape[1]
    pages_per_block = MOBA_BLOCK // PAGE_SIZE
    n_blocks = n_pages // pages_per_block
    dh = ATT_HEAD_DIM
    pairs = [(g, h) for g in range(n_seq) for h in range(ATT_HEADS)]

    def block_rows(page_refs, g, n, h):
        first = g * n_pages + n * pages_per_block
        return jnp.concatenate(
            [page_refs[j][0, pl.ds(h, PAGE_SIZE, stride=ATT_HEADS), :] for j in range(first, first + pages_per_block)],
            axis=0)

    qs, gate_rows, s_rows = [], [], []
    for g, h in pairs:
        q = q_ref[g, :, h * dh:(h + 1) * dh]
        qs.append((q * (dh ** -0.5)).astype(BF16))
        mean_rows, s_parts = [], []
        for n in range(n_blocks):
            kb = block_rows(k_refs, g, n, h)
            mean_rows.append(jnp.sum(kb, axis=0, keepdims=True) / MOBA_BLOCK)
            s_parts.append(lax.dot_general(qs[-1], kb.astype(BF16), _NT, preferred_element_type=F32))
        mean_rows.append(jnp.zeros((LANES - n_blocks, dh), F32))
        gate_rows.append(lax.dot_general(q, jnp.concatenate(mean_rows, axis=0), _NT,
                                         precision=lax.Precision.HIGHEST, preferred_element_type=F32))
        s_rows.append(jnp.concatenate(s_parts, axis=1))
    sel = _select_blocks(jnp.concatenate(gate_rows, axis=0), n_blocks)
    sel_keys = jnp.dot(sel.astype(BF16), e_ref[...], preferred_element_type=F32)
    s = jnp.where(sel_keys > 0.5, jnp.concatenate(s_rows, axis=0), NEG_INF)

    pad = jnp.zeros((LANES - rows, dh), F32)
    padded = lambda ref, g, h: jnp.concatenate([ref[g, :, h * dh:(h + 1) * dh], pad], axis=0).astype(BF16)
    s_own = jnp.concatenate(
        [lax.dot_general(qs[i], padded(kn_ref, g, h), _NT, preferred_element_type=F32)
         for i, (g, h) in enumerate(pairs)], axis=0)
    t_row = lax.broadcasted_iota(jnp.int32, s_own.shape, 0) % rows
    key = lax.broadcasted_iota(jnp.int32, s_own.shape, 1)
    s_own = jnp.where(key <= t_row, s_own, NEG_INF)

    m = jnp.maximum(jnp.max(s, axis=-1, keepdims=True), jnp.max(s_own, axis=-1, keepdims=True))
    p = jnp.exp(s - m)
    p_own = jnp.exp(s_own - m)
    inv = 1.0 / (jnp.sum(p, axis=-1, keepdims=True) + jnp.sum(p_own, axis=-1, keepdims=True))
    pb = p.astype(BF16)
    pb_own = p_own.astype(BF16)
    for i, (g, h) in enumerate(pairs):
        r0 = i * rows
        o = jnp.dot(pb_own[r0:r0 + rows], padded(vn_ref, g, h), preferred_element_type=F32)
        for n in range(n_blocks):
            o = o + jnp.dot(pb[r0:r0 + rows, n * MOBA_BLOCK:(n + 1) * MOBA_BLOCK],
                            block_rows(v_refs, g, n, h).astype(BF16), preferred_element_type=F32)
        o_ref[g, :, h * dh:(h + 1) * dh] = o * inv[r0:r0 + rows]


def _moba_sample(q, k_new, v_new, cache_k, cache_v, page_table, l, n_seq=2):
    bsz, rows, _ = q.shape
    n_pages = page_table.shape[1]
    n_phys = cache_k.shape[0] // DEPTH
    past_len = n_pages * PAGE_SIZE
    assert past_len % MOBA_BLOCK == 0 and past_len // MOBA_BLOCK <= SUBLANES and bsz % n_seq == 0
    base = l * n_phys
    tok_spec = pl.BlockSpec((n_seq, rows, ATT_WIDTH), lambda b, pt: (b, 0, 0))
    page_specs = [
        pl.BlockSpec((1, PAGE_SIZE * ATT_HEADS, ATT_HEAD_DIM),
                     lambda b, pt, g=g, j=j: (base + pt[(b * n_seq + g) * n_pages + j], 0, 0))
        for g in range(n_seq) for j in range(n_pages)
    ]
    grid_spec = pltpu.PrefetchScalarGridSpec(
        num_scalar_prefetch=1,
        grid=(bsz // n_seq,),
        in_specs=[tok_spec, tok_spec, tok_spec,
                  pl.BlockSpec((LANES, past_len), lambda b, pt: (0, 0))] + page_specs + page_specs,
        out_specs=tok_spec,
    )
    return pl.pallas_call(
        functools.partial(_moba_sample_kernel, n_pages=n_pages, n_seq=n_seq),
        grid_spec=grid_spec,
        out_shape=jax.ShapeDtypeStruct((bsz, rows, ATT_WIDTH), F32),
        compiler_params=_params(1),
        name="moba_sample",
    )(page_table.reshape(-1), q, k_new, v_new, _block_expander(past_len),
      *([cache_k] * (n_seq * n_pages)), *([cache_v] * (n_seq * n_pages)))


def _pool_windows_to_lanes(sums, counts, cur):
    lane = lax.broadcasted_iota(jnp.int32, cur.shape, 1)
    d = sums[-1] / counts[-1]
    for g in range(len(POOL_WINDOWS) - 2, -1, -1):
        d = jnp.where(lane < (g + 1) * POOL_GROUP_WIDTH, sums[g] / counts[g], d)
    return d - cur


def _conv_tail(y, ln_g, ln_b, pw_w, pw_b):
    yc = y - jnp.mean(y, axis=-1, keepdims=True)
    yn = yc * lax.rsqrt(jnp.mean(yc * yc, axis=-1, keepdims=True) + EPS) * ln_g + ln_b
    act = yn * jax.nn.sigmoid(yn)
    return jnp.dot(act.astype(BF16), pw_w, preferred_element_type=F32) + pw_b


POOL_HALO = 16
CONV_HALO = 32


def _poolconv_prompt_kernel(u_ref, ag_ref, pw_ref, ps_ref, dw_ref, db_ref, lg_ref, lb_ref, cw_ref, cb_ref,
                            yp_ref, yc_ref, ht_ref, e_ref, h_ref, *, tt):
    ti = pl.program_id(1)

    @pl.when(ti == 0)
    def _():
        e_ref[0:POOL_HALO, :] = jnp.zeros((POOL_HALO, POOL_WIDTH), F32)
        h_ref[0:CONV_HALO, :] = jnp.zeros((CONV_HALO, CONV_WIDTH), F32)

    u = u_ref[...]
    e_ref[POOL_HALO:POOL_HALO + tt, :] = u
    a = ag_ref[:, 0:CONV_WIDTH]
    gate = ag_ref[:, CONV_WIDTH:2 * CONV_WIDTH]
    h_ref[CONV_HALO:CONV_HALO + tt, :] = a * jax.nn.sigmoid(gate)

    pos = ti * tt + lax.broadcasted_iota(jnp.int32, (tt, 1), 0)
    sums, counts = [], []
    run = u
    taken = 1
    for win in POOL_WINDOWS:
        while taken < win:
            run = run + e_ref[POOL_HALO - taken:POOL_HALO - taken + tt, :]
            taken += 1
        sums.append(run)
        counts.append(jnp.minimum(pos + 1, win).astype(F32))
    d = _pool_windows_to_lanes(sums, counts, u)
    yp_ref[...] = jnp.dot(d.astype(BF16), pw_ref[...], preferred_element_type=F32) * ps_ref[...]

    y = jnp.zeros((tt, CONV_WIDTH), F32) + db_ref[...]
    for j in range(CONV_K):
        r0 = CONV_HALO - CONV_BUF + j
        y = y + h_ref[r0:r0 + tt, :] * dw_ref[j:j + 1, :]
    yc_ref[...] = _conv_tail(y, lg_ref[...], lb_ref[...], cw_ref[...], cb_ref[...])

    ht_ref[0] = h_ref[tt:tt + CONV_HALO, :]
    e_ref[0:POOL_HALO, :] = e_ref[tt:tt + POOL_HALO, :]
    h_ref[0:CONV_HALO, :] = h_ref[tt:tt + CONV_HALO, :]


def _poolconv_prompt(u, ag, bsz, t, weights, l, tt=512):
    pool_w, pool_s, dw_w, dw_b, ln_g, ln_b, pw_w, pw_b = weights
    nt = t // tt
    row = lambda b, i: (b * nt + i, 0)
    return pl.pallas_call(
        functools.partial(_poolconv_prompt_kernel, tt=tt),
        grid=(bsz, nt),
        in_specs=[
            pl.BlockSpec((tt, POOL_WIDTH), row),
            pl.BlockSpec((tt, 2 * CONV_WIDTH), row),
            _layer_spec((POOL_WIDTH, POOL_WIDTH), l, 2),
            _layer_spec((1, POOL_WIDTH), l, 2),
            _layer_spec((CONV_HALO, CONV_WIDTH), l, 2),
            _layer_spec((1, CONV_WIDTH), l, 2),
            _layer_spec((1, CONV_WIDTH), l, 2),
            _layer_spec((1, CONV_WIDTH), l, 2),
            _layer_spec((CONV_WIDTH, CONV_WIDTH), l, 2),
            _layer_spec((1, CONV_WIDTH), l, 2),
        ],
        out_specs=[
            pl.BlockSpec((tt, POOL_WIDTH), row),
            pl.BlockSpec((tt, CONV_WIDTH), row),
            pl.BlockSpec((1, CONV_HALO, CONV_WIDTH), lambda b, i: (b, 0, 0)),
        ],
        out_shape=[
            jax.ShapeDtypeStruct((bsz * t, POOL_WIDTH), F32),
            jax.ShapeDtypeStruct((bsz * t, CONV_WIDTH), F32),
            jax.ShapeDtypeStruct((bsz, CONV_HALO, CONV_WIDTH), F32),
        ],
        scratch_shapes=[
            pltpu.VMEM((POOL_HALO + tt, POOL_WIDTH), F32),
            pltpu.VMEM((CONV_HALO + tt, CONV_WIDTH), F32),
        ],
        compiler_params=_params(2),
        name="poolconv_prompt",
    )(u, ag, pool_w, pool_s, dw_w, dw_b, ln_g, ln_b, pw_w, pw_b)


def _poolconv_sample_kernel(sp_ref, u_ref, sc_ref, ag_ref, pw_ref, ps_ref, dw_ref, db_ref, lg_ref, lb_ref,
                            cw_ref, cb_ref, yp_ref, yc_ref, h_ref, *, n_new, pos0):
    def pool_row(r):
        return sp_ref[r] if r < POOL_BUF else u_ref[r - POOL_BUF]

    for t in range(n_new):
        h_ref[t] = ag_ref[t, :, 0:CONV_WIDTH] * jax.nn.sigmoid(ag_ref[t, :, CONV_WIDTH:2 * CONV_WIDTH])

    def conv_row(r):
        return sc_ref[r] if r < CONV_BUF else h_ref[r - CONV_BUF]

    for t in range(n_new):
        cur = u_ref[t]
        sums, counts = [], []
        run = cur
        taken = 1
        for win in POOL_WINDOWS:
            while taken < win:
                run = run + pool_row(POOL_BUF + t - taken)
                taken += 1
            sums.append(run)
            counts.append(float(min(pos0 + t + 1, win)))
        d = _pool_windows_to_lanes(sums, counts, cur)
        yp_ref[t] = jnp.dot(d.astype(BF16), pw_ref[...], preferred_element_type=F32) * ps_ref[...]

        y = jnp.zeros(cur.shape, F32) + db_ref[...]
        for j in range(CONV_K):
            y = y + conv_row(t + j) * dw_ref[j:j + 1, :]
        yc_ref[t] = _conv_tail(y, lg_ref[...], lb_ref[...], cw_ref[...], cb_ref[...])


def _poolconv_sample(state_pool_t, u_t, state_conv_t, ag_t, pos0, weights, l):
    pool_w, pool_s, dw_w, dw_b, ln_g, ln_b, pw_w, pw_b = weights
    n_new, bsz, _ = u_t.shape
    full = lambda a: pl.BlockSpec(a.shape, lambda i: (0,) * a.ndim)
    out = jax.ShapeDtypeStruct((n_new, bsz, CONV_WIDTH), F32)
    return pl.pallas_call(
        functools.partial(_poolconv_sample_kernel, n_new=n_new, pos0=pos0),
        grid=(1,),
        in_specs=[
            full(state_pool_t), full(u_t), full(state_conv_t), full(ag_t),
            _layer_spec((POOL_WIDTH, POOL_WIDTH), l, 1),
            _layer_spec((1, POOL_WIDTH), l, 1),
            _layer_spec((CONV_HALO, CONV_WIDTH), l, 1),
            _layer_spec((1, CONV_WIDTH), l, 1),
            _layer_spec((1, CONV_WIDTH), l, 1),
            _layer_spec((1, CONV_WIDTH), l, 1),
            _layer_spec((CONV_WIDTH, CONV_WIDTH), l, 1),
            _layer_spec((1, CONV_WIDTH), l, 1),
        ],
        out_specs=[pl.BlockSpec(out.shape, lambda i: (0, 0, 0))] * 3,
        out_shape=[out, out, out],
        compiler_params=_params(1),
        name="poolconv_sample",
    )(state_pool_t, u_t, state_conv_t, ag_t, pool_w, pool_s, dw_w, dw_b, ln_g, ln_b, pw_w, pw_b)


def _cross_kernel(x_ref, g_ref, wq_ref, hg_ref, k_ref, v_ref, wo_ref, o_ref, *, n_seq, interleaved):
    hd = MEM_HEAD_DIM
    rows = x_ref.shape[1]
    x = x_ref[...].reshape(n_seq * rows, D_MODEL)
    xn = _rms(x, g_ref[...]).astype(BF16)
    q_all = jnp.dot(xn, wq_ref[...], preferred_element_type=F32)
    halves = range(hd // LANES)

    def head_slab(ref, g, h):
        if not interleaved:
            return ref[g, :, h * hd:(h + 1) * hd]
        return jnp.concatenate(
            [ref[g, pl.ds(half * MEM_HEADS + h, N_MEM, stride=len(halves) * MEM_HEADS), :] for half in halves],
            axis=1)

    pairs = [(g, h) for g in range(n_seq) for h in range(MEM_HEADS)]
    scores = []
    for g, h in pairs:
        q = _rms(q_all[g * rows:(g + 1) * rows, h * hd:(h + 1) * hd], hg_ref[...])
        q = (q * (hd ** -0.5)).astype(BF16)
        scores.append(lax.dot_general(q, head_slab(k_ref, g, h).astype(BF16), _NT, preferred_element_type=F32))
    s = jnp.concatenate(scores, axis=0)
    p = jnp.exp(s - jnp.max(s, axis=-1, keepdims=True))
    inv = 1.0 / jnp.sum(p, axis=-1, keepdims=True)
    o_seqs = []
    for g in range(n_seq):
        o_heads = []
        for h in range(MEM_HEADS):
            r0 = (g * MEM_HEADS + h) * rows
            o = jnp.dot(p[r0:r0 + rows].astype(BF16), head_slab(v_ref, g, h).astype(BF16),
                        preferred_element_type=F32)
            o_heads.append((o * inv[r0:r0 + rows]).astype(BF16))
        o_seqs.append(jnp.concatenate(o_heads, axis=1))
    o_all = o_seqs[0] if n_seq == 1 else jnp.concatenate(o_seqs, axis=0)
    y = x + jnp.dot(o_all, wo_ref[...], preferred_element_type=F32)
    o_ref[...] = y.reshape(n_seq, rows, D_MODEL)


def _interleaved_memory(cache):
    depth, s, n_mem, heads, hd = cache.shape
    halves = hd // LANES
    c = cache.reshape(depth, s, n_mem, heads, halves, LANES).transpose(0, 1, 2, 4, 3, 5)
    return c.reshape(depth * s, n_mem * halves * heads, LANES)


def _cross(x, ln, w_q, head_gain, mem_k, mem_v, w_o, l, n_seq, tq, seq0=None):
    s_total, r, _ = x.shape
    interleaved = seq0 is not None
    if interleaved:
        assert seq0 % n_seq == 0
        kv_spec = pl.BlockSpec((n_seq,) + mem_k.shape[1:], lambda s, i: (seq0 // n_seq + s, 0, 0))
    else:
        kv_spec = pl.BlockSpec((n_seq, N_MEM, D_MODEL), lambda s, i: (s, 0, 0))
    x_spec = pl.BlockSpec((n_seq, tq, D_MODEL), lambda s, i: (s, i, 0))
    return pl.pallas_call(
        functools.partial(_cross_kernel, n_seq=n_seq, interleaved=interleaved),
        grid=(s_total // n_seq, r // tq),
        in_specs=[x_spec, _layer_spec((1, D_MODEL), l, 2), _layer_spec((D_MODEL, D_MODEL), l, 2),
                  _layer_spec((1, MEM_HEAD_DIM), l, 2), kv_spec, kv_spec,
                  _layer_spec((D_MODEL, D_MODEL), l, 2)],
        out_specs=x_spec,
        out_shape=jax.ShapeDtypeStruct(x.shape, F32),
        compiler_params=_params(2),
        name="cross",
    )(x, ln, w_q, head_gain, mem_k, mem_v, w_o)


def _pad_rows(x, rows):
    return jnp.pad(x, ((0, 0), (0, rows - x.shape[1]), (0, 0)))


def kernel(x_prompt, x_sample, cache_att_k, cache_att_v, cache_mem_k, cache_mem_v, state_pool, state_conv, page_table, mem_prompt, ln_ffn1, w_ffn1_in, w_ffn1_out, ln_mix, w_in, pool_w, pool_scale, conv_dw_w, conv_dw_b, conv_ln_g, conv_ln_b, conv_pw_w, conv_pw_b, att_q_norm, att_k_norm, w_out, ln_cross, ln_mem, w_cq, w_ckv, cq_norm, ck_norm, w_co, ln_ffn2, w_ffn2_in, w_ffn2_out):
    bsz, seq, _ = x_prompt.shape
    dbsz, dseq, _ = x_sample.shape
    n_pages = page_table.shape[1]
    past_len = n_pages * PAGE_SIZE
    n_phys = cache_att_k.shape[1]

    bf = lambda w: w.astype(BF16)
    rowv = lambda v: v[:, None, :]
    w1i, w1o, w2i, w2o = bf(w_ffn1_in), bf(w_ffn1_out), bf(w_ffn2_in), bf(w_ffn2_out)
    w_in_b, w_out_b, w_cq_b, w_ckv_b, w_co_b = bf(w_in), bf(w_out), bf(w_cq), bf(w_ckv), bf(w_co)
    groups = len(POOL_WINDOWS)
    eye = jnp.eye(groups, dtype=F32)
    pool_bd = bf((pool_w[:, :, :, None, :] * eye[None, :, None, :, None])
                 .reshape(DEPTH, POOL_WIDTH, POOL_WIDTH))
    dw_pad = jnp.pad(conv_dw_w, ((0, 0), (0, CONV_HALO - CONV_K), (0, 0)))
    mixer_w = (pool_bd, rowv(pool_scale), dw_pad, rowv(conv_dw_b), rowv(conv_ln_g), rowv(conv_ln_b),
               bf(conv_pw_w), rowv(conv_pw_b))
    ln1, lnm, lnc, lnmem, ln2 = rowv(ln_ffn1), rowv(ln_mix), rowv(ln_cross), rowv(ln_mem), rowv(ln_ffn2)

    tables_p = _rope_tables(jnp.arange(seq, dtype=jnp.int32))
    tm_s = dbsz * dseq
    tables_s = _rope_tables(past_len + jnp.arange(tm_s, dtype=jnp.int32) % dseq)

    c1 = POOL_WIDTH
    c3 = c1 + 2 * CONV_WIDTH
    c4 = c3 + ATT_WIDTH
    c5 = c4 + ATT_WIDTH
    mix_segs = [(c3, ATT_WIDTH, ATT_HEAD_DIM, 0, True, False), (c4, ATT_WIDTH, ATT_HEAD_DIM, 1, True, True),
                (c5, ATT_WIDTH, ATT_HEAD_DIM, None, False, True),
                (0, c1, None, None, False, False), (c1, 2 * CONV_WIDTH, None, None, False, False)]
    mem_segs = [(0, D_MODEL, MEM_HEAD_DIM, 0, False, False), (D_MODEL, D_MODEL, None, None, False, False)]
    out_rows = (0, c1, c1 + CONV_WIDTH)

    cache_k = cache_att_k.reshape(DEPTH * n_phys, PAGE_SIZE * ATT_HEADS, ATT_HEAD_DIM)
    cache_v = cache_att_v.reshape(DEPTH * n_phys, PAGE_SIZE * ATT_HEADS, ATT_HEAD_DIM)
    mem_k_s = _interleaved_memory(cache_mem_k)
    mem_v_s = _interleaved_memory(cache_mem_v)
    mem_rows = mem_prompt.reshape(bsz * N_MEM, D_MODEL)

    xp = x_prompt.reshape(bsz * seq, D_MODEL)
    xs = x_sample.reshape(dbsz * dseq, D_MODEL)
    rows_pad = SUBLANES
    outs = {name: [] for name in ("p_mk", "p_mv", "p_pool", "p_conv", "s_pool", "s_conv")}
    cq_gain = rowv(cq_norm)
    kv_p = kv_s = None

    for l in range(DEPTH):
        att_gains = jnp.stack([att_q_norm[l], att_k_norm[l]])

        mk, mv = _norm_proj(mem_rows, lnmem, w_ckv_b, l, ck_norm[l][None], mem_segs)
        xp = _ffn(xp, ln1, w1i, w1o, l)
        q, k, v, u, ag = _norm_proj(xp, lnm, w_in_b, l, att_gains, mix_segs, tables=tables_p, stacked=kv_p)
        kv_p = (k, v)
        y_att = _moba_prompt(q, k, v, bsz, seq, l)
        y_pool, y_conv, h_tail = _poolconv_prompt(u, ag, bsz, seq, mixer_w, l)
        xp = _out_proj(xp, [y_pool, y_conv, y_att], w_out_b, l, out_rows)
        xp = _cross(xp.reshape(bsz, seq, D_MODEL), lnc, w_cq_b, cq_gain, mk.reshape(bsz, N_MEM, D_MODEL),
                    mv.reshape(bsz, N_MEM, D_MODEL), w_co_b, l, 1, 512).reshape(bsz * seq, D_MODEL)
        xp = _ffn(xp, ln2, w2i, w2o, l)
        outs["p_mk"].append(mk.reshape(bsz, N_MEM, MEM_HEADS, MEM_HEAD_DIM))
        outs["p_mv"].append(mv.reshape(bsz, N_MEM, MEM_HEADS, MEM_HEAD_DIM))
        outs["p_pool"].append(u.reshape(bsz, seq, POOL_WIDTH)[:, seq - POOL_BUF:])
        outs["p_conv"].append(h_tail[:, CONV_HALO - CONV_BUF:])

        xs = _ffn(xs, ln1, w1i, w1o, l, tm=tm_s)
        q, k, v, u, ag = _norm_proj(xs, lnm, w_in_b, l, att_gains, mix_segs, tables=tables_s, tm=tm_s,
                                    stacked=kv_s)
        kv_s = (k, v)
        pad3 = lambda a: _pad_rows(a.reshape(dbsz, dseq, ATT_WIDTH), 2 * SUBLANES)
        y_att = _moba_sample(pad3(q), pad3(k[l]), pad3(v[l]), cache_k, cache_v, page_table, l)
        y_att = y_att[:, :dseq].reshape(dbsz * dseq, ATT_WIDTH)
        tmaj = lambda a: jnp.swapaxes(a, 0, 1)
        u3 = u.reshape(dbsz, dseq, POOL_WIDTH)
        y_pool, y_conv, h_new = _poolconv_sample(
            tmaj(state_pool[l]), tmaj(u3), tmaj(state_conv[l]), tmaj(ag.reshape(dbsz, dseq, 2 * CONV_WIDTH)),
            past_len, mixer_w, l)
        bmaj = lambda a: jnp.swapaxes(a, 0, 1).reshape(dbsz * dseq, a.shape[-1])
        xs = _out_proj(xs, [bmaj(y_pool), bmaj(y_conv), y_att], w_out_b, l, out_rows, tm=tm_s)
        n_seq = 4
        xs = _cross(_pad_rows(xs.reshape(dbsz, dseq, D_MODEL), rows_pad), lnc, w_cq_b, cq_gain, mem_k_s, mem_v_s,
                    w_co_b, l, n_seq, rows_pad, seq0=l * dbsz)[:, :dseq].reshape(dbsz * dseq, D_MODEL)
        xs = _ffn(xs, ln2, w2i, w2o, l, tm=tm_s)
        outs["s_pool"].append(jnp.concatenate([state_pool[l][:, dseq:], u3], axis=1))
        outs["s_conv"].append(jnp.concatenate([state_conv[l][:, dseq:], jnp.swapaxes(h_new, 0, 1)], axis=1))

    st = lambda name: jnp.stack(outs[name])
    heads5 = lambda a, b, t: a.reshape(DEPTH, b, t, ATT_HEADS, ATT_HEAD_DIM)
    return (xp.reshape(bsz, seq, D_MODEL), xs.reshape(dbsz, dseq, D_MODEL),
            heads5(kv_p[0], bsz, seq), heads5(kv_p[1], bsz, seq), st("p_mk"), st("p_mv"), st("p_pool"),
            st("p_conv"), heads5(kv_s[0], dbsz, dseq), heads5(kv_s[1], dbsz, dseq), st("s_pool"), st("s_conv"))
```

```python
import functools

import jax
import jax.numpy as jnp
import numpy as np
from jax import lax
from jax.experimental import pallas as pl
from jax.experimental.pallas import tpu as pltpu

D_MODEL = 1024
DEPTH = 2
PAGE_SIZE = 128

POOL_WIDTH = D_MODEL // 4
POOL_WINDOWS = (2, 4, 8, 16)
POOL_GROUP_WIDTH = POOL_WIDTH // len(POOL_WINDOWS)
POOL_BUF = max(POOL_WINDOWS) - 1

CONV_WIDTH = D_MODEL // 4
CONV_K = 31
CONV_BUF = CONV_K - 1

ATT_HEADS = 4
ATT_HEAD_DIM = D_MODEL // 8
ATT_WIDTH = ATT_HEADS * ATT_HEAD_DIM
ROT_DIM = ATT_HEAD_DIM // 4
ROPE_THETA = 500000.0
MOBA_BLOCK = 256
MOBA_TOPK = 3

IN_WIDTH = POOL_WIDTH + 2 * CONV_WIDTH + 3 * ATT_WIDTH

N_MEM = 256
MEM_HEADS = 4
MEM_HEAD_DIM = D_MODEL // MEM_HEADS

D_FF = ((8 * D_MODEL // 3 + 127) // 128) * 128
EPS = 1e-6

LANES = 128
SUBLANES = 8
VMEM_LIMIT_BYTES = 56 * 1024 * 1024

F32 = jnp.float32
BF16 = jnp.bfloat16
NEG_INF = float("-inf")

_NT = (((1,), (1,)), ((), ()))


def _params(n_axes):
    return pltpu.CompilerParams(dimension_semantics=("arbitrary",) * n_axes,
                                vmem_limit_bytes=VMEM_LIMIT_BYTES)


def _rms(x, g):
    return x * lax.rsqrt(jnp.mean(x * x, axis=-1, keepdims=True) + EPS) * g


def _layer_spec(shape, l, n_grid):
    zeros = (0,) * len(shape)
    if n_grid == 1:
        return pl.BlockSpec((None,) + tuple(shape), lambda i: (l,) + zeros)
    if n_grid == 2:
        return pl.BlockSpec((None,) + tuple(shape), lambda i, j: (l,) + zeros)
    return pl.BlockSpec((None,) + tuple(shape), lambda i, j, k: (l,) + zeros)


FFN_CHUNK = 256


def _ffn_kernel(x_ref, g_ref, wi_ref, wo_ref, o_ref):
    x = x_ref[...]
    xn = _rms(x, g_ref[...]).astype(BF16)
    acc = None
    for c in range(D_FF // FFN_CHUNK):
        lo = c * FFN_CHUNK
        gate = jnp.dot(xn, wi_ref[:, lo:lo + FFN_CHUNK], preferred_element_type=F32)
        up = jnp.dot(xn, wi_ref[:, D_FF + lo:D_FF + lo + FFN_CHUNK], preferred_element_type=F32)
        act = (gate * jax.nn.sigmoid(gate) * up).astype(BF16)
        part = jnp.dot(act, wo_ref[lo:lo + FFN_CHUNK, :], preferred_element_type=F32)
        acc = part if acc is None else acc + part
    o_ref[...] = x + 0.5 * acc


def _ffn(x, ln, w_in, w_out, l, tm=512):
    m = x.shape[0]
    return pl.pallas_call(
        _ffn_kernel,
        grid=(m // tm,),
        in_specs=[
            pl.BlockSpec((tm, D_MODEL), lambda i: (i, 0)),
            _layer_spec((1, D_MODEL), l, 1),
            _layer_spec((D_MODEL, 2 * D_FF), l, 1),
            _layer_spec((D_FF, D_MODEL), l, 1),
        ],
        out_specs=pl.BlockSpec((tm, D_MODEL), lambda i: (i, 0)),
        out_shape=jax.ShapeDtypeStruct((m, D_MODEL), F32),
        compiler_params=_params(1),
        name="ffn",
    )(x, ln, w_in, w_out)


def _norm_proj_kernel(*refs, segs, rope):
    x_ref, g_ref, w_ref, hg_ref = refs[:4]
    if rope:
        cos_ref, sin_lo_ref, sin_hi_ref = refs[4:7]
    out_refs = refs[len(refs) - len(segs):]
    tm = x_ref.shape[0]
    xn = _rms(x_ref[...], g_ref[...]).astype(BF16)
    for (c0, width, head_dim, gain_row, use_rope, head_rows), o_ref in zip(segs, out_refs):
        z = jnp.dot(xn, w_ref[:, c0:c0 + width], preferred_element_type=F32)
        if head_dim is None:
            o_ref[...] = z
            continue
        n_heads = width // head_dim
        for h in range(n_heads):
            zh = z[:, h * head_dim:(h + 1) * head_dim]
            if gain_row is not None:
                zh = _rms(zh, hg_ref[gain_row:gain_row + 1, :])
            if use_rope:
                half = ROT_DIM // 2
                zh = (zh * cos_ref[...]
                      + pltpu.roll(zh, half, 1) * sin_hi_ref[...]
                      + pltpu.roll(zh, LANES - half, 1) * sin_lo_ref[...])
            if not head_rows:
                o_ref[:, h * head_dim:(h + 1) * head_dim] = zh
                continue
            rows = pl.ds(h, tm, stride=n_heads)
            if len(o_ref.shape) == 2:
                o_ref[rows, :] = zh
            else:
                for d in range(o_ref.shape[0]):
                    o_ref[d, rows, :] = zh


def _norm_proj(x, ln, w, l, head_gains, segs, tables=None, tm=512, stacked=None):
    m = x.shape[0]
    n_cols = w.shape[-1]
    rope = tables is not None
    in_specs = [
        pl.BlockSpec((tm, D_MODEL), lambda i: (i, 0)),
        _layer_spec((1, D_MODEL), l, 1),
        _layer_spec((D_MODEL, n_cols), l, 1),
        pl.BlockSpec(head_gains.shape, lambda i: (0, 0)),
    ]
    args = [x, ln, w, head_gains]
    if rope:
        period = tables[0].shape[0] // tm
        for t in tables:
            in_specs.append(pl.BlockSpec((tm, LANES), lambda i: (i % period, 0)))
            args.append(t)
    out_specs, out_shape, aliases = [], [], {}
    for k, s in enumerate(segs):
        width, head_dim, head_rows = s[1], s[2], s[5]
        if not head_rows:
            out_specs.append(pl.BlockSpec((tm, width), lambda i: (i, 0)))
            out_shape.append(jax.ShapeDtypeStruct((m, width), F32))
            continue
        heads = width // head_dim
        out_shape.append(jax.ShapeDtypeStruct((DEPTH, m * heads, head_dim), F32))
        if stacked is None:
            out_specs.append(pl.BlockSpec((DEPTH, tm * heads, head_dim), lambda i: (0, i, 0)))
        else:
            out_specs.append(pl.BlockSpec((None, tm * heads, head_dim), lambda i: (l, i, 0)))
            aliases[len(args)] = k
            in_specs.append(pl.BlockSpec(memory_space=pl.ANY))
            args.append(stacked[len(aliases) - 1])
    return pl.pallas_call(
        functools.partial(_norm_proj_kernel, segs=tuple(segs), rope=rope),
        grid=(m // tm,),
        in_specs=in_specs,
        out_specs=out_specs,
        out_shape=out_shape,
        input_output_aliases=aliases,
        compiler_params=_params(1),
        name="norm_proj",
    )(*args)


def _rope_tables(positions):
    half = ROT_DIM // 2
    inv_freq = jnp.power(jnp.float32(ROPE_THETA), -jnp.arange(half, dtype=F32) / half)
    ang = positions.astype(F32)[:, None] * inv_freq[None, :]
    cos, sin = jnp.cos(ang), jnp.sin(ang)
    n = positions.shape[0]
    rest = ATT_HEAD_DIM - ROT_DIM
    cos_t = jnp.concatenate([cos, cos, jnp.ones((n, rest), F32)], axis=1)
    sin_lo = jnp.concatenate([-sin, jnp.zeros((n, half + rest), F32)], axis=1)
    sin_hi = jnp.concatenate([jnp.zeros((n, half), F32), sin, jnp.zeros((n, rest), F32)], axis=1)
    return cos_t, sin_lo, sin_hi


def _out_proj_kernel(*refs, n_terms):
    x_ref = refs[0]
    y_refs = refs[1:1 + n_terms]
    w_refs = refs[1 + n_terms:1 + 2 * n_terms]
    o_ref = refs[1 + 2 * n_terms]
    acc = x_ref[...]
    for y_ref, w_ref in zip(y_refs, w_refs):
        acc = acc + jnp.dot(y_ref[...].astype(BF16), w_ref[...], preferred_element_type=F32)
    o_ref[...] = acc


def _out_proj(x, ys, w, l, row_starts, tm=512):
    m = x.shape[0]
    in_specs = [pl.BlockSpec((tm, D_MODEL), lambda i: (i, 0))]
    for y in ys:
        in_specs.append(pl.BlockSpec((tm, y.shape[1]), lambda i: (i, 0)))
    for y, r0 in zip(ys, row_starts):
        width = y.shape[1]
        blk = r0 // width
        in_specs.append(pl.BlockSpec((None, width, D_MODEL), lambda i, blk=blk: (l, blk, 0)))
    return pl.pallas_call(
        functools.partial(_out_proj_kernel, n_terms=len(ys)),
        grid=(m // tm,),
        in_specs=in_specs,
        out_specs=pl.BlockSpec((tm, D_MODEL), lambda i: (i, 0)),
        out_shape=jax.ShapeDtypeStruct((m, D_MODEL), F32),
        compiler_params=_params(1),
        name="out_proj",
    )(x, *ys, *([w] * len(ys)))


def _select_blocks(gate, n_allowed):
    lane = lax.broadcasted_iota(jnp.int32, gate.shape, 1)
    allowed = lane < n_allowed
    g = jnp.where(allowed, gate, NEG_INF)
    rank = jnp.zeros(gate.shape, F32)
    for r in range(1, SUBLANES):
        lower = pltpu.roll(g, r, 1)
        higher = pltpu.roll(g, LANES - r, 1)
        rank = rank + jnp.where(lower >= g, 1.0, 0.0) + jnp.where(higher > g, 1.0, 0.0)
    return jnp.where(allowed, jnp.where(rank < MOBA_TOPK, 1.0, 0.0), 0.0)


def _block_expander(n_keys):
    blk = np.arange(n_keys) // MOBA_BLOCK
    return jnp.asarray((np.arange(LANES)[:, None] == blk[None, :]).astype(np.float32), dtype=BF16)


MASK_BIAS = -1e30


def _select_blocks_t(gate_t, n_allowed):
    blk = lax.broadcasted_iota(jnp.int32, gate_t.shape, 0)
    allowed = blk < n_allowed
    g = jnp.where(allowed, gate_t, NEG_INF)
    rank = jnp.zeros(gate_t.shape, F32)
    for m in range(SUBLANES):
        gm = g[m:m + 1, :]
        tie = jnp.where(gm == g, jnp.where(blk > m, 1.0, 0.0), 0.0)
        rank = rank + jnp.where(gm > g, 1.0, tie)
    return jnp.where(allowed, jnp.where(rank < MOBA_TOPK, 1.0, 0.0), 0.0)


def _moba_prompt_kernel(q_ref, k_ref, v_ref, kbias_ref, o_ref, ka_ref, vb_ref, km_ref, *, n_blocks):
    qi = pl.program_id(1)
    blk = MOBA_BLOCK
    dh = ATT_HEAD_DIM
    heads = range(ATT_HEADS)

    @pl.when(qi == 0)
    def _():
        for h in heads:
            vb_ref[:, h * dh:(h + 1) * dh] = v_ref[pl.ds(h, n_blocks * blk, stride=ATT_HEADS), :].astype(BF16)
            ka_ref[h, :, dh:2 * dh] = kbias_ref[...]
            for n in range(n_blocks):
                kb = k_ref[pl.ds(n * blk * ATT_HEADS + h, blk, stride=ATT_HEADS), :]
                ka_ref[h, n * blk:(n + 1) * blk, 0:dh] = kb.astype(BF16)
                km_ref[h, n:n + 1, :] = jnp.mean(kb, axis=0, keepdims=True)

    row = lax.broadcasted_iota(jnp.int32, (blk, blk), 0)
    col = lax.broadcasted_iota(jnp.int32, (blk, blk), 1)

    def step(c):
        n_keys = (c + 1) * blk
        for h in heads:
            q = q_ref[:, h * dh:(h + 1) * dh]
            qs = (q * (dh ** -0.5)).astype(BF16)
            if c <= MOBA_TOPK:
                s = lax.dot_general(qs, ka_ref[h, 0:n_keys, 0:dh], _NT, preferred_element_type=F32)
            else:
                gate_t = lax.dot_general(km_ref[h], q, _NT, precision=lax.Precision.HIGHEST,
                                         preferred_element_type=F32)
                blk_id = lax.broadcasted_iota(jnp.int32, gate_t.shape, 0)
                unchosen_t = jnp.where(blk_id == c, 0.0, 1.0 - _select_blocks_t(gate_t, c))
                unchosen = jnp.concatenate([unchosen_t, jnp.zeros((LANES - SUBLANES, blk), F32)], axis=0).T
                qa = jnp.concatenate([qs, unchosen.astype(BF16)], axis=1)
                s = lax.dot_general(qa, ka_ref[h, 0:n_keys, :], _NT, preferred_element_type=F32)
            parts = [s[:, 0:c * blk]] if c else []
            parts.append(jnp.where(row >= col, s[:, c * blk:n_keys], NEG_INF))
            s = jnp.concatenate(parts, axis=1)
            p = jnp.exp(s - jnp.max(s, axis=-1, keepdims=True))
            inv = 1.0 / jnp.sum(p, axis=-1, keepdims=True)
            o = jnp.dot(p.astype(BF16), vb_ref[0:n_keys, h * dh:(h + 1) * dh], preferred_element_type=F32)
            o_ref[:, h * dh:(h + 1) * dh] = o * inv

    for c in range(n_blocks):
        pl.when(qi == c)(functools.partial(step, c))


def _moba_prompt(q, k, v, bsz, t, l):
    assert t % MOBA_BLOCK == 0 and t // MOBA_BLOCK <= SUBLANES
    n_blocks = t // MOBA_BLOCK
    dh = ATT_HEAD_DIM
    key_blk = np.arange(t) // MOBA_BLOCK
    kbias = jnp.asarray(np.where(key_blk[:, None] == np.arange(LANES)[None, :], MASK_BIAS, 0.0), dtype=BF16)
    return pl.pallas_call(
        functools.partial(_moba_prompt_kernel, n_blocks=n_blocks),
        grid=(bsz, n_blocks),
        in_specs=[
            pl.BlockSpec((MOBA_BLOCK, ATT_WIDTH), lambda b, i: (b * n_blocks + i, 0)),
            pl.BlockSpec((None, t * ATT_HEADS, dh), lambda b, i: (l, b, 0)),
            pl.BlockSpec((None, t * ATT_HEADS, dh), lambda b, i: (l, b, 0)),
            pl.BlockSpec((t, LANES), lambda b, i: (0, 0)),
        ],
        out_specs=pl.BlockSpec((MOBA_BLOCK, ATT_WIDTH), lambda b, i: (b * n_blocks + i, 0)),
        out_shape=jax.ShapeDtypeStruct((bsz * t, ATT_WIDTH), F32),
        scratch_shapes=[
            pltpu.VMEM((ATT_HEADS, t, 2 * dh), BF16),
            pltpu.VMEM((t, ATT_WIDTH), BF16),
            pltpu.VMEM((ATT_HEADS, SUBLANES, dh), F32),
        ],
        compiler_params=_params(2),
        name="moba_prompt",
    )(q, k, v, kbias)


def _moba_sample_kernel(pt_ref, q_ref, kn_ref, vn_ref, e_ref, *refs, n_pages, n_seq):
    del pt_ref
    k_refs = refs[:n_seq * n_pages]
    v_refs = refs[n_seq * n_pages:2 * n_seq * n_pages]
    o_ref = refs[2 * n_seq * n_pages]
    rows = q_ref.shape[1]
    pages_per_block = MOBA_BLOCK // PAGE_SIZE
    n_blocks = n_pages // pages_per_block
    dh = ATT_HEAD_DIM
    pairs = [(g, h) for g in range(n_seq) for h in range(ATT_HEADS)]

    def block_rows(page_refs, g, n, h):
        first = g * n_pages + n * pages_per_block
        return jnp.concatenate(
            [page_refs[j][0, pl.ds(h, PAGE_SIZE, stride=ATT_HEADS), :] for j in range(first, first + pages_per_block)],
            axis=0)

    qs, gate_rows, s_rows = [], [], []
    for g, h in pairs:
        q = q_ref[g, :, h * dh:(h + 1) * dh]
        qs.append((q * (dh ** -0.5)).astype(BF16))
        mean_rows, s_parts = [], []
        for n in range(n_blocks):
            kb = block_rows(k_refs, g, n, h)
            mean_rows.append(jnp.sum(kb, axis=0, keepdims=True) / MOBA_BLOCK)
            s_parts.append(lax.dot_general(qs[-1], kb.astype(BF16), _NT, preferred_element_type=F32))
        mean_rows.append(jnp.zeros((LANES - n_blocks, dh), F32))
        gate_rows.append(lax.dot_general(q, jnp.concatenate(mean_rows, axis=0), _NT,
                                         precision=lax.Precision.HIGHEST, preferred_element_type=F32))
        s_rows.append(jnp.concatenate(s_parts, axis=1))
    sel = _select_blocks(jnp.concatenate(gate_rows, axis=0), n_blocks)
    sel_keys = jnp.dot(sel.astype(BF16), e_ref[...], preferred_element_type=F32)
    s = jnp.where(sel_keys > 0.5, jnp.concatenate(s_rows, axis=0), NEG_INF)

    pad = jnp.zeros((LANES - rows, dh), F32)
    padded = lambda ref, g, h: jnp.concatenate([ref[g, :, h * dh:(h + 1) * dh], pad], axis=0).astype(BF16)
    s_own = jnp.concatenate(
        [lax.dot_general(qs[i], padded(kn_ref, g, h), _NT, preferred_element_type=F32)
         for i, (g, h) in enumerate(pairs)], axis=0)
    t_row = lax.broadcasted_iota(jnp.int32, s_own.shape, 0) % rows
    key = lax.broadcasted_iota(jnp.int32, s_own.shape, 1)
    s_own = jnp.where(key <= t_row, s_own, NEG_INF)

    m = jnp.maximum(jnp.max(s, axis=-1, keepdims=True), jnp.max(s_own, axis=-1, keepdims=True))
    p = jnp.exp(s - m)
    p_own = jnp.exp(s_own - m)
    inv = 1.0 / (jnp.sum(p, axis=-1, keepdims=True) + jnp.sum(p_own, axis=-1, keepdims=True))
    pb = p.astype(BF16)
    pb_own = p_own.astype(BF16)
    for i, (g, h) in enumerate(pairs):
        r0 = i * rows
        o = jnp.dot(pb_own[r0:r0 + rows], padded(vn_ref, g, h), preferred_element_type=F32)
        for n in range(n_blocks):
            o = o + jnp.dot(pb[r0:r0 + rows, n * MOBA_BLOCK:(n + 1) * MOBA_BLOCK],
                            block_rows(v_refs, g, n, h).astype(BF16), preferred_element_type=F32)
        o_ref[g, :, h * dh:(h + 1) * dh] = o * inv[r0:r0 + rows]


def _moba_sample(q, k_new, v_new, cache_k, cache_v, page_table, l, n_seq=2):
    bsz, rows, _ = q.shape
    n_pages = page_table.shape[1]
    n_phys = cache_k.shape[0] // DEPTH
    past_len = n_pages * PAGE_SIZE
    assert past_len % MOBA_BLOCK == 0 and past_len // MOBA_BLOCK <= SUBLANES and bsz % n_seq == 0
    base = l * n_phys
    tok_spec = pl.BlockSpec((n_seq, rows, ATT_WIDTH), lambda b, pt: (b, 0, 0))
    page_specs = [
        pl.BlockSpec((1, PAGE_SIZE * ATT_HEADS, ATT_HEAD_DIM),
                     lambda b, pt, g=g, j=j: (base + pt[(b * n_seq + g) * n_pages + j], 0, 0))
        for g in range(n_seq) for j in range(n_pages)
    ]
    grid_spec = pltpu.PrefetchScalarGridSpec(
        num_scalar_prefetch=1,
        grid=(bsz // n_seq,),
        in_specs=[tok_spec, tok_spec, tok_spec,
                  pl.BlockSpec((LANES, past_len), lambda b, pt: (0, 0))] + page_specs + page_specs,
        out_specs=tok_spec,
    )
    return pl.pallas_call(
        functools.partial(_moba_sample_kernel, n_pages=n_pages, n_seq=n_seq),
        grid_spec=grid_spec,
        out_shape=jax.ShapeDtypeStruct((bsz, rows, ATT_WIDTH), F32),
        compiler_params=_params(1),
        name="moba_sample",
    )(page_table.reshape(-1), q, k_new, v_new, _block_expander(past_len),
      *([cache_k] * (n_seq * n_pages)), *([cache_v] * (n_seq * n_pages)))


def _pool_windows_to_lanes(sums, counts, cur):
    lane = lax.broadcasted_iota(jnp.int32, cur.shape, 1)
    d = sums[-1] / counts[-1]
    for g in range(len(POOL_WINDOWS) - 2, -1, -1):
        d = jnp.where(lane < (g + 1) * POOL_GROUP_WIDTH, sums[g] / counts[g], d)
    return d - cur


def _conv_tail(y, ln_g, ln_b, pw_w, pw_b):
    yc = y - jnp.mean(y, axis=-1, keepdims=True)
    yn = yc * lax.rsqrt(jnp.mean(yc * yc, axis=-1, keepdims=True) + EPS) * ln_g + ln_b
    act = yn * jax.nn.sigmoid(yn)
    return jnp.dot(act.astype(BF16), pw_w, preferred_element_type=F32) + pw_b


POOL_HALO = 16
CONV_HALO = 32


def _poolconv_prompt_kernel(x_ref, u_ref, ag_ref, ya_ref, pw_ref, ps_ref, dw_ref, db_ref, lg_ref, lb_ref,
                            cw_ref, cb_ref, wp_ref, wc_ref, wa_ref, o_ref, ht_ref, e_ref, h_ref, hs_ref, *, tt):
    ti = pl.program_id(1)

    @pl.when(ti == 0)
    def _():
        e_ref[0:POOL_HALO, :] = jnp.zeros((POOL_HALO, POOL_WIDTH), F32)
        h_ref[0:CONV_HALO, :] = jnp.zeros((CONV_HALO, CONV_WIDTH), F32)

    u = u_ref[...]
    e_ref[POOL_HALO:POOL_HALO + tt, :] = u
    a = ag_ref[:, 0:CONV_WIDTH]
    gate = ag_ref[:, CONV_WIDTH:2 * CONV_WIDTH]
    h_ref[CONV_HALO:CONV_HALO + tt, :] = a * jax.nn.sigmoid(gate)

    pos = ti * tt + lax.broadcasted_iota(jnp.int32, (tt, 1), 0)
    sums, counts = [], []
    run = u
    taken = 1
    for win in POOL_WINDOWS:
        while taken < win:
            run = run + e_ref[POOL_HALO - taken:POOL_HALO - taken + tt, :]
            taken += 1
        sums.append(run)
        counts.append(jnp.minimum(pos + 1, win).astype(F32))
    d = _pool_windows_to_lanes(sums, counts, u)
    y_pool = jnp.dot(d.astype(BF16), pw_ref[...], preferred_element_type=F32) * ps_ref[...]

    span = tt + CONV_HALO - SUBLANES
    for s in range(1, SUBLANES):
        hs_ref[s - 1, 0:span, :] = h_ref[s:s + span, :]
    y = jnp.zeros((tt, CONV_WIDTH), F32) + db_ref[...]
    for j in range(CONV_K):
        r0 = CONV_HALO - CONV_BUF + j
        s = r0 % SUBLANES
        rows = h_ref[r0:r0 + tt, :] if s == 0 else hs_ref[s - 1, r0 - s:r0 - s + tt, :]
        y = y + rows * dw_ref[j:j + 1, :]
    y_conv = _conv_tail(y, lg_ref[...], lb_ref[...], cw_ref[...], cb_ref[...])

    out = x_ref[...] + jnp.dot(y_pool.astype(BF16), wp_ref[...], preferred_element_type=F32)
    out = out + jnp.dot(y_conv.astype(BF16), wc_ref[...], preferred_element_type=F32)
    o_ref[...] = out + jnp.dot(ya_ref[...].astype(BF16), wa_ref[...], preferred_element_type=F32)

    ht_ref[0] = h_ref[tt:tt + CONV_HALO, :]
    e_ref[0:POOL_HALO, :] = e_ref[tt:tt + POOL_HALO, :]
    h_ref[0:CONV_HALO, :] = h_ref[tt:tt + CONV_HALO, :]


def _mix_out_prompt(x, u, ag, y_att, bsz, t, weights, w_out, l, tt=512):
    pool_w, pool_s, dw_w, dw_b, ln_g, ln_b, pw_w, pw_b = weights
    nt = t // tt
    row = lambda b, i: (b * nt + i, 0)
    w_rows = lambda width, blk: pl.BlockSpec((None, width, D_MODEL), lambda b, i: (l, blk, 0))
    return pl.pallas_call(
        functools.partial(_poolconv_prompt_kernel, tt=tt),
        grid=(bsz, nt),
        in_specs=[
            pl.BlockSpec((tt, D_MODEL), row),
            pl.BlockSpec((tt, POOL_WIDTH), row),
            pl.BlockSpec((tt, 2 * CONV_WIDTH), row),
            pl.BlockSpec((tt, ATT_WIDTH), row),
            _layer_spec((POOL_WIDTH, POOL_WIDTH), l, 2),
            _layer_spec((1, POOL_WIDTH), l, 2),
            _layer_spec((CONV_HALO, CONV_WIDTH), l, 2),
            _layer_spec((1, CONV_WIDTH), l, 2),
            _layer_spec((1, CONV_WIDTH), l, 2),
            _layer_spec((1, CONV_WIDTH), l, 2),
            _layer_spec((CONV_WIDTH, CONV_WIDTH), l, 2),
            _layer_spec((1, CONV_WIDTH), l, 2),
            w_rows(POOL_WIDTH, 0),
            w_rows(CONV_WIDTH, POOL_WIDTH // CONV_WIDTH),
            w_rows(ATT_WIDTH, (POOL_WIDTH + CONV_WIDTH) // ATT_WIDTH),
        ],
        out_specs=[
            pl.BlockSpec((tt, D_MODEL), row),
            pl.BlockSpec((1, CONV_HALO, CONV_WIDTH), lambda b, i: (b, 0, 0)),
        ],
        out_shape=[
            jax.ShapeDtypeStruct((bsz * t, D_MODEL), F32),
            jax.ShapeDtypeStruct((bsz, CONV_HALO, CONV_WIDTH), F32),
        ],
        scratch_shapes=[
            pltpu.VMEM((POOL_HALO + tt, POOL_WIDTH), F32),
            pltpu.VMEM((CONV_HALO + tt, CONV_WIDTH), F32),
            pltpu.VMEM((SUBLANES - 1, CONV_HALO + tt, CONV_WIDTH), F32),
        ],
        compiler_params=_params(2),
        name="mix_out_prompt",
    )(x, u, ag, y_att, pool_w, pool_s, dw_w, dw_b, ln_g, ln_b, pw_w, pw_b, w_out, w_out, w_out)


def _poolconv_sample_kernel(sp_ref, u_ref, sc_ref, ag_ref, pw_ref, ps_ref, dw_ref, db_ref, lg_ref, lb_ref,
                            cw_ref, cb_ref, yp_ref, yc_ref, h_ref, *, n_new, pos0):
    def pool_row(r):
        return sp_ref[r] if r < POOL_BUF else u_ref[r - POOL_BUF]

    for t in range(n_new):
        h_ref[t] = ag_ref[t, :, 0:CONV_WIDTH] * jax.nn.sigmoid(ag_ref[t, :, CONV_WIDTH:2 * CONV_WIDTH])

    def conv_row(r):
        return sc_ref[r] if r < CONV_BUF else h_ref[r - CONV_BUF]

    for t in range(n_new):
        cur = u_ref[t]
        sums, counts = [], []
        run = cur
        taken = 1
        for win in POOL_WINDOWS:
            while taken < win:
                run = run + pool_row(POOL_BUF + t - taken)
                taken += 1
            sums.append(run)
            counts.append(float(min(pos0 + t + 1, win)))
        d = _pool_windows_to_lanes(sums, counts, cur)
        yp_ref[t] = jnp.dot(d.astype(BF16), pw_ref[...], preferred_element_type=F32) * ps_ref[...]

        y = jnp.zeros(cur.shape, F32) + db_ref[...]
        for j in range(CONV_K):
            y = y + conv_row(t + j) * dw_ref[j:j + 1, :]
        yc_ref[t] = _conv_tail(y, lg_ref[...], lb_ref[...], cw_ref[...], cb_ref[...])


def _poolconv_sample(state_pool_t, u_t, state_conv_t, ag_t, pos0, weights, l):
    pool_w, pool_s, dw_w, dw_b, ln_g, ln_b, pw_w, pw_b = weights
    n_new, bsz, _ = u_t.shape
    full = lambda a: pl.BlockSpec(a.shape, lambda i: (0,) * a.ndim)
    out = jax.ShapeDtypeStruct((n_new, bsz, CONV_WIDTH), F32)
    return pl.pallas_call(
        functools.partial(_poolconv_sample_kernel, n_new=n_new, pos0=pos0),
        grid=(1,),
        in_specs=[
            full(state_pool_t), full(u_t), full(state_conv_t), full(ag_t),
            _layer_spec((POOL_WIDTH, POOL_WIDTH), l, 1),
            _layer_spec((1, POOL_WIDTH), l, 1),
            _layer_spec((CONV_HALO, CONV_WIDTH), l, 1),
            _layer_spec((1, CONV_WIDTH), l, 1),
            _layer_spec((1, CONV_WIDTH), l, 1),
            _layer_spec((1, CONV_WIDTH), l, 1),
            _layer_spec((CONV_WIDTH, CONV_WIDTH), l, 1),
            _layer_spec((1, CONV_WIDTH), l, 1),
        ],
        out_specs=[pl.BlockSpec(out.shape, lambda i: (0, 0, 0))] * 3,
        out_shape=[out, out, out],
        compiler_params=_params(1),
        name="poolconv_sample",
    )(state_pool_t, u_t, state_conv_t, ag_t, pool_w, pool_s, dw_w, dw_b, ln_g, ln_b, pw_w, pw_b)


def _cross_kernel(x_ref, g_ref, wq_ref, hg_ref, k_ref, v_ref, wo_ref, o_ref, *, n_seq, interleaved):
    hd = MEM_HEAD_DIM
    rows = x_ref.shape[1]
    x = x_ref[...].reshape(n_seq * rows, D_MODEL)
    xn = _rms(x, g_ref[...]).astype(BF16)
    q_all = jnp.dot(xn, wq_ref[...], preferred_element_type=F32)
    halves = range(hd // LANES)

    def head_slab(ref, g, h):
        if not interleaved:
            return ref[g, :, h * hd:(h + 1) * hd]
        return jnp.concatenate(
            [ref[g, pl.ds(half * MEM_HEADS + h, N_MEM, stride=len(halves) * MEM_HEADS), :] for half in halves],
            axis=1)

    pairs = [(g, h) for g in range(n_seq) for h in range(MEM_HEADS)]
    scores = []
    for g, h in pairs:
        q = _rms(q_all[g * rows:(g + 1) * rows, h * hd:(h + 1) * hd], hg_ref[...])
        q = (q * (hd ** -0.5)).astype(BF16)
        scores.append(lax.dot_general(q, head_slab(k_ref, g, h).astype(BF16), _NT, preferred_element_type=F32))
    s = jnp.concatenate(scores, axis=0)
    p = jnp.exp(s - jnp.max(s, axis=-1, keepdims=True))
    inv = 1.0 / jnp.sum(p, axis=-1, keepdims=True)
    o_seqs = []
    for g in range(n_seq):
        o_heads = []
        for h in range(MEM_HEADS):
            r0 = (g * MEM_HEADS + h) * rows
            o = jnp.dot(p[r0:r0 + rows].astype(BF16), head_slab(v_ref, g, h).astype(BF16),
                        preferred_element_type=F32)
            o_heads.append((o * inv[r0:r0 + rows]).astype(BF16))
        o_seqs.append(jnp.concatenate(o_heads, axis=1))
    o_all = o_seqs[0] if n_seq == 1 else jnp.concatenate(o_seqs, axis=0)
    y = x + jnp.dot(o_all, wo_ref[...], preferred_element_type=F32)
    o_ref[...] = y.reshape(n_seq, rows, D_MODEL)


def _interleaved_memory(cache):
    depth, s, n_mem, heads, hd = cache.shape
    halves = hd // LANES
    c = cache.reshape(depth, s, n_mem, heads, halves, LANES).transpose(0, 1, 2, 4, 3, 5)
    return c.reshape(depth * s, n_mem * halves * heads, LANES)


def _cross(x, ln, w_q, head_gain, mem_k, mem_v, w_o, l, n_seq, tq, seq0=None):
    s_total, r, _ = x.shape
    interleaved = seq0 is not None
    if interleaved:
        assert seq0 % n_seq == 0
        kv_spec = pl.BlockSpec((n_seq,) + mem_k.shape[1:], lambda s, i: (seq0 // n_seq + s, 0, 0))
    else:
        kv_spec = pl.BlockSpec((n_seq, N_MEM, D_MODEL), lambda s, i: (s, 0, 0))
    x_spec = pl.BlockSpec((n_seq, tq, D_MODEL), lambda s, i: (s, i, 0))
    return pl.pallas_call(
        functools.partial(_cross_kernel, n_seq=n_seq, interleaved=interleaved),
        grid=(s_total // n_seq, r // tq),
        in_specs=[x_spec, _layer_spec((1, D_MODEL), l, 2), _layer_spec((D_MODEL, D_MODEL), l, 2),
                  _layer_spec((1, MEM_HEAD_DIM), l, 2), kv_spec, kv_spec,
                  _layer_spec((D_MODEL, D_MODEL), l, 2)],
        out_specs=x_spec,
        out_shape=jax.ShapeDtypeStruct(x.shape, F32),
        compiler_params=_params(2),
        name="cross",
    )(x, ln, w_q, head_gain, mem_k, mem_v, w_o)


def _pad_rows(x, rows):
    return jnp.pad(x, ((0, 0), (0, rows - x.shape[1]), (0, 0)))


def kernel(x_prompt, x_sample, cache_att_k, cache_att_v, cache_mem_k, cache_mem_v, state_pool, state_conv, page_table, mem_prompt, ln_ffn1, w_ffn1_in, w_ffn1_out, ln_mix, w_in, pool_w, pool_scale, conv_dw_w, conv_dw_b, conv_ln_g, conv_ln_b, conv_pw_w, conv_pw_b, att_q_norm, att_k_norm, w_out, ln_cross, ln_mem, w_cq, w_ckv, cq_norm, ck_norm, w_co, ln_ffn2, w_ffn2_in, w_ffn2_out):
    bsz, seq, _ = x_prompt.shape
    dbsz, dseq, _ = x_sample.shape
    n_pages = page_table.shape[1]
    past_len = n_pages * PAGE_SIZE
    n_phys = cache_att_k.shape[1]

    bf = lambda w: w.astype(BF16)
    rowv = lambda v: v[:, None, :]
    w1i, w1o, w2i, w2o = bf(w_ffn1_in), bf(w_ffn1_out), bf(w_ffn2_in), bf(w_ffn2_out)
    w_in_b, w_out_b, w_cq_b, w_ckv_b, w_co_b = bf(w_in), bf(w_out), bf(w_cq), bf(w_ckv), bf(w_co)
    groups = len(POOL_WINDOWS)
    eye = jnp.eye(groups, dtype=F32)
    pool_bd = bf((pool_w[:, :, :, None, :] * eye[None, :, None, :, None])
                 .reshape(DEPTH, POOL_WIDTH, POOL_WIDTH))
    dw_pad = jnp.pad(conv_dw_w, ((0, 0), (0, CONV_HALO - CONV_K), (0, 0)))
    mixer_w = (pool_bd, rowv(pool_scale), dw_pad, rowv(conv_dw_b), rowv(conv_ln_g), rowv(conv_ln_b),
               bf(conv_pw_w), rowv(conv_pw_b))
    ln1, lnm, lnc, lnmem, ln2 = rowv(ln_ffn1), rowv(ln_mix), rowv(ln_cross), rowv(ln_mem), rowv(ln_ffn2)

    tables_p = _rope_tables(jnp.arange(seq, dtype=jnp.int32))
    tm_s = dbsz * dseq
    tables_s = _rope_tables(past_len + jnp.arange(tm_s, dtype=jnp.int32) % dseq)

    c1 = POOL_WIDTH
    c3 = c1 + 2 * CONV_WIDTH
    c4 = c3 + ATT_WIDTH
    c5 = c4 + ATT_WIDTH
    mix_segs = [(c3, ATT_WIDTH, ATT_HEAD_DIM, 0, True, False), (c4, ATT_WIDTH, ATT_HEAD_DIM, 1, True, True),
                (c5, ATT_WIDTH, ATT_HEAD_DIM, None, False, True),
                (0, c1, None, None, False, False), (c1, 2 * CONV_WIDTH, None, None, False, False)]
    mem_segs = [(0, D_MODEL, MEM_HEAD_DIM, 0, False, False), (D_MODEL, D_MODEL, None, None, False, False)]
    out_rows = (0, c1, c1 + CONV_WIDTH)

    cache_k = cache_att_k.reshape(DEPTH * n_phys, PAGE_SIZE * ATT_HEADS, ATT_HEAD_DIM)
    cache_v = cache_att_v.reshape(DEPTH * n_phys, PAGE_SIZE * ATT_HEADS, ATT_HEAD_DIM)
    mem_k_s = _interleaved_memory(cache_mem_k)
    mem_v_s = _interleaved_memory(cache_mem_v)
    mem_rows = mem_prompt.reshape(bsz * N_MEM, D_MODEL)

    xp = x_prompt.reshape(bsz * seq, D_MODEL)
    xs = x_sample.reshape(dbsz * dseq, D_MODEL)
    rows_pad = SUBLANES
    outs = {name: [] for name in ("p_mk", "p_mv", "p_pool", "p_conv", "s_pool", "s_conv")}
    cq_gain = rowv(cq_norm)
    kv_p = kv_s = None

    for l in range(DEPTH):
        att_gains = jnp.stack([att_q_norm[l], att_k_norm[l]])

        mk, mv = _norm_proj(mem_rows, lnmem, w_ckv_b, l, ck_norm[l][None], mem_segs)
        xp = _ffn(xp, ln1, w1i, w1o, l)
        q, k, v, u, ag = _norm_proj(xp, lnm, w_in_b, l, att_gains, mix_segs, tables=tables_p, stacked=kv_p)
        kv_p = (k, v)
        y_att = _moba_prompt(q, k, v, bsz, seq, l)
        xp, h_tail = _mix_out_prompt(xp, u, ag, y_att, bsz, seq, mixer_w, w_out_b, l)
        xp = _cross(xp.reshape(bsz, seq, D_MODEL), lnc, w_cq_b, cq_gain, mk.reshape(bsz, N_MEM, D_MODEL),
                    mv.reshape(bsz, N_MEM, D_MODEL), w_co_b, l, 1, 512).reshape(bsz * seq, D_MODEL)
        xp = _ffn(xp, ln2, w2i, w2o, l)
        outs["p_mk"].append(mk.reshape(bsz, N_MEM, MEM_HEADS, MEM_HEAD_DIM))
        outs["p_mv"].append(mv.reshape(bsz, N_MEM, MEM_HEADS, MEM_HEAD_DIM))
        outs["p_pool"].append(u.reshape(bsz, seq, POOL_WIDTH)[:, seq - POOL_BUF:])
        outs["p_conv"].append(h_tail[:, CONV_HALO - CONV_BUF:])

        xs = _ffn(xs, ln1, w1i, w1o, l, tm=tm_s)
        q, k, v, u, ag = _norm_proj(xs, lnm, w_in_b, l, att_gains, mix_segs, tables=tables_s, tm=tm_s,
                                    stacked=kv_s)
        kv_s = (k, v)
        pad3 = lambda a: _pad_rows(a.reshape(dbsz, dseq, ATT_WIDTH), 2 * SUBLANES)
        y_att = _moba_sample(pad3(q), pad3(k[l]), pad3(v[l]), cache_k, cache_v, page_table, l)
        y_att = y_att[:, :dseq].reshape(dbsz * dseq, ATT_WIDTH)
        tmaj = lambda a: jnp.swapaxes(a, 0, 1)
        u3 = u.reshape(dbsz, dseq, POOL_WIDTH)
        y_pool, y_conv, h_new = _poolconv_sample(
            tmaj(state_pool[l]), tmaj(u3), tmaj(state_conv[l]), tmaj(ag.reshape(dbsz, dseq, 2 * CONV_WIDTH)),
            past_len, mixer_w, l)
        bmaj = lambda a: jnp.swapaxes(a, 0, 1).reshape(dbsz * dseq, a.shape[-1])
        xs = _out_proj(xs, [bmaj(y_pool), bmaj(y_conv), y_att], w_out_b, l, out_rows, tm=tm_s)
        n_seq = 4
        xs = _cross(_pad_rows(xs.reshape(dbsz, dseq, D_MODEL), rows_pad), lnc, w_cq_b, cq_gain, mem_k_s, mem_v_s,
                    w_co_b, l, n_seq, rows_pad, seq0=l * dbsz)[:, :dseq].reshape(dbsz * dseq, D_MODEL)
        xs = _ffn(xs, ln2, w2i, w2o, l, tm=tm_s)
        outs["s_pool"].append(jnp.concatenate([state_pool[l][:, dseq:], u3], axis=1))
        outs["s_conv"].append(jnp.concatenate([state_conv[l][:, dseq:], jnp.swapaxes(h_new, 0, 1)], axis=1))

    st = lambda name: jnp.stack(outs[name])
    heads5 = lambda a, b, t: a.reshape(DEPTH, b, t, ATT_HEADS, ATT_HEAD_DIM)
    return (xp.reshape(bsz, seq, D_MODEL), xs.reshape(dbsz, dseq, D_MODEL),
            heads5(kv_p[0], bsz, seq), heads5(kv_p[1], bsz, seq), st("p_mk"), st("p_mv"), st("p_pool"),
            st("p_conv"), heads5(kv_s[0], dbsz, dseq), heads5(kv_s[1], dbsz, dseq), st("s_pool"), st("s_conv"))
```

```python
import functools

import jax
import jax.numpy as jnp
import numpy as np
from jax import lax
from jax.experimental import pallas as pl
from jax.experimental.pallas import tpu as pltpu

D_MODEL = 1024
DEPTH = 2
PAGE_SIZE = 128

POOL_WIDTH = D_MODEL // 4
POOL_WINDOWS = (2, 4, 8, 16)
POOL_GROUP_WIDTH = POOL_WIDTH // len(POOL_WINDOWS)
POOL_BUF = max(POOL_WINDOWS) - 1

CONV_WIDTH = D_MODEL // 4
CONV_K = 31
CONV_BUF = CONV_K - 1

ATT_HEADS = 4
ATT_HEAD_DIM = D_MODEL // 8
ATT_WIDTH = ATT_HEADS * ATT_HEAD_DIM
ROT_DIM = ATT_HEAD_DIM // 4
ROPE_THETA = 500000.0
MOBA_BLOCK = 256
MOBA_TOPK = 3

IN_WIDTH = POOL_WIDTH + 2 * CONV_WIDTH + 3 * ATT_WIDTH

N_MEM = 256
MEM_HEADS = 4
MEM_HEAD_DIM = D_MODEL // MEM_HEADS

D_FF = ((8 * D_MODEL // 3 + 127) // 128) * 128
EPS = 1e-6

LANES = 128
SUBLANES = 8
VMEM_LIMIT_BYTES = 56 * 1024 * 1024

F32 = jnp.float32
BF16 = jnp.bfloat16
NEG_INF = float("-inf")

_NT = (((1,), (1,)), ((), ()))


def _params(n_axes):
    return pltpu.CompilerParams(dimension_semantics=("arbitrary",) * n_axes,
                                vmem_limit_bytes=VMEM_LIMIT_BYTES)


def _rms(x, g):
    return x * lax.rsqrt(jnp.mean(x * x, axis=-1, keepdims=True) + EPS) * g


def _layer_spec(shape, l, n_grid):
    zeros = (0,) * len(shape)
    if n_grid == 1:
        return pl.BlockSpec((None,) + tuple(shape), lambda i: (l,) + zeros)
    if n_grid == 2:
        return pl.BlockSpec((None,) + tuple(shape), lambda i, j: (l,) + zeros)
    return pl.BlockSpec((None,) + tuple(shape), lambda i, j, k: (l,) + zeros)


FFN_CHUNK = 256


def _ffn_kernel(x_ref, g_ref, wi_ref, wo_ref, o_ref):
    x = x_ref[...]
    xn = _rms(x, g_ref[...]).astype(BF16)
    acc = None
    for c in range(D_FF // FFN_CHUNK):
        lo = c * FFN_CHUNK
        gate = jnp.dot(xn, wi_ref[:, lo:lo + FFN_CHUNK], preferred_element_type=F32)
        up = jnp.dot(xn, wi_ref[:, D_FF + lo:D_FF + lo + FFN_CHUNK], preferred_element_type=F32)
        act = (gate * jax.nn.sigmoid(gate) * up).astype(BF16)
        part = jnp.dot(act, wo_ref[lo:lo + FFN_CHUNK, :], preferred_element_type=F32)
        acc = part if acc is None else acc + part
    o_ref[...] = x + 0.5 * acc


def _ffn(x, ln, w_in, w_out, l, tm=512):
    m = x.shape[0]
    return pl.pallas_call(
        _ffn_kernel,
        grid=(m // tm,),
        in_specs=[
            pl.BlockSpec((tm, D_MODEL), lambda i: (i, 0)),
            _layer_spec((1, D_MODEL), l, 1),
            _layer_spec((D_MODEL, 2 * D_FF), l, 1),
            _layer_spec((D_FF, D_MODEL), l, 1),
        ],
        out_specs=pl.BlockSpec((tm, D_MODEL), lambda i: (i, 0)),
        out_shape=jax.ShapeDtypeStruct((m, D_MODEL), F32),
        compiler_params=_params(1),
        name="ffn",
    )(x, ln, w_in, w_out)


def _norm_proj_kernel(*refs, segs, rope):
    x_ref, g_ref, w_ref, hg_ref = refs[:4]
    if rope:
        cos_ref, sin_lo_ref, sin_hi_ref = refs[4:7]
    out_refs = refs[len(refs) - len(segs):]
    tm = x_ref.shape[0]
    xn = _rms(x_ref[...], g_ref[...]).astype(BF16)
    for (c0, width, head_dim, gain_row, use_rope, head_rows), o_ref in zip(segs, out_refs):
        z = jnp.dot(xn, w_ref[:, c0:c0 + width], preferred_element_type=F32)
        if head_dim is None:
            o_ref[...] = z
            continue
        n_heads = width // head_dim
        for h in range(n_heads):
            zh = z[:, h * head_dim:(h + 1) * head_dim]
            if gain_row is not None:
                zh = _rms(zh, hg_ref[gain_row:gain_row + 1, :])
            if use_rope:
                half = ROT_DIM // 2
                zh = (zh * cos_ref[...]
                      + pltpu.roll(zh, half, 1) * sin_hi_ref[...]
                      + pltpu.roll(zh, LANES - half, 1) * sin_lo_ref[...])
            if not head_rows:
                o_ref[:, h * head_dim:(h + 1) * head_dim] = zh
                continue
            tiles = head_dim // LANES
            for tile in range(tiles):
                rows = pl.ds(tile * n_heads + h, tm, stride=n_heads * tiles)
                piece = zh[:, tile * LANES:(tile + 1) * LANES]
                if len(o_ref.shape) == 2:
                    o_ref[rows, :] = piece
                else:
                    for d in range(o_ref.shape[0]):
                        o_ref[d, rows, :] = piece


def _norm_proj(x, ln, w, l, head_gains, segs, tables=None, tm=512, stacked=None):
    m = x.shape[0]
    n_cols = w.shape[-1]
    rope = tables is not None
    in_specs = [
        pl.BlockSpec((tm, D_MODEL), lambda i: (i, 0)),
        _layer_spec((1, D_MODEL), l, 1),
        _layer_spec((D_MODEL, n_cols), l, 1),
        pl.BlockSpec(head_gains.shape, lambda i: (0, 0)),
    ]
    args = [x, ln, w, head_gains]
    if rope:
        period = tables[0].shape[0] // tm
        for t in tables:
            in_specs.append(pl.BlockSpec((tm, LANES), lambda i: (i % period, 0)))
            args.append(t)
    out_specs, out_shape, aliases = [], [], {}
    for k, s in enumerate(segs):
        width, head_dim, head_rows = s[1], s[2], s[5]
        if not head_rows:
            out_specs.append(pl.BlockSpec((tm, width), lambda i: (i, 0)))
            out_shape.append(jax.ShapeDtypeStruct((m, width), F32))
            continue
        per_token = width // LANES
        out_shape.append(jax.ShapeDtypeStruct((DEPTH, m * per_token, LANES), F32))
        if stacked is None:
            out_specs.append(pl.BlockSpec((DEPTH, tm * per_token, LANES), lambda i: (0, i, 0)))
        else:
            out_specs.append(pl.BlockSpec((None, tm * per_token, LANES), lambda i: (l, i, 0)))
            aliases[len(args)] = k
            in_specs.append(pl.BlockSpec(memory_space=pl.ANY))
            args.append(stacked[len(aliases) - 1])
    return pl.pallas_call(
        functools.partial(_norm_proj_kernel, segs=tuple(segs), rope=rope),
        grid=(m // tm,),
        in_specs=in_specs,
        out_specs=out_specs,
        out_shape=out_shape,
        input_output_aliases=aliases,
        compiler_params=_params(1),
        name="norm_proj",
    )(*args)


def _rope_tables(positions):
    half = ROT_DIM // 2
    inv_freq = jnp.power(jnp.float32(ROPE_THETA), -jnp.arange(half, dtype=F32) / half)
    ang = positions.astype(F32)[:, None] * inv_freq[None, :]
    cos, sin = jnp.cos(ang), jnp.sin(ang)
    n = positions.shape[0]
    rest = ATT_HEAD_DIM - ROT_DIM
    cos_t = jnp.concatenate([cos, cos, jnp.ones((n, rest), F32)], axis=1)
    sin_lo = jnp.concatenate([-sin, jnp.zeros((n, half + rest), F32)], axis=1)
    sin_hi = jnp.concatenate([jnp.zeros((n, half), F32), sin, jnp.zeros((n, rest), F32)], axis=1)
    return cos_t, sin_lo, sin_hi


def _out_proj_kernel(*refs, n_terms):
    x_ref = refs[0]
    y_refs = refs[1:1 + n_terms]
    w_refs = refs[1 + n_terms:1 + 2 * n_terms]
    o_ref = refs[1 + 2 * n_terms]
    acc = x_ref[...]
    for y_ref, w_ref in zip(y_refs, w_refs):
        acc = acc + jnp.dot(y_ref[...].astype(BF16), w_ref[...], preferred_element_type=F32)
    o_ref[...] = acc


def _out_proj(x, ys, w, l, row_starts, tm=512):
    m = x.shape[0]
    in_specs = [pl.BlockSpec((tm, D_MODEL), lambda i: (i, 0))]
    for y in ys:
        in_specs.append(pl.BlockSpec((tm, y.shape[1]), lambda i: (i, 0)))
    for y, r0 in zip(ys, row_starts):
        width = y.shape[1]
        blk = r0 // width
        in_specs.append(pl.BlockSpec((None, width, D_MODEL), lambda i, blk=blk: (l, blk, 0)))
    return pl.pallas_call(
        functools.partial(_out_proj_kernel, n_terms=len(ys)),
        grid=(m // tm,),
        in_specs=in_specs,
        out_specs=pl.BlockSpec((tm, D_MODEL), lambda i: (i, 0)),
        out_shape=jax.ShapeDtypeStruct((m, D_MODEL), F32),
        compiler_params=_params(1),
        name="out_proj",
    )(x, *ys, *([w] * len(ys)))


def _select_blocks(gate, n_allowed):
    lane = lax.broadcasted_iota(jnp.int32, gate.shape, 1)
    allowed = lane < n_allowed
    g = jnp.where(allowed, gate, NEG_INF)
    rank = jnp.zeros(gate.shape, F32)
    for r in range(1, SUBLANES):
        lower = pltpu.roll(g, r, 1)
        higher = pltpu.roll(g, LANES - r, 1)
        rank = rank + jnp.where(lower >= g, 1.0, 0.0) + jnp.where(higher > g, 1.0, 0.0)
    return jnp.where(allowed, jnp.where(rank < MOBA_TOPK, 1.0, 0.0), 0.0)


def _block_expander(n_keys):
    blk = np.arange(n_keys) // MOBA_BLOCK
    return jnp.asarray((np.arange(LANES)[:, None] == blk[None, :]).astype(np.float32), dtype=BF16)


MASK_BIAS = -1e30


def _select_blocks_t(gate_t, n_allowed):
    blk = lax.broadcasted_iota(jnp.int32, gate_t.shape, 0)
    allowed = blk < n_allowed
    g = jnp.where(allowed, gate_t, NEG_INF)
    rank = jnp.zeros(gate_t.shape, F32)
    for m in range(SUBLANES):
        gm = g[m:m + 1, :]
        tie = jnp.where(gm == g, jnp.where(blk > m, 1.0, 0.0), 0.0)
        rank = rank + jnp.where(gm > g, 1.0, tie)
    return jnp.where(allowed, jnp.where(rank < MOBA_TOPK, 1.0, 0.0), 0.0)


def _moba_prompt_kernel(q_ref, k_ref, v_ref, kbias_ref, o_ref, ka_ref, vb_ref, km_ref, *, n_blocks):
    qi = pl.program_id(1)
    blk = MOBA_BLOCK
    dh = ATT_HEAD_DIM
    heads = range(ATT_HEADS)

    @pl.when(qi == 0)
    def _():
        km_ref[...] = jnp.zeros(km_ref.shape, F32)
        for h in heads:
            vb_ref[:, h * dh:(h + 1) * dh] = v_ref[pl.ds(h, n_blocks * blk, stride=ATT_HEADS), :].astype(BF16)
            ka_ref[h, :, dh:2 * dh] = kbias_ref[h]
            for n in range(n_blocks):
                kb = k_ref[pl.ds(n * blk * ATT_HEADS + h, blk, stride=ATT_HEADS), :]
                ka_ref[h, n * blk:(n + 1) * blk, 0:dh] = kb.astype(BF16)
                r = h * SUBLANES + n
                km_ref[r:r + 1, h * dh:(h + 1) * dh] = jnp.mean(kb, axis=0, keepdims=True)

    row = lax.broadcasted_iota(jnp.int32, (blk, blk), 0)
    col = lax.broadcasted_iota(jnp.int32, (blk, blk), 1)

    def step(c):
        n_keys = (c + 1) * blk
        gated = c > MOBA_TOPK
        if gated:
            qf = q_ref[...]
            q_hi = qf.astype(BF16)
            q_lo = (qf - q_hi.astype(F32)).astype(BF16)
            km = km_ref[...]
            km_hi = km.astype(BF16)
            km_lo = (km - km_hi.astype(F32)).astype(BF16)
            g_hi = lax.dot_general(jnp.concatenate([km_hi, km_lo], axis=0), q_hi, _NT, preferred_element_type=F32)
            g_lo = lax.dot_general(km_hi, q_lo, _NT, preferred_element_type=F32)
            n_rows = ATT_HEADS * SUBLANES
            gate_all = g_hi[0:n_rows] + g_hi[n_rows:2 * n_rows] + g_lo
            flags = []
            for h in heads:
                gate_t = gate_all[h * SUBLANES:(h + 1) * SUBLANES]
                blk_id = lax.broadcasted_iota(jnp.int32, gate_t.shape, 0)
                flags.append(jnp.where(blk_id == c, 0.0, 1.0 - _select_blocks_t(gate_t, c)))
            flags.append(jnp.zeros((LANES - n_rows, blk), F32))
            unchosen = jnp.concatenate(flags, axis=0).T.astype(BF16)
        for h in heads:
            q = q_ref[:, h * dh:(h + 1) * dh]
            qs = (q * (dh ** -0.5)).astype(BF16)
            if gated:
                qa = jnp.concatenate([qs, unchosen], axis=1)
                s = lax.dot_general(qa, ka_ref[h, 0:n_keys, :], _NT, preferred_element_type=F32)
            else:
                s = lax.dot_general(qs, ka_ref[h, 0:n_keys, 0:dh], _NT, preferred_element_type=F32)
            parts = [s[:, 0:c * blk]] if c else []
            parts.append(jnp.where(row >= col, s[:, c * blk:n_keys], NEG_INF))
            s = jnp.concatenate(parts, axis=1)
            p = jnp.exp(s - jnp.max(s, axis=-1, keepdims=True))
            inv = 1.0 / jnp.sum(p, axis=-1, keepdims=True)
            o = jnp.dot(p.astype(BF16), vb_ref[0:n_keys, h * dh:(h + 1) * dh], preferred_element_type=F32)
            o_ref[:, h * dh:(h + 1) * dh] = o * inv

    for c in range(n_blocks):
        pl.when(qi == c)(functools.partial(step, c))


def _moba_prompt(q, k, v, bsz, t, l):
    assert t % MOBA_BLOCK == 0 and t // MOBA_BLOCK <= SUBLANES
    n_blocks = t // MOBA_BLOCK
    dh = ATT_HEAD_DIM
    key_blk = np.arange(t) // MOBA_BLOCK
    lane_of = np.arange(ATT_HEADS)[:, None, None] * SUBLANES + key_blk[None, :, None]
    kbias = jnp.asarray(np.where(lane_of == np.arange(LANES)[None, None, :], MASK_BIAS, 0.0), dtype=BF16)
    return pl.pallas_call(
        functools.partial(_moba_prompt_kernel, n_blocks=n_blocks),
        grid=(bsz, n_blocks),
        in_specs=[
            pl.BlockSpec((MOBA_BLOCK, ATT_WIDTH), lambda b, i: (b * n_blocks + i, 0)),
            pl.BlockSpec((None, t * ATT_HEADS, dh), lambda b, i: (l, b, 0)),
            pl.BlockSpec((None, t * ATT_HEADS, dh), lambda b, i: (l, b, 0)),
            pl.BlockSpec((ATT_HEADS, t, LANES), lambda b, i: (0, 0, 0)),
        ],
        out_specs=pl.BlockSpec((MOBA_BLOCK, ATT_WIDTH), lambda b, i: (b * n_blocks + i, 0)),
        out_shape=jax.ShapeDtypeStruct((bsz * t, ATT_WIDTH), F32),
        scratch_shapes=[
            pltpu.VMEM((ATT_HEADS, t, 2 * dh), BF16),
            pltpu.VMEM((t, ATT_WIDTH), BF16),
            pltpu.VMEM((ATT_HEADS * SUBLANES, ATT_WIDTH), F32),
        ],
        compiler_params=_params(2),
        name="moba_prompt",
    )(q, k, v, kbias)


def _moba_sample_kernel(pt_ref, q_ref, kn_ref, vn_ref, e_ref, *refs, n_pages, n_seq):
    del pt_ref
    k_refs = refs[:n_seq * n_pages]
    v_refs = refs[n_seq * n_pages:2 * n_seq * n_pages]
    o_ref = refs[2 * n_seq * n_pages]
    rows = q_ref.shape[1]
    pages_per_block = MOBA_BLOCK // PAGE_SIZE
    n_blocks = n_pages // pages_per_block
    dh = ATT_HEAD_DIM
    pairs = [(g, h) for g in range(n_seq) for h in range(ATT_HEADS)]

    def block_rows(page_refs, g, n, h):
        first = g * n_pages + n * pages_per_block
        return jnp.concatenate(
            [page_refs[j][0, pl.ds(h, PAGE_SIZE, stride=ATT_HEADS), :] for j in range(first, first + pages_per_block)],
            axis=0)

    qs, gate_rows, s_rows = [], [], []
    for g, h in pairs:
        q = q_ref[g, :, h * dh:(h + 1) * dh]
        qs.append((q * (dh ** -0.5)).astype(BF16))
        mean_rows, s_parts = [], []
        for n in range(n_blocks):
            kb = block_rows(k_refs, g, n, h)
            mean_rows.append(jnp.sum(kb, axis=0, keepdims=True) / MOBA_BLOCK)
            s_parts.append(lax.dot_general(qs[-1], kb.astype(BF16), _NT, preferred_element_type=F32))
        mean_rows.append(jnp.zeros((LANES - n_blocks, dh), F32))
        gate_rows.append(lax.dot_general(q, jnp.concatenate(mean_rows, axis=0), _NT,
                                         precision=lax.Precision.HIGHEST, preferred_element_type=F32))
        s_rows.append(jnp.concatenate(s_parts, axis=1))
    sel = _select_blocks(jnp.concatenate(gate_rows, axis=0), n_blocks)
    sel_keys = jnp.dot(sel.astype(BF16), e_ref[...], preferred_element_type=F32)
    s = jnp.where(sel_keys > 0.5, jnp.concatenate(s_rows, axis=0), NEG_INF)

    pad = jnp.zeros((LANES - rows, dh), F32)
    padded = lambda ref, g, h: jnp.concatenate([ref[g, :, h * dh:(h + 1) * dh], pad], axis=0).astype(BF16)
    s_own = jnp.concatenate(
        [lax.dot_general(qs[i], padded(kn_ref, g, h), _NT, preferred_element_type=F32)
         for i, (g, h) in enumerate(pairs)], axis=0)
    t_row = lax.broadcasted_iota(jnp.int32, s_own.shape, 0) % rows
    key = lax.broadcasted_iota(jnp.int32, s_own.shape, 1)
    s_own = jnp.where(key <= t_row, s_own, NEG_INF)

    m = jnp.maximum(jnp.max(s, axis=-1, keepdims=True), jnp.max(s_own, axis=-1, keepdims=True))
    p = jnp.exp(s - m)
    p_own = jnp.exp(s_own - m)
    inv = 1.0 / (jnp.sum(p, axis=-1, keepdims=True) + jnp.sum(p_own, axis=-1, keepdims=True))
    pb = p.astype(BF16)
    pb_own = p_own.astype(BF16)
    for i, (g, h) in enumerate(pairs):
        r0 = i * rows
        o = jnp.dot(pb_own[r0:r0 + rows], padded(vn_ref, g, h), preferred_element_type=F32)
        for n in range(n_blocks):
            o = o + jnp.dot(pb[r0:r0 + rows, n * MOBA_BLOCK:(n + 1) * MOBA_BLOCK],
                            block_rows(v_refs, g, n, h).astype(BF16), preferred_element_type=F32)
        o_ref[g, :, h * dh:(h + 1) * dh] = o * inv[r0:r0 + rows]


def _moba_sample(q, k_new, v_new, cache_k, cache_v, page_table, l, n_seq=2):
    bsz, rows, _ = q.shape
    n_pages = page_table.shape[1]
    n_phys = cache_k.shape[0] // DEPTH
    past_len = n_pages * PAGE_SIZE
    assert past_len % MOBA_BLOCK == 0 and past_len // MOBA_BLOCK <= SUBLANES and bsz % n_seq == 0
    base = l * n_phys
    tok_spec = pl.BlockSpec((n_seq, rows, ATT_WIDTH), lambda b, pt: (b, 0, 0))
    page_specs = [
        pl.BlockSpec((1, PAGE_SIZE * ATT_HEADS, ATT_HEAD_DIM),
                     lambda b, pt, g=g, j=j: (base + pt[(b * n_seq + g) * n_pages + j], 0, 0))
        for g in range(n_seq) for j in range(n_pages)
    ]
    grid_spec = pltpu.PrefetchScalarGridSpec(
        num_scalar_prefetch=1,
        grid=(bsz // n_seq,),
        in_specs=[tok_spec, tok_spec, tok_spec,
                  pl.BlockSpec((LANES, past_len), lambda b, pt: (0, 0))] + page_specs + page_specs,
        out_specs=tok_spec,
    )
    return pl.pallas_call(
        functools.partial(_moba_sample_kernel, n_pages=n_pages, n_seq=n_seq),
        grid_spec=grid_spec,
        out_shape=jax.ShapeDtypeStruct((bsz, rows, ATT_WIDTH), F32),
        compiler_params=_params(1),
        name="moba_sample",
    )(page_table.reshape(-1), q, k_new, v_new, _block_expander(past_len),
      *([cache_k] * (n_seq * n_pages)), *([cache_v] * (n_seq * n_pages)))


def _pool_windows_to_lanes(sums, counts, cur):
    lane = lax.broadcasted_iota(jnp.int32, cur.shape, 1)
    d = sums[-1] / counts[-1]
    for g in range(len(POOL_WINDOWS) - 2, -1, -1):
        d = jnp.where(lane < (g + 1) * POOL_GROUP_WIDTH, sums[g] / counts[g], d)
    return d - cur


def _conv_tail(y, ln_g, ln_b, pw_w, pw_b):
    yc = y - jnp.mean(y, axis=-1, keepdims=True)
    yn = yc * lax.rsqrt(jnp.mean(yc * yc, axis=-1, keepdims=True) + EPS) * ln_g + ln_b
    act = yn * jax.nn.sigmoid(yn)
    return jnp.dot(act.astype(BF16), pw_w, preferred_element_type=F32) + pw_b


POOL_HALO = 16
CONV_HALO = 32


def _poolconv_prompt_kernel(x_ref, u_ref, ag_ref, ya_ref, pw_ref, ps_ref, dw_ref, db_ref, lg_ref, lb_ref,
                            cw_ref, cb_ref, wp_ref, wc_ref, wa_ref, o_ref, ht_ref, e_ref, h_ref, hs_ref, *, tt):
    ti = pl.program_id(1)

    @pl.when(ti == 0)
    def _():
        e_ref[0:POOL_HALO, :] = jnp.zeros((POOL_HALO, POOL_WIDTH), F32)
        h_ref[0:CONV_HALO, :] = jnp.zeros((CONV_HALO, CONV_WIDTH), F32)

    u = u_ref[...]
    e_ref[POOL_HALO:POOL_HALO + tt, :] = u
    a = ag_ref[:, 0:CONV_WIDTH]
    gate = ag_ref[:, CONV_WIDTH:2 * CONV_WIDTH]
    h_ref[CONV_HALO:CONV_HALO + tt, :] = a * jax.nn.sigmoid(gate)

    pos = ti * tt + lax.broadcasted_iota(jnp.int32, (tt, 1), 0)
    sums, counts = [], []
    run = u
    taken = 1
    for win in POOL_WINDOWS:
        while taken < win:
            run = run + e_ref[POOL_HALO - taken:POOL_HALO - taken + tt, :]
            taken += 1
        sums.append(run)
        counts.append(jnp.minimum(pos + 1, win).astype(F32))
    d = _pool_windows_to_lanes(sums, counts, u)
    y_pool = jnp.dot(d.astype(BF16), pw_ref[...], preferred_element_type=F32) * ps_ref[...]

    span = tt + CONV_HALO - SUBLANES
    for s in range(1, SUBLANES):
        hs_ref[s - 1, 0:span, :] = h_ref[s:s + span, :]
    y = jnp.zeros((tt, CONV_WIDTH), F32) + db_ref[...]
    for j in range(CONV_K):
        r0 = CONV_HALO - CONV_BUF + j
        s = r0 % SUBLANES
        rows = h_ref[r0:r0 + tt, :] if s == 0 else hs_ref[s - 1, r0 - s:r0 - s + tt, :]
        y = y + rows * dw_ref[j:j + 1, :]
    y_conv = _conv_tail(y, lg_ref[...], lb_ref[...], cw_ref[...], cb_ref[...])

    out = x_ref[...] + jnp.dot(y_pool.astype(BF16), wp_ref[...], preferred_element_type=F32)
    out = out + jnp.dot(y_conv.astype(BF16), wc_ref[...], preferred_element_type=F32)
    o_ref[...] = out + jnp.dot(ya_ref[...].astype(BF16), wa_ref[...], preferred_element_type=F32)

    ht_ref[0] = h_ref[tt:tt + CONV_HALO, :]
    e_ref[0:POOL_HALO, :] = e_ref[tt:tt + POOL_HALO, :]
    h_ref[0:CONV_HALO, :] = h_ref[tt:tt + CONV_HALO, :]


def _mix_out_prompt(x, u, ag, y_att, bsz, t, weights, w_out, l, tt=512):
    pool_w, pool_s, dw_w, dw_b, ln_g, ln_b, pw_w, pw_b = weights
    nt = t // tt
    row = lambda b, i: (b * nt + i, 0)
    w_rows = lambda width, blk: pl.BlockSpec((None, width, D_MODEL), lambda b, i: (l, blk, 0))
    return pl.pallas_call(
        functools.partial(_poolconv_prompt_kernel, tt=tt),
        grid=(bsz, nt),
        in_specs=[
            pl.BlockSpec((tt, D_MODEL), row),
            pl.BlockSpec((tt, POOL_WIDTH), row),
            pl.BlockSpec((tt, 2 * CONV_WIDTH), row),
            pl.BlockSpec((tt, ATT_WIDTH), row),
            _layer_spec((POOL_WIDTH, POOL_WIDTH), l, 2),
            _layer_spec((1, POOL_WIDTH), l, 2),
            _layer_spec((CONV_HALO, CONV_WIDTH), l, 2),
            _layer_spec((1, CONV_WIDTH), l, 2),
            _layer_spec((1, CONV_WIDTH), l, 2),
            _layer_spec((1, CONV_WIDTH), l, 2),
            _layer_spec((CONV_WIDTH, CONV_WIDTH), l, 2),
            _layer_spec((1, CONV_WIDTH), l, 2),
            w_rows(POOL_WIDTH, 0),
            w_rows(CONV_WIDTH, POOL_WIDTH // CONV_WIDTH),
            w_rows(ATT_WIDTH, (POOL_WIDTH + CONV_WIDTH) // ATT_WIDTH),
        ],
        out_specs=[
            pl.BlockSpec((tt, D_MODEL), row),
            pl.BlockSpec((1, CONV_HALO, CONV_WIDTH), lambda b, i: (b, 0, 0)),
        ],
        out_shape=[
            jax.ShapeDtypeStruct((bsz * t, D_MODEL), F32),
            jax.ShapeDtypeStruct((bsz, CONV_HALO, CONV_WIDTH), F32),
        ],
        scratch_shapes=[
            pltpu.VMEM((POOL_HALO + tt, POOL_WIDTH), F32),
            pltpu.VMEM((CONV_HALO + tt, CONV_WIDTH), F32),
            pltpu.VMEM((SUBLANES - 1, CONV_HALO + tt, CONV_WIDTH), F32),
        ],
        compiler_params=_params(2),
        name="mix_out_prompt",
    )(x, u, ag, y_att, pool_w, pool_s, dw_w, dw_b, ln_g, ln_b, pw_w, pw_b, w_out, w_out, w_out)


def _poolconv_sample_kernel(sp_ref, u_ref, sc_ref, ag_ref, pw_ref, ps_ref, dw_ref, db_ref, lg_ref, lb_ref,
                            cw_ref, cb_ref, yp_ref, yc_ref, h_ref, *, n_new, pos0):
    def pool_row(r):
        return sp_ref[r] if r < POOL_BUF else u_ref[r - POOL_BUF]

    for t in range(n_new):
        h_ref[t] = ag_ref[t, :, 0:CONV_WIDTH] * jax.nn.sigmoid(ag_ref[t, :, CONV_WIDTH:2 * CONV_WIDTH])

    def conv_row(r):
        return sc_ref[r] if r < CONV_BUF else h_ref[r - CONV_BUF]

    for t in range(n_new):
        cur = u_ref[t]
        sums, counts = [], []
        run = cur
        taken = 1
        for win in POOL_WINDOWS:
            while taken < win:
                run = run + pool_row(POOL_BUF + t - taken)
                taken += 1
            sums.append(run)
            counts.append(float(min(pos0 + t + 1, win)))
        d = _pool_windows_to_lanes(sums, counts, cur)
        yp_ref[t] = jnp.dot(d.astype(BF16), pw_ref[...], preferred_element_type=F32) * ps_ref[...]

        y = jnp.zeros(cur.shape, F32) + db_ref[...]
        for j in range(CONV_K):
            y = y + conv_row(t + j) * dw_ref[j:j + 1, :]
        yc_ref[t] = _conv_tail(y, lg_ref[...], lb_ref[...], cw_ref[...], cb_ref[...])


def _poolconv_sample(state_pool_t, u_t, state_conv_t, ag_t, pos0, weights, l):
    pool_w, pool_s, dw_w, dw_b, ln_g, ln_b, pw_w, pw_b = weights
    n_new, bsz, _ = u_t.shape
    full = lambda a: pl.BlockSpec(a.shape, lambda i: (0,) * a.ndim)
    out = jax.ShapeDtypeStruct((n_new, bsz, CONV_WIDTH), F32)
    return pl.pallas_call(
        functools.partial(_poolconv_sample_kernel, n_new=n_new, pos0=pos0),
        grid=(1,),
        in_specs=[
            full(state_pool_t), full(u_t), full(state_conv_t), full(ag_t),
            _layer_spec((POOL_WIDTH, POOL_WIDTH), l, 1),
            _layer_spec((1, POOL_WIDTH), l, 1),
            _layer_spec((CONV_HALO, CONV_WIDTH), l, 1),
            _layer_spec((1, CONV_WIDTH), l, 1),
            _layer_spec((1, CONV_WIDTH), l, 1),
            _layer_spec((1, CONV_WIDTH), l, 1),
            _layer_spec((CONV_WIDTH, CONV_WIDTH), l, 1),
            _layer_spec((1, CONV_WIDTH), l, 1),
        ],
        out_specs=[pl.BlockSpec(out.shape, lambda i: (0, 0, 0))] * 3,
        out_shape=[out, out, out],
        compiler_params=_params(1),
        name="poolconv_sample",
    )(state_pool_t, u_t, state_conv_t, ag_t, pool_w, pool_s, dw_w, dw_b, ln_g, ln_b, pw_w, pw_b)


def _cross_kernel(x_ref, g_ref, wq_ref, hg_ref, k_ref, v_ref, wo_ref, o_ref, *, n_seq, interleaved):
    hd = MEM_HEAD_DIM
    rows = x_ref.shape[1]
    x = x_ref[...].reshape(n_seq * rows, D_MODEL)
    xn = _rms(x, g_ref[...]).astype(BF16)
    q_all = jnp.dot(xn, wq_ref[...], preferred_element_type=F32)
    halves = range(hd // LANES)

    def head_slab(ref, g, h):
        if not interleaved:
            return ref[g, :, h * hd:(h + 1) * hd]
        return jnp.concatenate(
            [ref[g, pl.ds(half * MEM_HEADS + h, N_MEM, stride=len(halves) * MEM_HEADS), :] for half in halves],
            axis=1)

    pairs = [(g, h) for g in range(n_seq) for h in range(MEM_HEADS)]
    scores = []
    for g, h in pairs:
        q = _rms(q_all[g * rows:(g + 1) * rows, h * hd:(h + 1) * hd], hg_ref[...])
        q = (q * (hd ** -0.5)).astype(BF16)
        scores.append(lax.dot_general(q, head_slab(k_ref, g, h).astype(BF16), _NT, preferred_element_type=F32))
    s = jnp.concatenate(scores, axis=0)
    p = jnp.exp(s - jnp.max(s, axis=-1, keepdims=True))
    inv = 1.0 / jnp.sum(p, axis=-1, keepdims=True)
    o_seqs = []
    for g in range(n_seq):
        o_heads = []
        for h in range(MEM_HEADS):
            r0 = (g * MEM_HEADS + h) * rows
            o = jnp.dot(p[r0:r0 + rows].astype(BF16), head_slab(v_ref, g, h).astype(BF16),
                        preferred_element_type=F32)
            o_heads.append((o * inv[r0:r0 + rows]).astype(BF16))
        o_seqs.append(jnp.concatenate(o_heads, axis=1))
    o_all = o_seqs[0] if n_seq == 1 else jnp.concatenate(o_seqs, axis=0)
    y = x + jnp.dot(o_all, wo_ref[...], preferred_element_type=F32)
    o_ref[...] = y.reshape(n_seq, rows, D_MODEL)


def _interleaved_memory(cache):
    depth, s, n_mem, heads, hd = cache.shape
    halves = hd // LANES
    c = cache.reshape(depth, s, n_mem, heads, halves, LANES).transpose(0, 1, 2, 4, 3, 5)
    return c.reshape(depth * s, n_mem * halves * heads, LANES)


def _cross(x, ln, w_q, head_gain, mem_k, mem_v, w_o, l, n_seq, tq, seq0=None):
    s_total, r, _ = x.shape
    interleaved = seq0 is not None
    if interleaved:
        assert seq0 % n_seq == 0
        kv_spec = pl.BlockSpec((n_seq,) + mem_k.shape[1:], lambda s, i: (seq0 // n_seq + s, 0, 0))
    else:
        kv_spec = pl.BlockSpec((n_seq, N_MEM, D_MODEL), lambda s, i: (s, 0, 0))
    x_spec = pl.BlockSpec((n_seq, tq, D_MODEL), lambda s, i: (s, i, 0))
    return pl.pallas_call(
        functools.partial(_cross_kernel, n_seq=n_seq, interleaved=interleaved),
        grid=(s_total // n_seq, r // tq),
        in_specs=[x_spec, _layer_spec((1, D_MODEL), l, 2), _layer_spec((D_MODEL, D_MODEL), l, 2),
                  _layer_spec((1, MEM_HEAD_DIM), l, 2), kv_spec, kv_spec,
                  _layer_spec((D_MODEL, D_MODEL), l, 2)],
        out_specs=x_spec,
        out_shape=jax.ShapeDtypeStruct(x.shape, F32),
        compiler_params=_params(2),
        name="cross",
    )(x, ln, w_q, head_gain, mem_k, mem_v, w_o)


def _pad_rows(x, rows):
    return jnp.pad(x, ((0, 0), (0, rows - x.shape[1]), (0, 0)))


def kernel(x_prompt, x_sample, cache_att_k, cache_att_v, cache_mem_k, cache_mem_v, state_pool, state_conv, page_table, mem_prompt, ln_ffn1, w_ffn1_in, w_ffn1_out, ln_mix, w_in, pool_w, pool_scale, conv_dw_w, conv_dw_b, conv_ln_g, conv_ln_b, conv_pw_w, conv_pw_b, att_q_norm, att_k_norm, w_out, ln_cross, ln_mem, w_cq, w_ckv, cq_norm, ck_norm, w_co, ln_ffn2, w_ffn2_in, w_ffn2_out):
    bsz, seq, _ = x_prompt.shape
    dbsz, dseq, _ = x_sample.shape
    n_pages = page_table.shape[1]
    past_len = n_pages * PAGE_SIZE
    n_phys = cache_att_k.shape[1]

    bf = lambda w: w.astype(BF16)
    rowv = lambda v: v[:, None, :]
    w1i, w1o, w2i, w2o = bf(w_ffn1_in), bf(w_ffn1_out), bf(w_ffn2_in), bf(w_ffn2_out)
    w_in_b, w_out_b, w_cq_b, w_ckv_b, w_co_b = bf(w_in), bf(w_out), bf(w_cq), bf(w_ckv), bf(w_co)
    groups = len(POOL_WINDOWS)
    eye = jnp.eye(groups, dtype=F32)
    pool_bd = bf((pool_w[:, :, :, None, :] * eye[None, :, None, :, None])
                 .reshape(DEPTH, POOL_WIDTH, POOL_WIDTH))
    dw_pad = jnp.pad(conv_dw_w, ((0, 0), (0, CONV_HALO - CONV_K), (0, 0)))
    mixer_w = (pool_bd, rowv(pool_scale), dw_pad, rowv(conv_dw_b), rowv(conv_ln_g), rowv(conv_ln_b),
               bf(conv_pw_w), rowv(conv_pw_b))
    ln1, lnm, lnc, lnmem, ln2 = rowv(ln_ffn1), rowv(ln_mix), rowv(ln_cross), rowv(ln_mem), rowv(ln_ffn2)

    tables_p = _rope_tables(jnp.arange(seq, dtype=jnp.int32))
    tm_s = dbsz * dseq
    tables_s = _rope_tables(past_len + jnp.arange(tm_s, dtype=jnp.int32) % dseq)

    c1 = POOL_WIDTH
    c3 = c1 + 2 * CONV_WIDTH
    c4 = c3 + ATT_WIDTH
    c5 = c4 + ATT_WIDTH
    mix_segs = [(c3, ATT_WIDTH, ATT_HEAD_DIM, 0, True, False), (c4, ATT_WIDTH, ATT_HEAD_DIM, 1, True, True),
                (c5, ATT_WIDTH, ATT_HEAD_DIM, None, False, True),
                (0, c1, None, None, False, False), (c1, 2 * CONV_WIDTH, None, None, False, False)]
    mem_segs = [(0, D_MODEL, MEM_HEAD_DIM, 0, False, True), (D_MODEL, D_MODEL, MEM_HEAD_DIM, None, False, True)]
    out_rows = (0, c1, c1 + CONV_WIDTH)

    cache_k = cache_att_k.reshape(DEPTH * n_phys, PAGE_SIZE * ATT_HEADS, ATT_HEAD_DIM)
    cache_v = cache_att_v.reshape(DEPTH * n_phys, PAGE_SIZE * ATT_HEADS, ATT_HEAD_DIM)
    mem_k_s = _interleaved_memory(cache_mem_k)
    mem_v_s = _interleaved_memory(cache_mem_v)
    mem_rows = mem_prompt.reshape(bsz * N_MEM, D_MODEL)

    xp = x_prompt.reshape(bsz * seq, D_MODEL)
    xs = x_sample.reshape(dbsz * dseq, D_MODEL)
    rows_pad = SUBLANES
    outs = {name: [] for name in ("p_pool", "p_conv", "s_pool", "s_conv")}
    cq_gain = rowv(cq_norm)
    kv_p = kv_s = mem_p = None

    for l in range(DEPTH):
        att_gains = jnp.stack([att_q_norm[l], att_k_norm[l]])

        mem_p = _norm_proj(mem_rows, lnmem, w_ckv_b, l, ck_norm[l][None], mem_segs, stacked=mem_p)
        mk, mv = (a.reshape(DEPTH * bsz, -1, LANES) for a in mem_p)
        xp = _ffn(xp, ln1, w1i, w1o, l)
        q, k, v, u, ag = _norm_proj(xp, lnm, w_in_b, l, att_gains, mix_segs, tables=tables_p, stacked=kv_p)
        kv_p = (k, v)
        y_att = _moba_prompt(q, k, v, bsz, seq, l)
        xp, h_tail = _mix_out_prompt(xp, u, ag, y_att, bsz, seq, mixer_w, w_out_b, l)
        xp = _cross(xp.reshape(bsz, seq, D_MODEL), lnc, w_cq_b, cq_gain, mk, mv, w_co_b, l, 1, 512,
                    seq0=l * bsz).reshape(bsz * seq, D_MODEL)
        xp = _ffn(xp, ln2, w2i, w2o, l)
        outs["p_pool"].append(u.reshape(bsz, seq, POOL_WIDTH)[:, seq - POOL_BUF:])
        outs["p_conv"].append(h_tail[:, CONV_HALO - CONV_BUF:])

        xs = _ffn(xs, ln1, w1i, w1o, l, tm=tm_s)
        q, k, v, u, ag = _norm_proj(xs, lnm, w_in_b, l, att_gains, mix_segs, tables=tables_s, tm=tm_s,
                                    stacked=kv_s)
        kv_s = (k, v)
        pad3 = lambda a: _pad_rows(a.reshape(dbsz, dseq, ATT_WIDTH), 2 * SUBLANES)
        y_att = _moba_sample(pad3(q), pad3(k[l]), pad3(v[l]), cache_k, cache_v, page_table, l)
        y_att = y_att[:, :dseq].reshape(dbsz * dseq, ATT_WIDTH)
        tmaj = lambda a: jnp.swapaxes(a, 0, 1)
        u3 = u.reshape(dbsz, dseq, POOL_WIDTH)
        y_pool, y_conv, h_new = _poolconv_sample(
            tmaj(state_pool[l]), tmaj(u3), tmaj(state_conv[l]), tmaj(ag.reshape(dbsz, dseq, 2 * CONV_WIDTH)),
            past_len, mixer_w, l)
        bmaj = lambda a: jnp.swapaxes(a, 0, 1).reshape(dbsz * dseq, a.shape[-1])
        xs = _out_proj(xs, [bmaj(y_pool), bmaj(y_conv), y_att], w_out_b, l, out_rows, tm=tm_s)
        n_seq = 4
        xs = _cross(_pad_rows(xs.reshape(dbsz, dseq, D_MODEL), rows_pad), lnc, w_cq_b, cq_gain, mem_k_s, mem_v_s,
                    w_co_b, l, n_seq, rows_pad, seq0=l * dbsz)[:, :dseq].reshape(dbsz * dseq, D_MODEL)
        xs = _ffn(xs, ln2, w2i, w2o, l, tm=tm_s)
        outs["s_pool"].append(jnp.concatenate([state_pool[l][:, dseq:], u3], axis=1))
        outs["s_conv"].append(jnp.concatenate([state_conv[l][:, dseq:], jnp.swapaxes(h_new, 0, 1)], axis=1))

    st = lambda name: jnp.stack(outs[name])
    heads5 = lambda a, b, t: a.reshape(DEPTH, b, t, ATT_HEADS, ATT_HEAD_DIM)
    tiles = MEM_HEAD_DIM // LANES
    mem5 = lambda a: (a.reshape(DEPTH, bsz, N_MEM, tiles, MEM_HEADS, LANES).transpose(0, 1, 2, 4, 3, 5)
                      .reshape(DEPTH, bsz, N_MEM, MEM_HEADS, MEM_HEAD_DIM))
    return (xp.reshape(bsz, seq, D_MODEL), xs.reshape(dbsz, dseq, D_MODEL),
            heads5(kv_p[0], bsz, seq), heads5(kv_p[1], bsz, seq), mem5(mem_p[0]), mem5(mem_p[1]), st("p_pool"),
            st("p_conv"), heads5(kv_s[0], dbsz, dseq), heads5(kv_s[1], dbsz, dseq), st("s_pool"), st("s_conv"))
```

```python
import functools

import jax
import jax.numpy as jnp
import numpy as np
from jax import lax
from jax.experimental import pallas as pl
from jax.experimental.pallas import tpu as pltpu

D_MODEL = 1024
DEPTH = 2
PAGE_SIZE = 128

POOL_WIDTH = D_MODEL // 4
POOL_WINDOWS = (2, 4, 8, 16)
POOL_GROUP_WIDTH = POOL_WIDTH // len(POOL_WINDOWS)
POOL_BUF = max(POOL_WINDOWS) - 1

CONV_WIDTH = D_MODEL // 4
CONV_K = 31
CONV_BUF = CONV_K - 1

ATT_HEADS = 4
ATT_HEAD_DIM = D_MODEL // 8
ATT_WIDTH = ATT_HEADS * ATT_HEAD_DIM
ROT_DIM = ATT_HEAD_DIM // 4
ROPE_THETA = 500000.0
MOBA_BLOCK = 256
MOBA_TOPK = 3

IN_WIDTH = POOL_WIDTH + 2 * CONV_WIDTH + 3 * ATT_WIDTH

N_MEM = 256
MEM_HEADS = 4
MEM_HEAD_DIM = D_MODEL // MEM_HEADS

D_FF = ((8 * D_MODEL // 3 + 127) // 128) * 128
EPS = 1e-6

LANES = 128
SUBLANES = 8
VMEM_LIMIT_BYTES = 56 * 1024 * 1024

F32 = jnp.float32
BF16 = jnp.bfloat16
NEG_INF = float("-inf")

_NT = (((1,), (1,)), ((), ()))


def _params(n_axes):
    return pltpu.CompilerParams(dimension_semantics=("arbitrary",) * n_axes,
                                vmem_limit_bytes=VMEM_LIMIT_BYTES)


def _rms(x, g):
    return x * lax.rsqrt(jnp.mean(x * x, axis=-1, keepdims=True) + EPS) * g


def _layer_spec(shape, l, n_grid):
    zeros = (0,) * len(shape)
    if n_grid == 1:
        return pl.BlockSpec((None,) + tuple(shape), lambda i: (l,) + zeros)
    if n_grid == 2:
        return pl.BlockSpec((None,) + tuple(shape), lambda i, j: (l,) + zeros)
    return pl.BlockSpec((None,) + tuple(shape), lambda i, j, k: (l,) + zeros)


FFN_CHUNK = 256


def _ffn_kernel(x_ref, g_ref, wi_ref, wo_ref, o_ref):
    x = x_ref[...]
    xn = _rms(x, g_ref[...]).astype(BF16)
    acc = None
    for c in range(D_FF // FFN_CHUNK):
        lo = c * FFN_CHUNK
        gate = jnp.dot(xn, wi_ref[:, lo:lo + FFN_CHUNK], preferred_element_type=F32)
        up = jnp.dot(xn, wi_ref[:, D_FF + lo:D_FF + lo + FFN_CHUNK], preferred_element_type=F32)
        act = (gate * jax.nn.sigmoid(gate) * up).astype(BF16)
        part = jnp.dot(act, wo_ref[lo:lo + FFN_CHUNK, :], preferred_element_type=F32)
        acc = part if acc is None else acc + part
    o_ref[...] = x + 0.5 * acc


def _ffn(x, ln, w_in, w_out, l, tm=512):
    m = x.shape[0]
    return pl.pallas_call(
        _ffn_kernel,
        grid=(m // tm,),
        in_specs=[
            pl.BlockSpec((tm, D_MODEL), lambda i: (i, 0)),
            _layer_spec((1, D_MODEL), l, 1),
            _layer_spec((D_MODEL, 2 * D_FF), l, 1),
            _layer_spec((D_FF, D_MODEL), l, 1),
        ],
        out_specs=pl.BlockSpec((tm, D_MODEL), lambda i: (i, 0)),
        out_shape=jax.ShapeDtypeStruct((m, D_MODEL), F32),
        compiler_params=_params(1),
        name="ffn",
    )(x, ln, w_in, w_out)


def _norm_proj_kernel(*refs, segs, rope):
    x_ref, g_ref, w_ref, hg_ref = refs[:4]
    if rope:
        cos_ref, sin_lo_ref, sin_hi_ref = refs[4:7]
    out_refs = refs[len(refs) - len(segs):]
    tm = x_ref.shape[0]
    xn = _rms(x_ref[...], g_ref[...]).astype(BF16)
    for (c0, width, head_dim, gain_row, use_rope, head_rows), o_ref in zip(segs, out_refs):
        z = jnp.dot(xn, w_ref[:, c0:c0 + width], preferred_element_type=F32)
        if head_dim is None:
            o_ref[...] = z
            continue
        n_heads = width // head_dim
        for h in range(n_heads):
            zh = z[:, h * head_dim:(h + 1) * head_dim]
            if gain_row is not None:
                zh = _rms(zh, hg_ref[gain_row:gain_row + 1, :])
            if use_rope:
                half = ROT_DIM // 2
                zh = (zh * cos_ref[...]
                      + pltpu.roll(zh, half, 1) * sin_hi_ref[...]
                      + pltpu.roll(zh, LANES - half, 1) * sin_lo_ref[...])
            if not head_rows:
                o_ref[:, h * head_dim:(h + 1) * head_dim] = zh
                continue
            tiles = head_dim // LANES
            for tile in range(tiles):
                rows = pl.ds(tile * n_heads + h, tm, stride=n_heads * tiles)
                piece = zh[:, tile * LANES:(tile + 1) * LANES]
                if len(o_ref.shape) == 2:
                    o_ref[rows, :] = piece
                else:
                    for d in range(o_ref.shape[0]):
                        o_ref[d, rows, :] = piece


def _norm_proj(x, ln, w, l, head_gains, segs, tables=None, tm=512, stacked=None):
    m = x.shape[0]
    n_cols = w.shape[-1]
    rope = tables is not None
    in_specs = [
        pl.BlockSpec((tm, D_MODEL), lambda i: (i, 0)),
        _layer_spec((1, D_MODEL), l, 1),
        _layer_spec((D_MODEL, n_cols), l, 1),
        pl.BlockSpec(head_gains.shape, lambda i: (0, 0)),
    ]
    args = [x, ln, w, head_gains]
    if rope:
        period = tables[0].shape[0] // tm
        for t in tables:
            in_specs.append(pl.BlockSpec((tm, LANES), lambda i: (i % period, 0)))
            args.append(t)
    out_specs, out_shape, aliases = [], [], {}
    for k, s in enumerate(segs):
        width, head_dim, head_rows = s[1], s[2], s[5]
        if not head_rows:
            out_specs.append(pl.BlockSpec((tm, width), lambda i: (i, 0)))
            out_shape.append(jax.ShapeDtypeStruct((m, width), F32))
            continue
        per_token = width // LANES
        out_shape.append(jax.ShapeDtypeStruct((DEPTH, m * per_token, LANES), F32))
        if stacked is None:
            out_specs.append(pl.BlockSpec((DEPTH, tm * per_token, LANES), lambda i: (0, i, 0)))
        else:
            out_specs.append(pl.BlockSpec((None, tm * per_token, LANES), lambda i: (l, i, 0)))
            aliases[len(args)] = k
            in_specs.append(pl.BlockSpec(memory_space=pl.ANY))
            args.append(stacked[len(aliases) - 1])
    return pl.pallas_call(
        functools.partial(_norm_proj_kernel, segs=tuple(segs), rope=rope),
        grid=(m // tm,),
        in_specs=in_specs,
        out_specs=out_specs,
        out_shape=out_shape,
        input_output_aliases=aliases,
        compiler_params=_params(1),
        name="norm_proj",
    )(*args)


def _rope_tables(positions):
    half = ROT_DIM // 2
    inv_freq = jnp.power(jnp.float32(ROPE_THETA), -jnp.arange(half, dtype=F32) / half)
    ang = positions.astype(F32)[:, None] * inv_freq[None, :]
    cos, sin = jnp.cos(ang), jnp.sin(ang)
    n = positions.shape[0]
    rest = ATT_HEAD_DIM - ROT_DIM
    cos_t = jnp.concatenate([cos, cos, jnp.ones((n, rest), F32)], axis=1)
    sin_lo = jnp.concatenate([-sin, jnp.zeros((n, half + rest), F32)], axis=1)
    sin_hi = jnp.concatenate([jnp.zeros((n, half), F32), sin, jnp.zeros((n, rest), F32)], axis=1)
    return cos_t, sin_lo, sin_hi


def _out_proj_kernel(*refs, n_terms):
    x_ref = refs[0]
    y_refs = refs[1:1 + n_terms]
    w_refs = refs[1 + n_terms:1 + 2 * n_terms]
    o_ref = refs[1 + 2 * n_terms]
    acc = x_ref[...]
    for y_ref, w_ref in zip(y_refs, w_refs):
        acc = acc + jnp.dot(y_ref[...].astype(BF16), w_ref[...], preferred_element_type=F32)
    o_ref[...] = acc


def _out_proj(x, ys, w, l, row_starts, tm=512):
    m = x.shape[0]
    in_specs = [pl.BlockSpec((tm, D_MODEL), lambda i: (i, 0))]
    for y in ys:
        in_specs.append(pl.BlockSpec((tm, y.shape[1]), lambda i: (i, 0)))
    for y, r0 in zip(ys, row_starts):
        width = y.shape[1]
        blk = r0 // width
        in_specs.append(pl.BlockSpec((None, width, D_MODEL), lambda i, blk=blk: (l, blk, 0)))
    return pl.pallas_call(
        functools.partial(_out_proj_kernel, n_terms=len(ys)),
        grid=(m // tm,),
        in_specs=in_specs,
        out_specs=pl.BlockSpec((tm, D_MODEL), lambda i: (i, 0)),
        out_shape=jax.ShapeDtypeStruct((m, D_MODEL), F32),
        compiler_params=_params(1),
        name="out_proj",
    )(x, *ys, *([w] * len(ys)))


def _select_blocks(gate, n_allowed):
    lane = lax.broadcasted_iota(jnp.int32, gate.shape, 1)
    allowed = lane < n_allowed
    g = jnp.where(allowed, gate, NEG_INF)
    rank = jnp.zeros(gate.shape, F32)
    for r in range(1, SUBLANES):
        lower = pltpu.roll(g, r, 1)
        higher = pltpu.roll(g, LANES - r, 1)
        rank = rank + jnp.where(lower >= g, 1.0, 0.0) + jnp.where(higher > g, 1.0, 0.0)
    return jnp.where(allowed, jnp.where(rank < MOBA_TOPK, 1.0, 0.0), 0.0)


def _block_expander(n_keys):
    blk = np.arange(n_keys) // MOBA_BLOCK
    return jnp.asarray((np.arange(LANES)[:, None] == blk[None, :]).astype(np.float32), dtype=BF16)


MASK_BIAS = -1e30


def _select_blocks_t(gate_t, n_allowed):
    blk = lax.broadcasted_iota(jnp.int32, gate_t.shape, 0)
    allowed = blk < n_allowed
    g = jnp.where(allowed, gate_t, NEG_INF)
    rank = jnp.zeros(gate_t.shape, F32)
    for m in range(SUBLANES):
        gm = g[m:m + 1, :]
        tie = jnp.where(gm == g, jnp.where(blk > m, 1.0, 0.0), 0.0)
        rank = rank + jnp.where(gm > g, 1.0, tie)
    return jnp.where(allowed, jnp.where(rank < MOBA_TOPK, 1.0, 0.0), 0.0)


def _moba_prompt_kernel(q_ref, k_ref, v_ref, kbias_ref, o_ref, ka_ref, vb_ref, km_ref, *, n_blocks):
    qi = pl.program_id(1)
    blk = MOBA_BLOCK
    dh = ATT_HEAD_DIM
    heads = range(ATT_HEADS)

    @pl.when(qi == 0)
    def _():
        km_ref[...] = jnp.zeros(km_ref.shape, F32)
        for h in heads:
            vb_ref[:, h * dh:(h + 1) * dh] = v_ref[pl.ds(h, n_blocks * blk, stride=ATT_HEADS), :].astype(BF16)
            ka_ref[h, :, dh:2 * dh] = kbias_ref[h]
            for n in range(n_blocks):
                kb = k_ref[pl.ds(n * blk * ATT_HEADS + h, blk, stride=ATT_HEADS), :]
                ka_ref[h, n * blk:(n + 1) * blk, 0:dh] = kb.astype(BF16)
                r = h * SUBLANES + n
                km_ref[r:r + 1, h * dh:(h + 1) * dh] = jnp.mean(kb, axis=0, keepdims=True)

    row = lax.broadcasted_iota(jnp.int32, (blk, blk), 0)
    col = lax.broadcasted_iota(jnp.int32, (blk, blk), 1)

    def step(c):
        n_keys = (c + 1) * blk
        gated = c > MOBA_TOPK
        if gated:
            qf = q_ref[...]
            q_hi = qf.astype(BF16)
            q_lo = (qf - q_hi.astype(F32)).astype(BF16)
            km = km_ref[...]
            km_hi = km.astype(BF16)
            km_lo = (km - km_hi.astype(F32)).astype(BF16)
            g_hi = lax.dot_general(jnp.concatenate([km_hi, km_lo], axis=0), q_hi, _NT, preferred_element_type=F32)
            g_lo = lax.dot_general(km_hi, q_lo, _NT, preferred_element_type=F32)
            n_rows = ATT_HEADS * SUBLANES
            gate_all = g_hi[0:n_rows] + g_hi[n_rows:2 * n_rows] + g_lo
            flags = []
            for h in heads:
                gate_t = gate_all[h * SUBLANES:(h + 1) * SUBLANES]
                blk_id = lax.broadcasted_iota(jnp.int32, gate_t.shape, 0)
                flags.append(jnp.where(blk_id == c, 0.0, 1.0 - _select_blocks_t(gate_t, c)))
            flags.append(jnp.zeros((LANES - n_rows, blk), F32))
            unchosen = jnp.concatenate(flags, axis=0).T.astype(BF16)
        for h in heads:
            q = q_ref[:, h * dh:(h + 1) * dh]
            qs = (q * (dh ** -0.5)).astype(BF16)
            if gated:
                qa = jnp.concatenate([qs, unchosen], axis=1)
                s = lax.dot_general(qa, ka_ref[h, 0:n_keys, :], _NT, preferred_element_type=F32)
            else:
                s = lax.dot_general(qs, ka_ref[h, 0:n_keys, 0:dh], _NT, preferred_element_type=F32)
            parts = [s[:, 0:c * blk]] if c else []
            parts.append(jnp.where(row >= col, s[:, c * blk:n_keys], NEG_INF))
            s = jnp.concatenate(parts, axis=1)
            p = jnp.exp(s - jnp.max(s, axis=-1, keepdims=True))
            inv = 1.0 / jnp.sum(p, axis=-1, keepdims=True)
            o = jnp.dot(p.astype(BF16), vb_ref[0:n_keys, h * dh:(h + 1) * dh], preferred_element_type=F32)
            o_ref[:, h * dh:(h + 1) * dh] = o * inv

    for c in range(n_blocks):
        pl.when(qi == c)(functools.partial(step, c))


def _moba_prompt(q, k, v, bsz, t, l):
    assert t % MOBA_BLOCK == 0 and t // MOBA_BLOCK <= SUBLANES
    n_blocks = t // MOBA_BLOCK
    dh = ATT_HEAD_DIM
    key_blk = np.arange(t) // MOBA_BLOCK
    lane_of = np.arange(ATT_HEADS)[:, None, None] * SUBLANES + key_blk[None, :, None]
    kbias = jnp.asarray(np.where(lane_of == np.arange(LANES)[None, None, :], MASK_BIAS, 0.0), dtype=BF16)
    return pl.pallas_call(
        functools.partial(_moba_prompt_kernel, n_blocks=n_blocks),
        grid=(bsz, n_blocks),
        in_specs=[
            pl.BlockSpec((MOBA_BLOCK, ATT_WIDTH), lambda b, i: (b * n_blocks + i, 0)),
            pl.BlockSpec((None, t * ATT_HEADS, dh), lambda b, i: (l, b, 0)),
            pl.BlockSpec((None, t * ATT_HEADS, dh), lambda b, i: (l, b, 0)),
            pl.BlockSpec((ATT_HEADS, t, LANES), lambda b, i: (0, 0, 0)),
        ],
        out_specs=pl.BlockSpec((MOBA_BLOCK, ATT_WIDTH), lambda b, i: (b * n_blocks + i, 0)),
        out_shape=jax.ShapeDtypeStruct((bsz * t, ATT_WIDTH), F32),
        scratch_shapes=[
            pltpu.VMEM((ATT_HEADS, t, 2 * dh), BF16),
            pltpu.VMEM((t, ATT_WIDTH), BF16),
            pltpu.VMEM((ATT_HEADS * SUBLANES, ATT_WIDTH), F32),
        ],
        compiler_params=_params(2),
        name="moba_prompt",
    )(q, k, v, kbias)


def _moba_sample_kernel(pt_ref, q_ref, kn_ref, vn_ref, e_ref, *refs, n_pages, n_seq):
    del pt_ref
    k_refs = refs[:n_seq * n_pages]
    v_refs = refs[n_seq * n_pages:2 * n_seq * n_pages]
    o_ref = refs[2 * n_seq * n_pages]
    rows = q_ref.shape[1]
    pages_per_block = MOBA_BLOCK // PAGE_SIZE
    n_blocks = n_pages // pages_per_block
    dh = ATT_HEAD_DIM
    pairs = [(g, h) for g in range(n_seq) for h in range(ATT_HEADS)]

    def block_rows(page_refs, g, n, h):
        first = g * n_pages + n * pages_per_block
        return jnp.concatenate(
            [page_refs[j][0, pl.ds(h, PAGE_SIZE, stride=ATT_HEADS), :] for j in range(first, first + pages_per_block)],
            axis=0)

    qs, gate_rows, s_rows = [], [], []
    for g, h in pairs:
        q = q_ref[g, :, h * dh:(h + 1) * dh]
        qs.append((q * (dh ** -0.5)).astype(BF16))
        mean_rows, s_parts = [], []
        for n in range(n_blocks):
            kb = block_rows(k_refs, g, n, h)
            mean_rows.append(jnp.sum(kb, axis=0, keepdims=True) / MOBA_BLOCK)
            s_parts.append(lax.dot_general(qs[-1], kb.astype(BF16), _NT, preferred_element_type=F32))
        mean_rows.append(jnp.zeros((LANES - n_blocks, dh), F32))
        gate_rows.append(lax.dot_general(q, jnp.concatenate(mean_rows, axis=0), _NT,
                                         precision=lax.Precision.HIGHEST, preferred_element_type=F32))
        s_rows.append(jnp.concatenate(s_parts, axis=1))
    sel = _select_blocks(jnp.concatenate(gate_rows, axis=0), n_blocks)
    sel_keys = jnp.dot(sel.astype(BF16), e_ref[...], preferred_element_type=F32)
    s = jnp.where(sel_keys > 0.5, jnp.concatenate(s_rows, axis=0), NEG_INF)

    pad = jnp.zeros((LANES - rows, dh), F32)
    padded = lambda ref, g, h: jnp.concatenate([ref[g, :, h * dh:(h + 1) * dh], pad], axis=0).astype(BF16)
    s_own = jnp.concatenate(
        [lax.dot_general(qs[i], padded(kn_ref, g, h), _NT, preferred_element_type=F32)
         for i, (g, h) in enumerate(pairs)], axis=0)
    t_row = lax.broadcasted_iota(jnp.int32, s_own.shape, 0) % rows
    key = lax.broadcasted_iota(jnp.int32, s_own.shape, 1)
    s_own = jnp.where(key <= t_row, s_own, NEG_INF)

    m = jnp.maximum(jnp.max(s, axis=-1, keepdims=True), jnp.max(s_own, axis=-1, keepdims=True))
    p = jnp.exp(s - m)
    p_own = jnp.exp(s_own - m)
    inv = 1.0 / (jnp.sum(p, axis=-1, keepdims=True) + jnp.sum(p_own, axis=-1, keepdims=True))
    pb = p.astype(BF16)
    pb_own = p_own.astype(BF16)
    for i, (g, h) in enumerate(pairs):
        r0 = i * rows
        o = jnp.dot(pb_own[r0:r0 + rows], padded(vn_ref, g, h), preferred_element_type=F32)
        for n in range(n_blocks):
            o = o + jnp.dot(pb[r0:r0 + rows, n * MOBA_BLOCK:(n + 1) * MOBA_BLOCK],
                            block_rows(v_refs, g, n, h).astype(BF16), preferred_element_type=F32)
        o_ref[g, :, h * dh:(h + 1) * dh] = o * inv[r0:r0 + rows]


def _moba_sample(q, k_new, v_new, cache_k, cache_v, page_table, l, n_seq=2):
    bsz, rows, _ = q.shape
    n_pages = page_table.shape[1]
    n_phys = cache_k.shape[0] // DEPTH
    past_len = n_pages * PAGE_SIZE
    assert past_len % MOBA_BLOCK == 0 and past_len // MOBA_BLOCK <= SUBLANES and bsz % n_seq == 0
    base = l * n_phys
    tok_spec = pl.BlockSpec((n_seq, rows, ATT_WIDTH), lambda b, pt: (b, 0, 0))
    page_specs = [
        pl.BlockSpec((1, PAGE_SIZE * ATT_HEADS, ATT_HEAD_DIM),
                     lambda b, pt, g=g, j=j: (base + pt[(b * n_seq + g) * n_pages + j], 0, 0))
        for g in range(n_seq) for j in range(n_pages)
    ]
    grid_spec = pltpu.PrefetchScalarGridSpec(
        num_scalar_prefetch=1,
        grid=(bsz // n_seq,),
        in_specs=[tok_spec, tok_spec, tok_spec,
                  pl.BlockSpec((LANES, past_len), lambda b, pt: (0, 0))] + page_specs + page_specs,
        out_specs=tok_spec,
    )
    return pl.pallas_call(
        functools.partial(_moba_sample_kernel, n_pages=n_pages, n_seq=n_seq),
        grid_spec=grid_spec,
        out_shape=jax.ShapeDtypeStruct((bsz, rows, ATT_WIDTH), F32),
        compiler_params=_params(1),
        name="moba_sample",
    )(page_table.reshape(-1), q, k_new, v_new, _block_expander(past_len),
      *([cache_k] * (n_seq * n_pages)), *([cache_v] * (n_seq * n_pages)))


def _pool_windows_to_lanes(sums, counts, cur):
    lane = lax.broadcasted_iota(jnp.int32, cur.shape, 1)
    d = sums[-1] / counts[-1]
    for g in range(len(POOL_WINDOWS) - 2, -1, -1):
        d = jnp.where(lane < (g + 1) * POOL_GROUP_WIDTH, sums[g] / counts[g], d)
    return d - cur


def _conv_tail(y, ln_g, ln_b, pw_w, pw_b):
    yc = y - jnp.mean(y, axis=-1, keepdims=True)
    yn = yc * lax.rsqrt(jnp.mean(yc * yc, axis=-1, keepdims=True) + EPS) * ln_g + ln_b
    act = yn * jax.nn.sigmoid(yn)
    return jnp.dot(act.astype(BF16), pw_w, preferred_element_type=F32) + pw_b


POOL_HALO = 16
CONV_HALO = 32


def _poolconv_prompt_kernel(x_ref, u_ref, ag_ref, ya_ref, pw_ref, ps_ref, dw_ref, db_ref, lg_ref, lb_ref,
                            cw_ref, cb_ref, wp_ref, wc_ref, wa_ref, o_ref, ht_ref, e_ref, h_ref, hs_ref, *, tt,
                            side_work=None):
    ti = pl.program_id(1)

    @pl.when(ti == 0)
    def _():
        e_ref[0:POOL_HALO, :] = jnp.zeros((POOL_HALO, POOL_WIDTH), F32)
        h_ref[0:CONV_HALO, :] = jnp.zeros((CONV_HALO, CONV_WIDTH), F32)

    if side_work is not None:
        side_work()

    u = u_ref[...]
    e_ref[POOL_HALO:POOL_HALO + tt, :] = u
    a = ag_ref[:, 0:CONV_WIDTH]
    gate = ag_ref[:, CONV_WIDTH:2 * CONV_WIDTH]
    h_ref[CONV_HALO:CONV_HALO + tt, :] = a * jax.nn.sigmoid(gate)

    pos = ti * tt + lax.broadcasted_iota(jnp.int32, (tt, 1), 0)
    sums, counts = [], []
    run = u
    taken = 1
    for win in POOL_WINDOWS:
        while taken < win:
            run = run + e_ref[POOL_HALO - taken:POOL_HALO - taken + tt, :]
            taken += 1
        sums.append(run)
        counts.append(jnp.minimum(pos + 1, win).astype(F32))
    d = _pool_windows_to_lanes(sums, counts, u)
    y_pool = jnp.dot(d.astype(BF16), pw_ref[...], preferred_element_type=F32) * ps_ref[...]

    span = tt + CONV_HALO - SUBLANES
    for s in range(1, SUBLANES):
        hs_ref[s - 1, 0:span, :] = h_ref[s:s + span, :]
    y = jnp.zeros((tt, CONV_WIDTH), F32) + db_ref[...]
    for j in range(CONV_K):
        r0 = CONV_HALO - CONV_BUF + j
        s = r0 % SUBLANES
        rows = h_ref[r0:r0 + tt, :] if s == 0 else hs_ref[s - 1, r0 - s:r0 - s + tt, :]
        y = y + rows * dw_ref[j:j + 1, :]
    y_conv = _conv_tail(y, lg_ref[...], lb_ref[...], cw_ref[...], cb_ref[...])

    out = x_ref[...] + jnp.dot(y_pool.astype(BF16), wp_ref[...], preferred_element_type=F32)
    out = out + jnp.dot(y_conv.astype(BF16), wc_ref[...], preferred_element_type=F32)
    o_ref[...] = out + jnp.dot(ya_ref[...].astype(BF16), wa_ref[...], preferred_element_type=F32)

    ht_ref[0] = h_ref[tt:tt + CONV_HALO, :]
    e_ref[0:POOL_HALO, :] = e_ref[tt:tt + POOL_HALO, :]
    h_ref[0:CONV_HALO, :] = h_ref[tt:tt + CONV_HALO, :]


def _mix_out_prompt(x, u, ag, y_att, bsz, t, weights, w_out, l, tt=512):
    pool_w, pool_s, dw_w, dw_b, ln_g, ln_b, pw_w, pw_b = weights
    nt = t // tt
    row = lambda b, i: (b * nt + i, 0)
    w_rows = lambda width, blk: pl.BlockSpec((None, width, D_MODEL), lambda b, i: (l, blk, 0))
    return pl.pallas_call(
        functools.partial(_poolconv_prompt_kernel, tt=tt),
        grid=(bsz, nt),
        in_specs=[
            pl.BlockSpec((tt, D_MODEL), row),
            pl.BlockSpec((tt, POOL_WIDTH), row),
            pl.BlockSpec((tt, 2 * CONV_WIDTH), row),
            pl.BlockSpec((tt, ATT_WIDTH), row),
            _layer_spec((POOL_WIDTH, POOL_WIDTH), l, 2),
            _layer_spec((1, POOL_WIDTH), l, 2),
            _layer_spec((CONV_HALO, CONV_WIDTH), l, 2),
            _layer_spec((1, CONV_WIDTH), l, 2),
            _layer_spec((1, CONV_WIDTH), l, 2),
            _layer_spec((1, CONV_WIDTH), l, 2),
            _layer_spec((CONV_WIDTH, CONV_WIDTH), l, 2),
            _layer_spec((1, CONV_WIDTH), l, 2),
            w_rows(POOL_WIDTH, 0),
            w_rows(CONV_WIDTH, POOL_WIDTH // CONV_WIDTH),
            w_rows(ATT_WIDTH, (POOL_WIDTH + CONV_WIDTH) // ATT_WIDTH),
        ],
        out_specs=[
            pl.BlockSpec((tt, D_MODEL), row),
            pl.BlockSpec((1, CONV_HALO, CONV_WIDTH), lambda b, i: (b, 0, 0)),
        ],
        out_shape=[
            jax.ShapeDtypeStruct((bsz * t, D_MODEL), F32),
            jax.ShapeDtypeStruct((bsz, CONV_HALO, CONV_WIDTH), F32),
        ],
        scratch_shapes=[
            pltpu.VMEM((POOL_HALO + tt, POOL_WIDTH), F32),
            pltpu.VMEM((CONV_HALO + tt, CONV_WIDTH), F32),
            pltpu.VMEM((SUBLANES - 1, CONV_HALO + tt, CONV_WIDTH), F32),
        ],
        compiler_params=_params(2),
        name="mix_out_prompt",
    )(x, u, ag, y_att, pool_w, pool_s, dw_w, dw_b, ln_g, ln_b, pw_w, pw_b, w_out, w_out, w_out)


N_MIX_IN = 15


def _mix_out_moba_kernel(pt_ref, *refs, tt, n_pages, n_seq):
    n_moba_in = 4 + 2 * n_seq * n_pages
    mix_in = refs[:N_MIX_IN]
    moba_in = refs[N_MIX_IN:N_MIX_IN + n_moba_in]
    o_ref, ht_ref, att_ref = refs[N_MIX_IN + n_moba_in:N_MIX_IN + n_moba_in + 3]
    scratch = refs[N_MIX_IN + n_moba_in + 3:]
    side = functools.partial(_moba_sample_kernel, pt_ref, *moba_in, att_ref, n_pages=n_pages, n_seq=n_seq)
    _poolconv_prompt_kernel(*mix_in, o_ref, ht_ref, *scratch, tt=tt, side_work=side)


def _mix_out_prompt_moba_sample(x, u, ag, y_att, bsz, t, weights, w_out, q_s, k_new, v_new, cache_k, cache_v,
                                page_table, l, tt=256):
    pool_w, pool_s, dw_w, dw_b, ln_g, ln_b, pw_w, pw_b = weights
    nt = t // tt
    dbsz, rows, _ = q_s.shape
    n_seq = dbsz // (bsz * nt)
    assert n_seq * bsz * nt == dbsz
    n_pages = page_table.shape[1]
    n_phys = cache_k.shape[0] // DEPTH
    past_len = n_pages * PAGE_SIZE
    assert past_len % MOBA_BLOCK == 0 and past_len // MOBA_BLOCK <= SUBLANES
    base = l * n_phys
    row = lambda b, i, pt: (b * nt + i, 0)
    w_rows = lambda width, blk: pl.BlockSpec((None, width, D_MODEL), lambda b, i, pt: (l, blk, 0))
    tok_spec = pl.BlockSpec((n_seq, rows, ATT_WIDTH), lambda b, i, pt: (b * nt + i, 0, 0))
    page_specs = [
        pl.BlockSpec((1, PAGE_SIZE * ATT_HEADS, ATT_HEAD_DIM),
                     lambda b, i, pt, g=g, j=j: (base + pt[((b * nt + i) * n_seq + g) * n_pages + j], 0, 0))
        for g in range(n_seq) for j in range(n_pages)
    ]
    grid_spec = pltpu.PrefetchScalarGridSpec(
        num_scalar_prefetch=1,
        grid=(bsz, nt),
        in_specs=[
            pl.BlockSpec((tt, D_MODEL), row),
            pl.BlockSpec((tt, POOL_WIDTH), row),
            pl.BlockSpec((tt, 2 * CONV_WIDTH), row),
            pl.BlockSpec((tt, ATT_WIDTH), row),
            _layer_spec((POOL_WIDTH, POOL_WIDTH), l, 3),
            _layer_spec((1, POOL_WIDTH), l, 3),
            _layer_spec((CONV_HALO, CONV_WIDTH), l, 3),
            _layer_spec((1, CONV_WIDTH), l, 3),
            _layer_spec((1, CONV_WIDTH), l, 3),
            _layer_spec((1, CONV_WIDTH), l, 3),
            _layer_spec((CONV_WIDTH, CONV_WIDTH), l, 3),
            _layer_spec((1, CONV_WIDTH), l, 3),
            w_rows(POOL_WIDTH, 0),
            w_rows(CONV_WIDTH, POOL_WIDTH // CONV_WIDTH),
            w_rows(ATT_WIDTH, (POOL_WIDTH + CONV_WIDTH) // ATT_WIDTH),
            tok_spec, tok_spec, tok_spec,
            pl.BlockSpec((LANES, past_len), lambda b, i, pt: (0, 0)),
        ] + page_specs + page_specs,
        out_specs=[
            pl.BlockSpec((tt, D_MODEL), row),
            pl.BlockSpec((1, CONV_HALO, CONV_WIDTH), lambda b, i, pt: (b, 0, 0)),
            tok_spec,
        ],
        scratch_shapes=[
            pltpu.VMEM((POOL_HALO + tt, POOL_WIDTH), F32),
            pltpu.VMEM((CONV_HALO + tt, CONV_WIDTH), F32),
            pltpu.VMEM((SUBLANES - 1, CONV_HALO + tt, CONV_WIDTH), F32),
        ],
    )
    return pl.pallas_call(
        functools.partial(_mix_out_moba_kernel, tt=tt, n_pages=n_pages, n_seq=n_seq),
        grid_spec=grid_spec,
        out_shape=[
            jax.ShapeDtypeStruct((bsz * t, D_MODEL), F32),
            jax.ShapeDtypeStruct((bsz, CONV_HALO, CONV_WIDTH), F32),
            jax.ShapeDtypeStruct((dbsz, rows, ATT_WIDTH), F32),
        ],
        compiler_params=_params(2),
        name="mix_out_moba",
    )(page_table.reshape(-1), x, u, ag, y_att, pool_w, pool_s, dw_w, dw_b, ln_g, ln_b, pw_w, pw_b,
      w_out, w_out, w_out, q_s, k_new, v_new, _block_expander(past_len),
      *([cache_k] * (n_seq * n_pages)), *([cache_v] * (n_seq * n_pages)))


def _poolconv_sample_kernel(sp_ref, u_ref, sc_ref, ag_ref, pw_ref, ps_ref, dw_ref, db_ref, lg_ref, lb_ref,
                            cw_ref, cb_ref, yp_ref, yc_ref, h_ref, *, n_new, pos0):
    def pool_row(r):
        return sp_ref[r] if r < POOL_BUF else u_ref[r - POOL_BUF]

    for t in range(n_new):
        h_ref[t] = ag_ref[t, :, 0:CONV_WIDTH] * jax.nn.sigmoid(ag_ref[t, :, CONV_WIDTH:2 * CONV_WIDTH])

    def conv_row(r):
        return sc_ref[r] if r < CONV_BUF else h_ref[r - CONV_BUF]

    for t in range(n_new):
        cur = u_ref[t]
        sums, counts = [], []
        run = cur
        taken = 1
        for win in POOL_WINDOWS:
            while taken < win:
                run = run + pool_row(POOL_BUF + t - taken)
                taken += 1
            sums.append(run)
            counts.append(float(min(pos0 + t + 1, win)))
        d = _pool_windows_to_lanes(sums, counts, cur)
        yp_ref[t] = jnp.dot(d.astype(BF16), pw_ref[...], preferred_element_type=F32) * ps_ref[...]

        y = jnp.zeros(cur.shape, F32) + db_ref[...]
        for j in range(CONV_K):
            y = y + conv_row(t + j) * dw_ref[j:j + 1, :]
        yc_ref[t] = _conv_tail(y, lg_ref[...], lb_ref[...], cw_ref[...], cb_ref[...])


def _poolconv_sample(state_pool_t, u_t, state_conv_t, ag_t, pos0, weights, l):
    pool_w, pool_s, dw_w, dw_b, ln_g, ln_b, pw_w, pw_b = weights
    n_new, bsz, _ = u_t.shape
    full = lambda a: pl.BlockSpec(a.shape, lambda i: (0,) * a.ndim)
    out = jax.ShapeDtypeStruct((n_new, bsz, CONV_WIDTH), F32)
    return pl.pallas_call(
        functools.partial(_poolconv_sample_kernel, n_new=n_new, pos0=pos0),
        grid=(1,),
        in_specs=[
            full(state_pool_t), full(u_t), full(state_conv_t), full(ag_t),
            _layer_spec((POOL_WIDTH, POOL_WIDTH), l, 1),
            _layer_spec((1, POOL_WIDTH), l, 1),
            _layer_spec((CONV_HALO, CONV_WIDTH), l, 1),
            _layer_spec((1, CONV_WIDTH), l, 1),
            _layer_spec((1, CONV_WIDTH), l, 1),
            _layer_spec((1, CONV_WIDTH), l, 1),
            _layer_spec((CONV_WIDTH, CONV_WIDTH), l, 1),
            _layer_spec((1, CONV_WIDTH), l, 1),
        ],
        out_specs=[pl.BlockSpec(out.shape, lambda i: (0, 0, 0))] * 3,
        out_shape=[out, out, out],
        compiler_params=_params(1),
        name="poolconv_sample",
    )(state_pool_t, u_t, state_conv_t, ag_t, pool_w, pool_s, dw_w, dw_b, ln_g, ln_b, pw_w, pw_b)


def _cross_kernel(x_ref, g_ref, wq_ref, hg_ref, k_ref, v_ref, wo_ref, o_ref, *, n_seq, interleaved):
    hd = MEM_HEAD_DIM
    rows = x_ref.shape[1]
    x = x_ref[...].reshape(n_seq * rows, D_MODEL)
    xn = _rms(x, g_ref[...]).astype(BF16)
    q_all = jnp.dot(xn, wq_ref[...], preferred_element_type=F32)
    halves = range(hd // LANES)

    def head_slab(ref, g, h):
        if not interleaved:
            return ref[g, :, h * hd:(h + 1) * hd]
        return jnp.concatenate(
            [ref[g, pl.ds(half * MEM_HEADS + h, N_MEM, stride=len(halves) * MEM_HEADS), :] for half in halves],
            axis=1)

    pairs = [(g, h) for g in range(n_seq) for h in range(MEM_HEADS)]
    scores = []
    for g, h in pairs:
        q = _rms(q_all[g * rows:(g + 1) * rows, h * hd:(h + 1) * hd], hg_ref[...])
        q = (q * (hd ** -0.5)).astype(BF16)
        scores.append(lax.dot_general(q, head_slab(k_ref, g, h).astype(BF16), _NT, preferred_element_type=F32))
    s = jnp.concatenate(scores, axis=0)
    p = jnp.exp(s - jnp.max(s, axis=-1, keepdims=True))
    inv = 1.0 / jnp.sum(p, axis=-1, keepdims=True)
    o_seqs = []
    for g in range(n_seq):
        o_heads = []
        for h in range(MEM_HEADS):
            r0 = (g * MEM_HEADS + h) * rows
            o = jnp.dot(p[r0:r0 + rows].astype(BF16), head_slab(v_ref, g, h).astype(BF16),
                        preferred_element_type=F32)
            o_heads.append((o * inv[r0:r0 + rows]).astype(BF16))
        o_seqs.append(jnp.concatenate(o_heads, axis=1))
    o_all = o_seqs[0] if n_seq == 1 else jnp.concatenate(o_seqs, axis=0)
    y = x + jnp.dot(o_all, wo_ref[...], preferred_element_type=F32)
    o_ref[...] = y.reshape(n_seq, rows, D_MODEL)


def _interleaved_memory(cache):
    depth, s, n_mem, heads, hd = cache.shape
    halves = hd // LANES
    c = cache.reshape(depth, s, n_mem, heads, halves, LANES).transpose(0, 1, 2, 4, 3, 5)
    return c.reshape(depth * s, n_mem * halves * heads, LANES)


def _cross(x, ln, w_q, head_gain, mem_k, mem_v, w_o, l, n_seq, tq, seq0=None):
    s_total, r, _ = x.shape
    interleaved = seq0 is not None
    if interleaved:
        assert seq0 % n_seq == 0
        kv_spec = pl.BlockSpec((n_seq,) + mem_k.shape[1:], lambda s, i: (seq0 // n_seq + s, 0, 0))
    else:
        kv_spec = pl.BlockSpec((n_seq, N_MEM, D_MODEL), lambda s, i: (s, 0, 0))
    x_spec = pl.BlockSpec((n_seq, tq, D_MODEL), lambda s, i: (s, i, 0))
    return pl.pallas_call(
        functools.partial(_cross_kernel, n_seq=n_seq, interleaved=interleaved),
        grid=(s_total // n_seq, r // tq),
        in_specs=[x_spec, _layer_spec((1, D_MODEL), l, 2), _layer_spec((D_MODEL, D_MODEL), l, 2),
                  _layer_spec((1, MEM_HEAD_DIM), l, 2), kv_spec, kv_spec,
                  _layer_spec((D_MODEL, D_MODEL), l, 2)],
        out_specs=x_spec,
        out_shape=jax.ShapeDtypeStruct(x.shape, F32),
        compiler_params=_params(2),
        name="cross",
    )(x, ln, w_q, head_gain, mem_k, mem_v, w_o)


def _pad_rows(x, rows):
    return jnp.pad(x, ((0, 0), (0, rows - x.shape[1]), (0, 0)))


def kernel(x_prompt, x_sample, cache_att_k, cache_att_v, cache_mem_k, cache_mem_v, state_pool, state_conv, page_table, mem_prompt, ln_ffn1, w_ffn1_in, w_ffn1_out, ln_mix, w_in, pool_w, pool_scale, conv_dw_w, conv_dw_b, conv_ln_g, conv_ln_b, conv_pw_w, conv_pw_b, att_q_norm, att_k_norm, w_out, ln_cross, ln_mem, w_cq, w_ckv, cq_norm, ck_norm, w_co, ln_ffn2, w_ffn2_in, w_ffn2_out):
    bsz, seq, _ = x_prompt.shape
    dbsz, dseq, _ = x_sample.shape
    n_pages = page_table.shape[1]
    past_len = n_pages * PAGE_SIZE
    n_phys = cache_att_k.shape[1]

    bf = lambda w: w.astype(BF16)
    rowv = lambda v: v[:, None, :]
    w1i, w1o, w2i, w2o = bf(w_ffn1_in), bf(w_ffn1_out), bf(w_ffn2_in), bf(w_ffn2_out)
    w_in_b, w_out_b, w_cq_b, w_ckv_b, w_co_b = bf(w_in), bf(w_out), bf(w_cq), bf(w_ckv), bf(w_co)
    groups = len(POOL_WINDOWS)
    eye = jnp.eye(groups, dtype=F32)
    pool_bd = bf((pool_w[:, :, :, None, :] * eye[None, :, None, :, None])
                 .reshape(DEPTH, POOL_WIDTH, POOL_WIDTH))
    dw_pad = jnp.pad(conv_dw_w, ((0, 0), (0, CONV_HALO - CONV_K), (0, 0)))
    mixer_w = (pool_bd, rowv(pool_scale), dw_pad, rowv(conv_dw_b), rowv(conv_ln_g), rowv(conv_ln_b),
               bf(conv_pw_w), rowv(conv_pw_b))
    ln1, lnm, lnc, lnmem, ln2 = rowv(ln_ffn1), rowv(ln_mix), rowv(ln_cross), rowv(ln_mem), rowv(ln_ffn2)

    tables_p = _rope_tables(jnp.arange(seq, dtype=jnp.int32))
    tm_s = dbsz * dseq
    tables_s = _rope_tables(past_len + jnp.arange(tm_s, dtype=jnp.int32) % dseq)

    c1 = POOL_WIDTH
    c3 = c1 + 2 * CONV_WIDTH
    c4 = c3 + ATT_WIDTH
    c5 = c4 + ATT_WIDTH
    mix_segs = [(c3, ATT_WIDTH, ATT_HEAD_DIM, 0, True, False), (c4, ATT_WIDTH, ATT_HEAD_DIM, 1, True, True),
                (c5, ATT_WIDTH, ATT_HEAD_DIM, None, False, True),
                (0, c1, None, None, False, False), (c1, 2 * CONV_WIDTH, None, None, False, False)]
    mem_segs = [(0, D_MODEL, MEM_HEAD_DIM, 0, False, True), (D_MODEL, D_MODEL, MEM_HEAD_DIM, None, False, True)]
    out_rows = (0, c1, c1 + CONV_WIDTH)

    cache_k = cache_att_k.reshape(DEPTH * n_phys, PAGE_SIZE * ATT_HEADS, ATT_HEAD_DIM)
    cache_v = cache_att_v.reshape(DEPTH * n_phys, PAGE_SIZE * ATT_HEADS, ATT_HEAD_DIM)
    mem_k_s = _interleaved_memory(cache_mem_k)
    mem_v_s = _interleaved_memory(cache_mem_v)
    mem_rows = mem_prompt.reshape(bsz * N_MEM, D_MODEL)

    xp = x_prompt.reshape(bsz * seq, D_MODEL)
    xs = x_sample.reshape(dbsz * dseq, D_MODEL)
    rows_pad = SUBLANES
    outs = {name: [] for name in ("p_pool", "p_conv", "s_pool", "s_conv")}
    cq_gain = rowv(cq_norm)
    kv_p = kv_s = mem_p = None

    for l in range(DEPTH):
        att_gains = jnp.stack([att_q_norm[l], att_k_norm[l]])

        mem_p = _norm_proj(mem_rows, lnmem, w_ckv_b, l, ck_norm[l][None], mem_segs, stacked=mem_p)
        mk, mv = (a.reshape(DEPTH * bsz, -1, LANES) for a in mem_p)
        xp = _ffn(xp, ln1, w1i, w1o, l)
        q, k, v, u, ag = _norm_proj(xp, lnm, w_in_b, l, att_gains, mix_segs, tables=tables_p, stacked=kv_p)
        kv_p = (k, v)
        y_att = _moba_prompt(q, k, v, bsz, seq, l)
        p_pool = u.reshape(bsz, seq, POOL_WIDTH)[:, seq - POOL_BUF:]

        xs = _ffn(xs, ln1, w1i, w1o, l, tm=tm_s)
        q_s, k_s, v_s, u_s, ag_s = _norm_proj(xs, lnm, w_in_b, l, att_gains, mix_segs, tables=tables_s, tm=tm_s,
                                              stacked=kv_s)
        kv_s = (k_s, v_s)
        pad3 = lambda a: _pad_rows(a.reshape(dbsz, dseq, ATT_WIDTH), 2 * SUBLANES)
        xp, h_tail, y_att_s = _mix_out_prompt_moba_sample(
            xp, u, ag, y_att, bsz, seq, mixer_w, w_out_b, pad3(q_s), pad3(k_s[l]), pad3(v_s[l]),
            cache_k, cache_v, page_table, l)
        xp = _cross(xp.reshape(bsz, seq, D_MODEL), lnc, w_cq_b, cq_gain, mk, mv, w_co_b, l, 1, 512,
                    seq0=l * bsz).reshape(bsz * seq, D_MODEL)
        xp = _ffn(xp, ln2, w2i, w2o, l)
        outs["p_pool"].append(p_pool)
        outs["p_conv"].append(h_tail[:, CONV_HALO - CONV_BUF:])

        y_att = y_att_s[:, :dseq].reshape(dbsz * dseq, ATT_WIDTH)
        tmaj = lambda a: jnp.swapaxes(a, 0, 1)
        u3 = u_s.reshape(dbsz, dseq, POOL_WIDTH)
        y_pool, y_conv, h_new = _poolconv_sample(
            tmaj(state_pool[l]), tmaj(u3), tmaj(state_conv[l]), tmaj(ag_s.reshape(dbsz, dseq, 2 * CONV_WIDTH)),
            past_len, mixer_w, l)
        bmaj = lambda a: jnp.swapaxes(a, 0, 1).reshape(dbsz * dseq, a.shape[-1])
        xs = _out_proj(xs, [bmaj(y_pool), bmaj(y_conv), y_att], w_out_b, l, out_rows, tm=tm_s)
        n_seq = 4
        xs = _cross(_pad_rows(xs.reshape(dbsz, dseq, D_MODEL), rows_pad), lnc, w_cq_b, cq_gain, mem_k_s, mem_v_s,
                    w_co_b, l, n_seq, rows_pad, seq0=l * dbsz)[:, :dseq].reshape(dbsz * dseq, D_MODEL)
        xs = _ffn(xs, ln2, w2i, w2o, l, tm=tm_s)
        outs["s_pool"].append(jnp.concatenate([state_pool[l][:, dseq:], u3], axis=1))
        outs["s_conv"].append(jnp.concatenate([state_conv[l][:, dseq:], jnp.swapaxes(h_new, 0, 1)], axis=1))

    st = lambda name: jnp.stack(outs[name])
    heads5 = lambda a, b, t: a.reshape(DEPTH, b, t, ATT_HEADS, ATT_HEAD_DIM)
    tiles = MEM_HEAD_DIM // LANES
    mem5 = lambda a: (a.reshape(DEPTH, bsz, N_MEM, tiles, MEM_HEADS, LANES).transpose(0, 1, 2, 4, 3, 5)
                      .reshape(DEPTH, bsz, N_MEM, MEM_HEADS, MEM_HEAD_DIM))
    return (xp.reshape(bsz, seq, D_MODEL), xs.reshape(dbsz, dseq, D_MODEL),
            heads5(kv_p[0], bsz, seq), heads5(kv_p[1], bsz, seq), mem5(mem_p[0]), mem5(mem_p[1]), st("p_pool"),
            st("p_conv"), heads5(kv_s[0], dbsz, dseq), heads5(kv_s[1], dbsz, dseq), st("s_pool"), st("s_conv"))
```

```python
import functools

import jax
import jax.numpy as jnp
import numpy as np
from jax import lax
from jax.experimental import pallas as pl
from jax.experimental.pallas import tpu as pltpu

D_MODEL = 1024
DEPTH = 2
PAGE_SIZE = 128

POOL_WIDTH = D_MODEL // 4
POOL_WINDOWS = (2, 4, 8, 16)
POOL_GROUP_WIDTH = POOL_WIDTH // len(POOL_WINDOWS)
POOL_BUF = max(POOL_WINDOWS) - 1

CONV_WIDTH = D_MODEL // 4
CONV_K = 31
CONV_BUF = CONV_K - 1

ATT_HEADS = 4
ATT_HEAD_DIM = D_MODEL // 8
ATT_WIDTH = ATT_HEADS * ATT_HEAD_DIM
ROT_DIM = ATT_HEAD_DIM // 4
ROPE_THETA = 500000.0
MOBA_BLOCK = 256
MOBA_TOPK = 3

IN_WIDTH = POOL_WIDTH + 2 * CONV_WIDTH + 3 * ATT_WIDTH

N_MEM = 256
MEM_HEADS = 4
MEM_HEAD_DIM = D_MODEL // MEM_HEADS

D_FF = ((8 * D_MODEL // 3 + 127) // 128) * 128
EPS = 1e-6

LANES = 128
SUBLANES = 8
VMEM_LIMIT_BYTES = 56 * 1024 * 1024

F32 = jnp.float32
BF16 = jnp.bfloat16
NEG_INF = float("-inf")

_NT = (((1,), (1,)), ((), ()))


def _params(n_axes):
    return pltpu.CompilerParams(dimension_semantics=("arbitrary",) * n_axes,
                                vmem_limit_bytes=VMEM_LIMIT_BYTES)


def _rms(x, g):
    return x * lax.rsqrt(jnp.mean(x * x, axis=-1, keepdims=True) + EPS) * g


def _layer_spec(shape, l, n_grid):
    zeros = (0,) * len(shape)
    if n_grid == 1:
        return pl.BlockSpec((None,) + tuple(shape), lambda i: (l,) + zeros)
    if n_grid == 2:
        return pl.BlockSpec((None,) + tuple(shape), lambda i, j: (l,) + zeros)
    return pl.BlockSpec((None,) + tuple(shape), lambda i, j, k: (l,) + zeros)


FFN_CHUNK = 256


def _ffn_kernel(xa_ref, xb_ref, g_ref, wi_ref, wo_ref, oa_ref, ob_ref):
    i = pl.program_id(0)
    x = jnp.where(i == 0, xb_ref[...], xa_ref[...])
    xn = _rms(x, g_ref[...]).astype(BF16)
    acc = None
    for c in range(D_FF // FFN_CHUNK):
        lo = c * FFN_CHUNK
        gate = jnp.dot(xn, wi_ref[:, lo:lo + FFN_CHUNK], preferred_element_type=F32)
        up = jnp.dot(xn, wi_ref[:, D_FF + lo:D_FF + lo + FFN_CHUNK], preferred_element_type=F32)
        act = (gate * jax.nn.sigmoid(gate) * up).astype(BF16)
        part = jnp.dot(act, wo_ref[lo:lo + FFN_CHUNK, :], preferred_element_type=F32)
        acc = part if acc is None else acc + part
    oa_ref[...] = x + 0.5 * acc

    @pl.when(i == 0)
    def _():
        ob_ref[...] = oa_ref[...]


def _ffn(xa, xb, ln, w_in, w_out, l, tm=512):
    n_a = xa.shape[0] // tm
    assert xa.shape[0] == n_a * tm and xb.shape[0] == tm
    a_spec = pl.BlockSpec((tm, D_MODEL), lambda i: (jnp.maximum(i - 1, 0), 0))
    b_spec = pl.BlockSpec((tm, D_MODEL), lambda i: (0, 0))
    return pl.pallas_call(
        _ffn_kernel,
        grid=(n_a + 1,),
        in_specs=[
            a_spec, b_spec,
            _layer_spec((1, D_MODEL), l, 1),
            _layer_spec((D_MODEL, 2 * D_FF), l, 1),
            _layer_spec((D_FF, D_MODEL), l, 1),
        ],
        out_specs=[a_spec, b_spec],
        out_shape=[jax.ShapeDtypeStruct(xa.shape, F32), jax.ShapeDtypeStruct(xb.shape, F32)],
        compiler_params=_params(1),
        name="ffn",
    )(xa, xb, ln, w_in, w_out)


def _norm_proj_kernel(*refs, segs, rope):
    x_ref, g_ref, w_ref, hg_ref = refs[:4]
    if rope:
        cos_ref, sin_lo_ref, sin_hi_ref = refs[4:7]
    out_refs = refs[len(refs) - len(segs):]
    tm = x_ref.shape[0]
    xn = _rms(x_ref[...], g_ref[...]).astype(BF16)
    for (c0, width, head_dim, gain_row, use_rope, head_rows), o_ref in zip(segs, out_refs):
        z = jnp.dot(xn, w_ref[:, c0:c0 + width], preferred_element_type=F32)
        if head_dim is None:
            o_ref[...] = z
            continue
        n_heads = width // head_dim
        for h in range(n_heads):
            zh = z[:, h * head_dim:(h + 1) * head_dim]
            if gain_row is not None:
                zh = _rms(zh, hg_ref[gain_row:gain_row + 1, :])
            if use_rope:
                half = ROT_DIM // 2
                zh = (zh * cos_ref[...]
                      + pltpu.roll(zh, half, 1) * sin_hi_ref[...]
                      + pltpu.roll(zh, LANES - half, 1) * sin_lo_ref[...])
            if not head_rows:
                o_ref[:, h * head_dim:(h + 1) * head_dim] = zh
                continue
            tiles = head_dim // LANES
            for tile in range(tiles):
                rows = pl.ds(tile * n_heads + h, tm, stride=n_heads * tiles)
                piece = zh[:, tile * LANES:(tile + 1) * LANES]
                if len(o_ref.shape) == 2:
                    o_ref[rows, :] = piece
                else:
                    for d in range(o_ref.shape[0]):
                        o_ref[d, rows, :] = piece


def _norm_proj(x, ln, w, l, head_gains, segs, tables=None, tm=512, stacked=None):
    m = x.shape[0]
    n_cols = w.shape[-1]
    rope = tables is not None
    in_specs = [
        pl.BlockSpec((tm, D_MODEL), lambda i: (i, 0)),
        _layer_spec((1, D_MODEL), l, 1),
        _layer_spec((D_MODEL, n_cols), l, 1),
        pl.BlockSpec(head_gains.shape, lambda i: (0, 0)),
    ]
    args = [x, ln, w, head_gains]
    if rope:
        period = tables[0].shape[0] // tm
        for t in tables:
            in_specs.append(pl.BlockSpec((tm, LANES), lambda i: (i % period, 0)))
            args.append(t)
    out_specs, out_shape, aliases = [], [], {}
    for k, s in enumerate(segs):
        width, head_dim, head_rows = s[1], s[2], s[5]
        if not head_rows:
            out_specs.append(pl.BlockSpec((tm, width), lambda i: (i, 0)))
            out_shape.append(jax.ShapeDtypeStruct((m, width), F32))
            continue
        per_token = width // LANES
        out_shape.append(jax.ShapeDtypeStruct((DEPTH, m * per_token, LANES), F32))
        if stacked is None:
            out_specs.append(pl.BlockSpec((DEPTH, tm * per_token, LANES), lambda i: (0, i, 0)))
        else:
            out_specs.append(pl.BlockSpec((None, tm * per_token, LANES), lambda i: (l, i, 0)))
            aliases[len(args)] = k
            in_specs.append(pl.BlockSpec(memory_space=pl.ANY))
            args.append(stacked[len(aliases) - 1])
    return pl.pallas_call(
        functools.partial(_norm_proj_kernel, segs=tuple(segs), rope=rope),
        grid=(m // tm,),
        in_specs=in_specs,
        out_specs=out_specs,
        out_shape=out_shape,
        input_output_aliases=aliases,
        compiler_params=_params(1),
        name="norm_proj",
    )(*args)


def _rope_tables(positions):
    half = ROT_DIM // 2
    inv_freq = jnp.power(jnp.float32(ROPE_THETA), -jnp.arange(half, dtype=F32) / half)
    ang = positions.astype(F32)[:, None] * inv_freq[None, :]
    cos, sin = jnp.cos(ang), jnp.sin(ang)
    n = positions.shape[0]
    rest = ATT_HEAD_DIM - ROT_DIM
    cos_t = jnp.concatenate([cos, cos, jnp.ones((n, rest), F32)], axis=1)
    sin_lo = jnp.concatenate([-sin, jnp.zeros((n, half + rest), F32)], axis=1)
    sin_hi = jnp.concatenate([jnp.zeros((n, half), F32), sin, jnp.zeros((n, rest), F32)], axis=1)
    return cos_t, sin_lo, sin_hi


def _out_proj_kernel(*refs, n_terms):
    x_ref = refs[0]
    y_refs = refs[1:1 + n_terms]
    w_refs = refs[1 + n_terms:1 + 2 * n_terms]
    o_ref = refs[1 + 2 * n_terms]
    acc = x_ref[...]
    for y_ref, w_ref in zip(y_refs, w_refs):
        acc = acc + jnp.dot(y_ref[...].astype(BF16), w_ref[...], preferred_element_type=F32)
    o_ref[...] = acc


def _out_proj(x, ys, w, l, row_starts, tm=512):
    m = x.shape[0]
    in_specs = [pl.BlockSpec((tm, D_MODEL), lambda i: (i, 0))]
    for y in ys:
        in_specs.append(pl.BlockSpec((tm, y.shape[1]), lambda i: (i, 0)))
    for y, r0 in zip(ys, row_starts):
        width = y.shape[1]
        blk = r0 // width
        in_specs.append(pl.BlockSpec((None, width, D_MODEL), lambda i, blk=blk: (l, blk, 0)))
    return pl.pallas_call(
        functools.partial(_out_proj_kernel, n_terms=len(ys)),
        grid=(m // tm,),
        in_specs=in_specs,
        out_specs=pl.BlockSpec((tm, D_MODEL), lambda i: (i, 0)),
        out_shape=jax.ShapeDtypeStruct((m, D_MODEL), F32),
        compiler_params=_params(1),
        name="out_proj",
    )(x, *ys, *([w] * len(ys)))


def _select_blocks(gate, n_allowed):
    lane = lax.broadcasted_iota(jnp.int32, gate.shape, 1)
    allowed = lane < n_allowed
    g = jnp.where(allowed, gate, NEG_INF)
    rank = jnp.zeros(gate.shape, F32)
    for r in range(1, SUBLANES):
        lower = pltpu.roll(g, r, 1)
        higher = pltpu.roll(g, LANES - r, 1)
        rank = rank + jnp.where(lower >= g, 1.0, 0.0) + jnp.where(higher > g, 1.0, 0.0)
    return jnp.where(allowed, jnp.where(rank < MOBA_TOPK, 1.0, 0.0), 0.0)


def _dot_nt_split(a, b):
    a_hi = a.astype(BF16)
    a_lo = (a - a_hi.astype(F32)).astype(BF16)
    b_hi = b.astype(BF16)
    b_lo = (b - b_hi.astype(F32)).astype(BF16)
    m = a.shape[0]
    by_hi = lax.dot_general(jnp.concatenate([a_hi, a_lo], axis=0), b_hi, _NT, preferred_element_type=F32)
    return by_hi[0:m] + by_hi[m:2 * m] + lax.dot_general(a_hi, b_lo, _NT, preferred_element_type=F32)


def _block_expander(n_keys):
    blk = np.arange(n_keys) // MOBA_BLOCK
    return jnp.asarray((np.arange(LANES)[:, None] == blk[None, :]).astype(np.float32), dtype=BF16)


MASK_BIAS = -1e30


def _select_blocks_t(gate_t, n_allowed):
    blk = lax.broadcasted_iota(jnp.int32, gate_t.shape, 0)
    allowed = blk < n_allowed
    g = jnp.where(allowed, gate_t, NEG_INF)
    rank = jnp.zeros(gate_t.shape, F32)
    for m in range(SUBLANES):
        gm = g[m:m + 1, :]
        tie = jnp.where(gm == g, jnp.where(blk > m, 1.0, 0.0), 0.0)
        rank = rank + jnp.where(gm > g, 1.0, tie)
    return jnp.where(allowed, jnp.where(rank < MOBA_TOPK, 1.0, 0.0), 0.0)


def _moba_prompt_kernel(q_ref, k_ref, v_ref, kbias_ref, o_ref, ka_ref, vb_ref, km_ref, *, n_blocks):
    qi = pl.program_id(1)
    blk = MOBA_BLOCK
    dh = ATT_HEAD_DIM
    heads = range(ATT_HEADS)

    @pl.when(qi == 0)
    def _():
        km_ref[...] = jnp.zeros(km_ref.shape, F32)
        for h in heads:
            vb_ref[:, h * dh:(h + 1) * dh] = v_ref[pl.ds(h, n_blocks * blk, stride=ATT_HEADS), :].astype(BF16)
            ka_ref[h, :, dh:2 * dh] = kbias_ref[h]
            for n in range(n_blocks):
                kb = k_ref[pl.ds(n * blk * ATT_HEADS + h, blk, stride=ATT_HEADS), :]
                ka_ref[h, n * blk:(n + 1) * blk, 0:dh] = kb.astype(BF16)
                r = h * SUBLANES + n
                km_ref[r:r + 1, h * dh:(h + 1) * dh] = jnp.mean(kb, axis=0, keepdims=True)

    row = lax.broadcasted_iota(jnp.int32, (blk, blk), 0)
    col = lax.broadcasted_iota(jnp.int32, (blk, blk), 1)

    def step(c):
        n_keys = (c + 1) * blk
        gated = c > MOBA_TOPK
        if gated:
            gate_all = _dot_nt_split(km_ref[...], q_ref[...])
            n_rows = ATT_HEADS * SUBLANES
            flags = []
            for h in heads:
                gate_t = gate_all[h * SUBLANES:(h + 1) * SUBLANES]
                blk_id = lax.broadcasted_iota(jnp.int32, gate_t.shape, 0)
                flags.append(jnp.where(blk_id == c, 0.0, 1.0 - _select_blocks_t(gate_t, c)))
            flags.append(jnp.zeros((LANES - n_rows, blk), F32))
            unchosen = jnp.concatenate(flags, axis=0).T.astype(BF16)
        for h in heads:
            q = q_ref[:, h * dh:(h + 1) * dh]
            qs = (q * (dh ** -0.5)).astype(BF16)
            if gated:
                qa = jnp.concatenate([qs, unchosen], axis=1)
                s = lax.dot_general(qa, ka_ref[h, 0:n_keys, :], _NT, preferred_element_type=F32)
            else:
                s = lax.dot_general(qs, ka_ref[h, 0:n_keys, 0:dh], _NT, preferred_element_type=F32)
            parts = [s[:, 0:c * blk]] if c else []
            parts.append(jnp.where(row >= col, s[:, c * blk:n_keys], NEG_INF))
            s = jnp.concatenate(parts, axis=1)
            p = jnp.exp(s - jnp.max(s, axis=-1, keepdims=True))
            inv = 1.0 / jnp.sum(p, axis=-1, keepdims=True)
            o = jnp.dot(p.astype(BF16), vb_ref[0:n_keys, h * dh:(h + 1) * dh], preferred_element_type=F32)
            o_ref[:, h * dh:(h + 1) * dh] = o * inv

    for c in range(n_blocks):
        pl.when(qi == c)(functools.partial(step, c))


def _moba_prompt(q, k, v, bsz, t, l):
    assert t % MOBA_BLOCK == 0 and t // MOBA_BLOCK <= SUBLANES
    n_blocks = t // MOBA_BLOCK
    dh = ATT_HEAD_DIM
    key_blk = np.arange(t) // MOBA_BLOCK
    lane_of = np.arange(ATT_HEADS)[:, None, None] * SUBLANES + key_blk[None, :, None]
    kbias = jnp.asarray(np.where(lane_of == np.arange(LANES)[None, None, :], MASK_BIAS, 0.0), dtype=BF16)
    return pl.pallas_call(
        functools.partial(_moba_prompt_kernel, n_blocks=n_blocks),
        grid=(bsz, n_blocks),
        in_specs=[
            pl.BlockSpec((MOBA_BLOCK, ATT_WIDTH), lambda b, i: (b * n_blocks + i, 0)),
            pl.BlockSpec((None, t * ATT_HEADS, dh), lambda b, i: (l, b, 0)),
            pl.BlockSpec((None, t * ATT_HEADS, dh), lambda b, i: (l, b, 0)),
            pl.BlockSpec((ATT_HEADS, t, LANES), lambda b, i: (0, 0, 0)),
        ],
        out_specs=pl.BlockSpec((MOBA_BLOCK, ATT_WIDTH), lambda b, i: (b * n_blocks + i, 0)),
        out_shape=jax.ShapeDtypeStruct((bsz * t, ATT_WIDTH), F32),
        scratch_shapes=[
            pltpu.VMEM((ATT_HEADS, t, 2 * dh), BF16),
            pltpu.VMEM((t, ATT_WIDTH), BF16),
            pltpu.VMEM((ATT_HEADS * SUBLANES, ATT_WIDTH), F32),
        ],
        compiler_params=_params(2),
        name="moba_prompt",
    )(q, k, v, kbias)


def _moba_sample_kernel(pt_ref, q_ref, kn_ref, vn_ref, e_ref, *refs, n_pages, n_seq):
    del pt_ref
    k_refs = refs[:n_seq * n_pages]
    v_refs = refs[n_seq * n_pages:2 * n_seq * n_pages]
    o_ref = refs[2 * n_seq * n_pages]
    rows = q_ref.shape[1]
    pages_per_block = MOBA_BLOCK // PAGE_SIZE
    n_blocks = n_pages // pages_per_block
    dh = ATT_HEAD_DIM
    pairs = [(g, h) for g in range(n_seq) for h in range(ATT_HEADS)]

    def block_rows(page_refs, g, n, h):
        first = g * n_pages + n * pages_per_block
        return jnp.concatenate(
            [page_refs[j][0, pl.ds(h, PAGE_SIZE, stride=ATT_HEADS), :] for j in range(first, first + pages_per_block)],
            axis=0)

    qs, gate_rows, s_rows = [], [], []
    for g, h in pairs:
        q = q_ref[g, :, h * dh:(h + 1) * dh]
        qs.append((q * (dh ** -0.5)).astype(BF16))
        mean_rows, s_parts = [], []
        for n in range(n_blocks):
            kb = block_rows(k_refs, g, n, h)
            mean_rows.append(jnp.sum(kb, axis=0, keepdims=True) / MOBA_BLOCK)
            s_parts.append(lax.dot_general(qs[-1], kb.astype(BF16), _NT, preferred_element_type=F32))
        mean_rows.append(jnp.zeros((LANES - n_blocks, dh), F32))
        gate_rows.append(_dot_nt_split(q, jnp.concatenate(mean_rows, axis=0)))
        s_rows.append(jnp.concatenate(s_parts, axis=1))
    sel = _select_blocks(jnp.concatenate(gate_rows, axis=0), n_blocks)
    sel_keys = jnp.dot(sel.astype(BF16), e_ref[...], preferred_element_type=F32)
    s = jnp.where(sel_keys > 0.5, jnp.concatenate(s_rows, axis=0), NEG_INF)

    pad = jnp.zeros((LANES - rows, dh), F32)
    padded = lambda ref, g, h: jnp.concatenate([ref[g, :, h * dh:(h + 1) * dh], pad], axis=0).astype(BF16)
    s_own = jnp.concatenate(
        [lax.dot_general(qs[i], padded(kn_ref, g, h), _NT, preferred_element_type=F32)
         for i, (g, h) in enumerate(pairs)], axis=0)
    t_row = lax.broadcasted_iota(jnp.int32, s_own.shape, 0) % rows
    key = lax.broadcasted_iota(jnp.int32, s_own.shape, 1)
    s_own = jnp.where(key <= t_row, s_own, NEG_INF)

    m = jnp.maximum(jnp.max(s, axis=-1, keepdims=True), jnp.max(s_own, axis=-1, keepdims=True))
    p = jnp.exp(s - m)
    p_own = jnp.exp(s_own - m)
    inv = 1.0 / (jnp.sum(p, axis=-1, keepdims=True) + jnp.sum(p_own, axis=-1, keepdims=True))
    pb = p.astype(BF16)
    pb_own = p_own.astype(BF16)
    outs = []
    for i, (g, h) in enumerate(pairs):
        r0 = i * rows
        o = jnp.dot(pb_own[r0:r0 + rows], padded(vn_ref, g, h), preferred_element_type=F32)
        for n in range(n_blocks):
            o = o + jnp.dot(pb[r0:r0 + rows, n * MOBA_BLOCK:(n + 1) * MOBA_BLOCK],
                            block_rows(v_refs, g, n, h).astype(BF16), preferred_element_type=F32)
        outs.append(o * inv[r0:r0 + rows])
    for g in range(n_seq):
        o_ref[g] = jnp.concatenate(outs[g * ATT_HEADS:(g + 1) * ATT_HEADS], axis=1)


def _pool_windows_to_lanes(sums, counts, cur):
    lane = lax.broadcasted_iota(jnp.int32, cur.shape, 1)
    d = sums[-1] / counts[-1]
    for g in range(len(POOL_WINDOWS) - 2, -1, -1):
        d = jnp.where(lane < (g + 1) * POOL_GROUP_WIDTH, sums[g] / counts[g], d)
    return d - cur


def _conv_tail(y, ln_g, ln_b, pw_w, pw_b):
    yc = y - jnp.mean(y, axis=-1, keepdims=True)
    yn = yc * lax.rsqrt(jnp.mean(yc * yc, axis=-1, keepdims=True) + EPS) * ln_g + ln_b
    act = yn * jax.nn.sigmoid(yn)
    return jnp.dot(act.astype(BF16), pw_w, preferred_element_type=F32) + pw_b


POOL_HALO = 16
CONV_HALO = 32


def _poolconv_prompt_kernel(x_ref, u_ref, ag_ref, ya_ref, pw_ref, ps_ref, dw_ref, db_ref, lg_ref, lb_ref,
                            cw_ref, cb_ref, wp_ref, wc_ref, wa_ref, o_ref, ht_ref, e_ref, h_ref, hs_ref, *, tt,
                            side_work=None):
    ti = pl.program_id(1)

    @pl.when(ti == 0)
    def _():
        e_ref[0:POOL_HALO, :] = jnp.zeros((POOL_HALO, POOL_WIDTH), F32)
        h_ref[0:CONV_HALO, :] = jnp.zeros((CONV_HALO, CONV_WIDTH), F32)

    if side_work is not None:
        side_work()

    u = u_ref[...]
    e_ref[POOL_HALO:POOL_HALO + tt, :] = u
    a = ag_ref[:, 0:CONV_WIDTH]
    gate = ag_ref[:, CONV_WIDTH:2 * CONV_WIDTH]
    h_ref[CONV_HALO:CONV_HALO + tt, :] = a * jax.nn.sigmoid(gate)

    pos = ti * tt + lax.broadcasted_iota(jnp.int32, (tt, 1), 0)
    sums, counts = [], []
    run = u
    taken = 1
    for win in POOL_WINDOWS:
        while taken < win:
            run = run + e_ref[POOL_HALO - taken:POOL_HALO - taken + tt, :]
            taken += 1
        sums.append(run)
        counts.append(jnp.minimum(pos + 1, win).astype(F32))
    d = _pool_windows_to_lanes(sums, counts, u)
    y_pool = jnp.dot(d.astype(BF16), pw_ref[...], preferred_element_type=F32) * ps_ref[...]

    span = tt + CONV_HALO - SUBLANES
    for s in range(1, SUBLANES):
        hs_ref[s - 1, 0:span, :] = h_ref[s:s + span, :]
    y = jnp.zeros((tt, CONV_WIDTH), F32) + db_ref[...]
    for j in range(CONV_K):
        r0 = CONV_HALO - CONV_BUF + j
        s = r0 % SUBLANES
        rows = h_ref[r0:r0 + tt, :] if s == 0 else hs_ref[s - 1, r0 - s:r0 - s + tt, :]
        y = y + rows * dw_ref[j:j + 1, :]
    y_conv = _conv_tail(y, lg_ref[...], lb_ref[...], cw_ref[...], cb_ref[...])

    out = x_ref[...] + jnp.dot(y_pool.astype(BF16), wp_ref[...], preferred_element_type=F32)
    out = out + jnp.dot(y_conv.astype(BF16), wc_ref[...], preferred_element_type=F32)
    o_ref[...] = out + jnp.dot(ya_ref[...].astype(BF16), wa_ref[...], preferred_element_type=F32)

    ht_ref[0] = h_ref[tt:tt + CONV_HALO, :]
    e_ref[0:POOL_HALO, :] = e_ref[tt:tt + POOL_HALO, :]
    h_ref[0:CONV_HALO, :] = h_ref[tt:tt + CONV_HALO, :]


N_MIX_IN = 15


def _mix_out_moba_kernel(pt_ref, *refs, tt, n_pages, n_seq):
    n_moba_in = 4 + 2 * n_seq * n_pages
    mix_in = refs[:N_MIX_IN]
    moba_in = refs[N_MIX_IN:N_MIX_IN + n_moba_in]
    o_ref, ht_ref, att_ref = refs[N_MIX_IN + n_moba_in:N_MIX_IN + n_moba_in + 3]
    scratch = refs[N_MIX_IN + n_moba_in + 3:]
    side = functools.partial(_moba_sample_kernel, pt_ref, *moba_in, att_ref, n_pages=n_pages, n_seq=n_seq)
    _poolconv_prompt_kernel(*mix_in, o_ref, ht_ref, *scratch, tt=tt, side_work=side)


def _mix_out_prompt_moba_sample(x, u, ag, y_att, bsz, t, weights, w_out, q_s, k_new, v_new, cache_k, cache_v,
                                page_table, l, tt=256):
    pool_w, pool_s, dw_w, dw_b, ln_g, ln_b, pw_w, pw_b = weights
    nt = t // tt
    dbsz, rows, _ = q_s.shape
    n_seq = dbsz // (bsz * nt)
    assert n_seq * bsz * nt == dbsz
    n_pages = page_table.shape[1]
    n_phys = cache_k.shape[0] // DEPTH
    past_len = n_pages * PAGE_SIZE
    assert past_len % MOBA_BLOCK == 0 and past_len // MOBA_BLOCK <= SUBLANES
    base = l * n_phys
    row = lambda b, i, pt: (b * nt + i, 0)
    w_rows = lambda width, blk: pl.BlockSpec((None, width, D_MODEL), lambda b, i, pt: (l, blk, 0))
    tok_spec = pl.BlockSpec((n_seq, rows, ATT_WIDTH), lambda b, i, pt: (b * nt + i, 0, 0))
    page_specs = [
        pl.BlockSpec((1, PAGE_SIZE * ATT_HEADS, ATT_HEAD_DIM),
                     lambda b, i, pt, g=g, j=j: (base + pt[((b * nt + i) * n_seq + g) * n_pages + j], 0, 0))
        for g in range(n_seq) for j in range(n_pages)
    ]
    grid_spec = pltpu.PrefetchScalarGridSpec(
        num_scalar_prefetch=1,
        grid=(bsz, nt),
        in_specs=[
            pl.BlockSpec((tt, D_MODEL), row),
            pl.BlockSpec((tt, POOL_WIDTH), row),
            pl.BlockSpec((tt, 2 * CONV_WIDTH), row),
            pl.BlockSpec((tt, ATT_WIDTH), row),
            _layer_spec((POOL_WIDTH, POOL_WIDTH), l, 3),
            _layer_spec((1, POOL_WIDTH), l, 3),
            _layer_spec((CONV_HALO, CONV_WIDTH), l, 3),
            _layer_spec((1, CONV_WIDTH), l, 3),
            _layer_spec((1, CONV_WIDTH), l, 3),
            _layer_spec((1, CONV_WIDTH), l, 3),
            _layer_spec((CONV_WIDTH, CONV_WIDTH), l, 3),
            _layer_spec((1, CONV_WIDTH), l, 3),
            w_rows(POOL_WIDTH, 0),
            w_rows(CONV_WIDTH, POOL_WIDTH // CONV_WIDTH),
            w_rows(ATT_WIDTH, (POOL_WIDTH + CONV_WIDTH) // ATT_WIDTH),
            tok_spec, tok_spec, tok_spec,
            pl.BlockSpec((LANES, past_len), lambda b, i, pt: (0, 0)),
        ] + page_specs + page_specs,
        out_specs=[
            pl.BlockSpec((tt, D_MODEL), row),
            pl.BlockSpec((1, CONV_HALO, CONV_WIDTH), lambda b, i, pt: (b, 0, 0)),
            tok_spec,
        ],
        scratch_shapes=[
            pltpu.VMEM((POOL_HALO + tt, POOL_WIDTH), F32),
            pltpu.VMEM((CONV_HALO + tt, CONV_WIDTH), F32),
            pltpu.VMEM((SUBLANES - 1, CONV_HALO + tt, CONV_WIDTH), F32),
        ],
    )
    return pl.pallas_call(
        functools.partial(_mix_out_moba_kernel, tt=tt, n_pages=n_pages, n_seq=n_seq),
        grid_spec=grid_spec,
        out_shape=[
            jax.ShapeDtypeStruct((bsz * t, D_MODEL), F32),
            jax.ShapeDtypeStruct((bsz, CONV_HALO, CONV_WIDTH), F32),
            jax.ShapeDtypeStruct((dbsz, rows, ATT_WIDTH), F32),
        ],
        compiler_params=_params(2),
        name="mix_out_moba",
    )(page_table.reshape(-1), x, u, ag, y_att, pool_w, pool_s, dw_w, dw_b, ln_g, ln_b, pw_w, pw_b,
      w_out, w_out, w_out, q_s, k_new, v_new, _block_expander(past_len),
      *([cache_k] * (n_seq * n_pages)), *([cache_v] * (n_seq * n_pages)))


def _poolconv_sample_kernel(sp_ref, u_ref, sc_ref, ag_ref, pw_ref, ps_ref, dw_ref, db_ref, lg_ref, lb_ref,
                            cw_ref, cb_ref, yp_ref, yc_ref, h_ref, *, n_new, pos0):
    def pool_row(r):
        return sp_ref[r] if r < POOL_BUF else u_ref[r - POOL_BUF]

    for t in range(n_new):
        h_ref[t] = ag_ref[t, :, 0:CONV_WIDTH] * jax.nn.sigmoid(ag_ref[t, :, CONV_WIDTH:2 * CONV_WIDTH])

    def conv_row(r):
        return sc_ref[r] if r < CONV_BUF else h_ref[r - CONV_BUF]

    for t in range(n_new):
        cur = u_ref[t]
        sums, counts = [], []
        run = cur
        taken = 1
        for win in POOL_WINDOWS:
            while taken < win:
                run = run + pool_row(POOL_BUF + t - taken)
                taken += 1
            sums.append(run)
            counts.append(float(min(pos0 + t + 1, win)))
        d = _pool_windows_to_lanes(sums, counts, cur)
        yp_ref[t] = jnp.dot(d.astype(BF16), pw_ref[...], preferred_element_type=F32) * ps_ref[...]

        y = jnp.zeros(cur.shape, F32) + db_ref[...]
        for j in range(CONV_K):
            y = y + conv_row(t + j) * dw_ref[j:j + 1, :]
        yc_ref[t] = _conv_tail(y, lg_ref[...], lb_ref[...], cw_ref[...], cb_ref[...])


def _poolconv_sample(state_pool_t, u_t, state_conv_t, ag_t, pos0, weights, l):
    pool_w, pool_s, dw_w, dw_b, ln_g, ln_b, pw_w, pw_b = weights
    n_new, bsz, _ = u_t.shape
    full = lambda a: pl.BlockSpec(a.shape, lambda i: (0,) * a.ndim)
    out = jax.ShapeDtypeStruct((n_new, bsz, CONV_WIDTH), F32)
    return pl.pallas_call(
        functools.partial(_poolconv_sample_kernel, n_new=n_new, pos0=pos0),
        grid=(1,),
        in_specs=[
            full(state_pool_t), full(u_t), full(state_conv_t), full(ag_t),
            _layer_spec((POOL_WIDTH, POOL_WIDTH), l, 1),
            _layer_spec((1, POOL_WIDTH), l, 1),
            _layer_spec((CONV_HALO, CONV_WIDTH), l, 1),
            _layer_spec((1, CONV_WIDTH), l, 1),
            _layer_spec((1, CONV_WIDTH), l, 1),
            _layer_spec((1, CONV_WIDTH), l, 1),
            _layer_spec((CONV_WIDTH, CONV_WIDTH), l, 1),
            _layer_spec((1, CONV_WIDTH), l, 1),
        ],
        out_specs=[pl.BlockSpec(out.shape, lambda i: (0, 0, 0))] * 3,
        out_shape=[out, out, out],
        compiler_params=_params(1),
        name="poolconv_sample",
    )(state_pool_t, u_t, state_conv_t, ag_t, pool_w, pool_s, dw_w, dw_b, ln_g, ln_b, pw_w, pw_b)


def _cross_kernel(x_ref, g_ref, wq_ref, hg_ref, k_ref, v_ref, wo_ref, o_ref, *, n_seq, interleaved):
    hd = MEM_HEAD_DIM
    rows = x_ref.shape[1]
    x = x_ref[...].reshape(n_seq * rows, D_MODEL)
    xn = _rms(x, g_ref[...]).astype(BF16)
    q_all = jnp.dot(xn, wq_ref[...], preferred_element_type=F32)
    halves = range(hd // LANES)

    def head_slab(ref, g, h):
        if not interleaved:
            return ref[g, :, h * hd:(h + 1) * hd]
        return jnp.concatenate(
            [ref[g, pl.ds(half * MEM_HEADS + h, N_MEM, stride=len(halves) * MEM_HEADS), :] for half in halves],
            axis=1)

    pairs = [(g, h) for g in range(n_seq) for h in range(MEM_HEADS)]
    scores = []
    for g, h in pairs:
        q = _rms(q_all[g * rows:(g + 1) * rows, h * hd:(h + 1) * hd], hg_ref[...])
        q = (q * (hd ** -0.5)).astype(BF16)
        scores.append(lax.dot_general(q, head_slab(k_ref, g, h).astype(BF16), _NT, preferred_element_type=F32))
    s = jnp.concatenate(scores, axis=0)
    p = jnp.exp(s - jnp.max(s, axis=-1, keepdims=True))
    inv = 1.0 / jnp.sum(p, axis=-1, keepdims=True)
    o_seqs = []
    for g in range(n_seq):
        o_heads = []
        for h in range(MEM_HEADS):
            r0 = (g * MEM_HEADS + h) * rows
            o = jnp.dot(p[r0:r0 + rows].astype(BF16), head_slab(v_ref, g, h).astype(BF16),
                        preferred_element_type=F32)
            o_heads.append((o * inv[r0:r0 + rows]).astype(BF16))
        o_seqs.append(jnp.concatenate(o_heads, axis=1))
    o_all = o_seqs[0] if n_seq == 1 else jnp.concatenate(o_seqs, axis=0)
    y = x + jnp.dot(o_all, wo_ref[...], preferred_element_type=F32)
    o_ref[...] = y.reshape(n_seq, rows, D_MODEL)


def _interleaved_memory(cache):
    depth, s, n_mem, heads, hd = cache.shape
    halves = hd // LANES
    c = cache.reshape(depth, s, n_mem, heads, halves, LANES).transpose(0, 1, 2, 4, 3, 5)
    return c.reshape(depth * s, n_mem * halves * heads, LANES)


def _cross(x, ln, w_q, head_gain, mem_k, mem_v, w_o, l, n_seq, tq, seq0=None):
    s_total, r, _ = x.shape
    interleaved = seq0 is not None
    if interleaved:
        assert seq0 % n_seq == 0
        kv_spec = pl.BlockSpec((n_seq,) + mem_k.shape[1:], lambda s, i: (seq0 // n_seq + s, 0, 0))
    else:
        kv_spec = pl.BlockSpec((n_seq, N_MEM, D_MODEL), lambda s, i: (s, 0, 0))
    x_spec = pl.BlockSpec((n_seq, tq, D_MODEL), lambda s, i: (s, i, 0))
    return pl.pallas_call(
        functools.partial(_cross_kernel, n_seq=n_seq, interleaved=interleaved),
        grid=(s_total // n_seq, r // tq),
        in_specs=[x_spec, _layer_spec((1, D_MODEL), l, 2), _layer_spec((D_MODEL, D_MODEL), l, 2),
                  _layer_spec((1, MEM_HEAD_DIM), l, 2), kv_spec, kv_spec,
                  _layer_spec((D_MODEL, D_MODEL), l, 2)],
        out_specs=x_spec,
        out_shape=jax.ShapeDtypeStruct(x.shape, F32),
        compiler_params=_params(2),
        name="cross",
    )(x, ln, w_q, head_gain, mem_k, mem_v, w_o)


def _pad_rows(x, rows):
    return jnp.pad(x, ((0, 0), (0, rows - x.shape[1]), (0, 0)))


def kernel(x_prompt, x_sample, cache_att_k, cache_att_v, cache_mem_k, cache_mem_v, state_pool, state_conv, page_table, mem_prompt, ln_ffn1, w_ffn1_in, w_ffn1_out, ln_mix, w_in, pool_w, pool_scale, conv_dw_w, conv_dw_b, conv_ln_g, conv_ln_b, conv_pw_w, conv_pw_b, att_q_norm, att_k_norm, w_out, ln_cross, ln_mem, w_cq, w_ckv, cq_norm, ck_norm, w_co, ln_ffn2, w_ffn2_in, w_ffn2_out):
    bsz, seq, _ = x_prompt.shape
    dbsz, dseq, _ = x_sample.shape
    n_pages = page_table.shape[1]
    past_len = n_pages * PAGE_SIZE
    n_phys = cache_att_k.shape[1]

    bf = lambda w: w.astype(BF16)
    rowv = lambda v: v[:, None, :]
    w1i, w1o, w2i, w2o = bf(w_ffn1_in), bf(w_ffn1_out), bf(w_ffn2_in), bf(w_ffn2_out)
    w_in_b, w_out_b, w_cq_b, w_ckv_b, w_co_b = bf(w_in), bf(w_out), bf(w_cq), bf(w_ckv), bf(w_co)
    groups = len(POOL_WINDOWS)
    eye = jnp.eye(groups, dtype=F32)
    pool_bd = bf((pool_w[:, :, :, None, :] * eye[None, :, None, :, None])
                 .reshape(DEPTH, POOL_WIDTH, POOL_WIDTH))
    dw_pad = jnp.pad(conv_dw_w, ((0, 0), (0, CONV_HALO - CONV_K), (0, 0)))
    mixer_w = (pool_bd, rowv(pool_scale), dw_pad, rowv(conv_dw_b), rowv(conv_ln_g), rowv(conv_ln_b),
               bf(conv_pw_w), rowv(conv_pw_b))
    ln1, lnm, lnc, lnmem, ln2 = rowv(ln_ffn1), rowv(ln_mix), rowv(ln_cross), rowv(ln_mem), rowv(ln_ffn2)

    tables_p = _rope_tables(jnp.arange(seq, dtype=jnp.int32))
    tm_s = dbsz * dseq
    tables_s = _rope_tables(past_len + jnp.arange(tm_s, dtype=jnp.int32) % dseq)

    c1 = POOL_WIDTH
    c3 = c1 + 2 * CONV_WIDTH
    c4 = c3 + ATT_WIDTH
    c5 = c4 + ATT_WIDTH
    mix_segs = [(c3, ATT_WIDTH, ATT_HEAD_DIM, 0, True, False), (c4, ATT_WIDTH, ATT_HEAD_DIM, 1, True, True),
                (c5, ATT_WIDTH, ATT_HEAD_DIM, None, False, True),
                (0, c1, None, None, False, False), (c1, 2 * CONV_WIDTH, None, None, False, False)]
    mem_segs = [(0, D_MODEL, MEM_HEAD_DIM, 0, False, True), (D_MODEL, D_MODEL, MEM_HEAD_DIM, None, False, True)]
    out_rows = (0, c1, c1 + CONV_WIDTH)

    cache_k = cache_att_k.reshape(DEPTH * n_phys, PAGE_SIZE * ATT_HEADS, ATT_HEAD_DIM)
    cache_v = cache_att_v.reshape(DEPTH * n_phys, PAGE_SIZE * ATT_HEADS, ATT_HEAD_DIM)
    mem_k_s = _interleaved_memory(cache_mem_k)
    mem_v_s = _interleaved_memory(cache_mem_v)
    mem_rows = mem_prompt.reshape(bsz * N_MEM, D_MODEL)

    xp = x_prompt.reshape(bsz * seq, D_MODEL)
    xs = x_sample.reshape(dbsz * dseq, D_MODEL)
    rows_pad = SUBLANES
    outs = {name: [] for name in ("p_pool", "p_conv", "s_pool", "s_conv")}
    cq_gain = rowv(cq_norm)
    kv_p = kv_s = mem_p = None

    for l in range(DEPTH):
        att_gains = jnp.stack([att_q_norm[l], att_k_norm[l]])

        mem_p = _norm_proj(mem_rows, lnmem, w_ckv_b, l, ck_norm[l][None], mem_segs, stacked=mem_p)
        mk, mv = (a.reshape(DEPTH * bsz, -1, LANES) for a in mem_p)
        xp, xs = _ffn(xp, xs, ln1, w1i, w1o, l)
        q, k, v, u, ag = _norm_proj(xp, lnm, w_in_b, l, att_gains, mix_segs, tables=tables_p, stacked=kv_p,
                                    tm=256)
        kv_p = (k, v)
        y_att = _moba_prompt(q, k, v, bsz, seq, l)
        p_pool = u.reshape(bsz, seq, POOL_WIDTH)[:, seq - POOL_BUF:]

        q_s, k_s, v_s, u_s, ag_s = _norm_proj(xs, lnm, w_in_b, l, att_gains, mix_segs, tables=tables_s, tm=tm_s,
                                              stacked=kv_s)
        kv_s = (k_s, v_s)
        pad3 = lambda a: _pad_rows(a.reshape(dbsz, dseq, ATT_WIDTH), 2 * SUBLANES)
        xp, h_tail, y_att_s = _mix_out_prompt_moba_sample(
            xp, u, ag, y_att, bsz, seq, mixer_w, w_out_b, pad3(q_s), pad3(k_s[l]), pad3(v_s[l]),
            cache_k, cache_v, page_table, l)
        xp = _cross(xp.reshape(bsz, seq, D_MODEL), lnc, w_cq_b, cq_gain, mk, mv, w_co_b, l, 1, 1024,
                    seq0=l * bsz).reshape(bsz * seq, D_MODEL)
        outs["p_pool"].append(p_pool)
        outs["p_conv"].append(h_tail[:, CONV_HALO - CONV_BUF:])

        y_att = y_att_s[:, :dseq].reshape(dbsz * dseq, ATT_WIDTH)
        tmaj = lambda a: jnp.swapaxes(a, 0, 1)
        u3 = u_s.reshape(dbsz, dseq, POOL_WIDTH)
        y_pool, y_conv, h_new = _poolconv_sample(
            tmaj(state_pool[l]), tmaj(u3), tmaj(state_conv[l]), tmaj(ag_s.reshape(dbsz, dseq, 2 * CONV_WIDTH)),
            past_len, mixer_w, l)
        bmaj = lambda a: jnp.swapaxes(a, 0, 1).reshape(dbsz * dseq, a.shape[-1])
        xs = _out_proj(xs, [bmaj(y_pool), bmaj(y_conv), y_att], w_out_b, l, out_rows, tm=tm_s)
        n_seq = 4
        xs = _cross(_pad_rows(xs.reshape(dbsz, dseq, D_MODEL), rows_pad), lnc, w_cq_b, cq_gain, mem_k_s, mem_v_s,
                    w_co_b, l, n_seq, rows_pad, seq0=l * dbsz)[:, :dseq].reshape(dbsz * dseq, D_MODEL)
        xp, xs = _ffn(xp, xs, ln2, w2i, w2o, l)
        outs["s_pool"].append(jnp.concatenate([state_pool[l][:, dseq:], u3], axis=1))
        outs["s_conv"].append(jnp.concatenate([state_conv[l][:, dseq:], jnp.swapaxes(h_new, 0, 1)], axis=1))

    st = lambda name: jnp.stack(outs[name])
    heads5 = lambda a, b, t: a.reshape(DEPTH, b, t, ATT_HEADS, ATT_HEAD_DIM)
    tiles = MEM_HEAD_DIM // LANES
    mem5 = lambda a: (a.reshape(DEPTH, bsz, N_MEM, tiles, MEM_HEADS, LANES).transpose(0, 1, 2, 4, 3, 5)
                      .reshape(DEPTH, bsz, N_MEM, MEM_HEADS, MEM_HEAD_DIM))
    return (xp.reshape(bsz, seq, D_MODEL), xs.reshape(dbsz, dseq, D_MODEL),
            heads5(kv_p[0], bsz, seq), heads5(kv_p[1], bsz, seq), mem5(mem_p[0]), mem5(mem_p[1]), st("p_pool"),
            st("p_conv"), heads5(kv_s[0], dbsz, dseq), heads5(kv_s[1], dbsz, dseq), st("s_pool"), st("s_conv"))
```

```python
import functools

import jax
import jax.numpy as jnp
import numpy as np
from jax import lax
from jax.experimental import pallas as pl
from jax.experimental.pallas import tpu as pltpu

D_MODEL = 1024
DEPTH = 2
PAGE_SIZE = 128

POOL_WIDTH = D_MODEL // 4
POOL_WINDOWS = (2, 4, 8, 16)
POOL_GROUP_WIDTH = POOL_WIDTH // len(POOL_WINDOWS)
POOL_BUF = max(POOL_WINDOWS) - 1

CONV_WIDTH = D_MODEL // 4
CONV_K = 31
CONV_BUF = CONV_K - 1

ATT_HEADS = 4
ATT_HEAD_DIM = D_MODEL // 8
ATT_WIDTH = ATT_HEADS * ATT_HEAD_DIM
ROT_DIM = ATT_HEAD_DIM // 4
ROPE_THETA = 500000.0
MOBA_BLOCK = 256
MOBA_TOPK = 3

N_MEM = 256
MEM_HEADS = 4
MEM_HEAD_DIM = D_MODEL // MEM_HEADS

D_FF = ((8 * D_MODEL // 3 + 127) // 128) * 128
EPS = 1e-6

LANES = 128
SUBLANES = 8
VMEM_LIMIT_BYTES = 56 * 1024 * 1024

F32 = jnp.float32
BF16 = jnp.bfloat16
NEG_INF = float("-inf")

_NT = (((1,), (1,)), ((), ()))


def _params(n_axes):
    return pltpu.CompilerParams(dimension_semantics=("arbitrary",) * n_axes,
                                vmem_limit_bytes=VMEM_LIMIT_BYTES)


def _rms(x, g):
    return x * lax.rsqrt(jnp.mean(x * x, axis=-1, keepdims=True) + EPS) * g


def _layer_spec(shape, l, n_grid):
    zeros = (0,) * len(shape)
    if n_grid == 1:
        return pl.BlockSpec((None,) + tuple(shape), lambda i: (l,) + zeros)
    if n_grid == 2:
        return pl.BlockSpec((None,) + tuple(shape), lambda i, j: (l,) + zeros)
    return pl.BlockSpec((None,) + tuple(shape), lambda i, j, k: (l,) + zeros)


FFN_CHUNK = 256


def _ffn_kernel(xa_ref, xb_ref, g_ref, wi_ref, wo_ref, oa_ref, ob_ref):
    i = pl.program_id(0)
    x = jnp.where(i == 0, xb_ref[...], xa_ref[...])
    xn = _rms(x, g_ref[...]).astype(BF16)
    acc = None
    for c in range(D_FF // FFN_CHUNK):
        lo = c * FFN_CHUNK
        gate = jnp.dot(xn, wi_ref[:, lo:lo + FFN_CHUNK], preferred_element_type=F32)
        up = jnp.dot(xn, wi_ref[:, D_FF + lo:D_FF + lo + FFN_CHUNK], preferred_element_type=F32)
        act = (gate * jax.nn.sigmoid(gate) * up).astype(BF16)
        part = jnp.dot(act, wo_ref[lo:lo + FFN_CHUNK, :], preferred_element_type=F32)
        acc = part if acc is None else acc + part
    oa_ref[...] = x + 0.5 * acc

    @pl.when(i == 0)
    def _():
        ob_ref[...] = oa_ref[...]


def _ffn(xa, xb, ln, w_in, w_out, l, tm=512):
    n_a = xa.shape[0] // tm
    assert xa.shape[0] == n_a * tm and xb.shape[0] == tm
    a_spec = pl.BlockSpec((tm, D_MODEL), lambda i: (jnp.maximum(i - 1, 0), 0))
    b_spec = pl.BlockSpec((tm, D_MODEL), lambda i: (0, 0))
    return pl.pallas_call(
        _ffn_kernel,
        grid=(n_a + 1,),
        in_specs=[
            a_spec, b_spec,
            _layer_spec((1, D_MODEL), l, 1),
            _layer_spec((D_MODEL, 2 * D_FF), l, 1),
            _layer_spec((D_FF, D_MODEL), l, 1),
        ],
        out_specs=[a_spec, b_spec],
        out_shape=[jax.ShapeDtypeStruct(xa.shape, F32), jax.ShapeDtypeStruct(xb.shape, F32)],
        compiler_params=_params(1),
        name="ffn",
    )(xa, xb, ln, w_in, w_out)


def _norm_proj_kernel(*refs, segs, rope):
    x_ref, g_ref, w_ref, hg_ref = refs[:4]
    if rope:
        cos_ref, sin_lo_ref, sin_hi_ref = refs[4:7]
    out_refs = refs[len(refs) - len(segs):]
    tm = x_ref.shape[0]
    xn = _rms(x_ref[...], g_ref[...]).astype(BF16)
    for (c0, width, head_dim, gain_row, use_rope, head_rows), o_ref in zip(segs, out_refs):
        z = jnp.dot(xn, w_ref[:, c0:c0 + width], preferred_element_type=F32)
        if head_dim is None:
            o_ref[...] = z
            continue
        n_heads = width // head_dim
        for h in range(n_heads):
            zh = z[:, h * head_dim:(h + 1) * head_dim]
            if gain_row is not None:
                zh = _rms(zh, hg_ref[gain_row:gain_row + 1, :])
            if use_rope:
                half = ROT_DIM // 2
                zh = (zh * cos_ref[...]
                      + pltpu.roll(zh, half, 1) * sin_hi_ref[...]
                      + pltpu.roll(zh, LANES - half, 1) * sin_lo_ref[...])
            if not head_rows:
                o_ref[:, h * head_dim:(h + 1) * head_dim] = zh
                continue
            tiles = head_dim // LANES
            for tile in range(tiles):
                rows = pl.ds(tile * n_heads + h, tm, stride=n_heads * tiles)
                piece = zh[:, tile * LANES:(tile + 1) * LANES]
                if len(o_ref.shape) == 2:
                    o_ref[rows, :] = piece
                else:
                    for d in range(o_ref.shape[0]):
                        o_ref[d, rows, :] = piece


def _norm_proj(x, ln, w, l, head_gains, segs, tables=None, tm=512, stacked=None):
    m = x.shape[0]
    n_cols = w.shape[-1]
    rope = tables is not None
    in_specs = [
        pl.BlockSpec((tm, D_MODEL), lambda i: (i, 0)),
        _layer_spec((1, D_MODEL), l, 1),
        _layer_spec((D_MODEL, n_cols), l, 1),
        pl.BlockSpec(head_gains.shape, lambda i: (0, 0)),
    ]
    args = [x, ln, w, head_gains]
    if rope:
        period = tables[0].shape[0] // tm
        for t in tables:
            in_specs.append(pl.BlockSpec((tm, LANES), lambda i: (i % period, 0)))
            args.append(t)
    out_specs, out_shape, aliases = [], [], {}
    for k, s in enumerate(segs):
        width, head_dim, head_rows = s[1], s[2], s[5]
        if not head_rows:
            out_specs.append(pl.BlockSpec((tm, width), lambda i: (i, 0)))
            out_shape.append(jax.ShapeDtypeStruct((m, width), F32))
            continue
        per_token = width // LANES
        out_shape.append(jax.ShapeDtypeStruct((DEPTH, m * per_token, LANES), F32))
        if stacked is None:
            out_specs.append(pl.BlockSpec((DEPTH, tm * per_token, LANES), lambda i: (0, i, 0)))
        else:
            out_specs.append(pl.BlockSpec((None, tm * per_token, LANES), lambda i: (l, i, 0)))
            aliases[len(args)] = k
            in_specs.append(pl.BlockSpec(memory_space=pl.ANY))
            args.append(stacked[len(aliases) - 1])
    return pl.pallas_call(
        functools.partial(_norm_proj_kernel, segs=tuple(segs), rope=rope),
        grid=(m // tm,),
        in_specs=in_specs,
        out_specs=out_specs,
        out_shape=out_shape,
        input_output_aliases=aliases,
        compiler_params=_params(1),
        name="norm_proj",
    )(*args)


def _rope_tables(positions):
    half = ROT_DIM // 2
    inv_freq = jnp.power(jnp.float32(ROPE_THETA), -jnp.arange(half, dtype=F32) / half)
    ang = positions.astype(F32)[:, None] * inv_freq[None, :]
    cos, sin = jnp.cos(ang), jnp.sin(ang)
    n = positions.shape[0]
    rest = ATT_HEAD_DIM - ROT_DIM
    cos_t = jnp.concatenate([cos, cos, jnp.ones((n, rest), F32)], axis=1)
    sin_lo = jnp.concatenate([-sin, jnp.zeros((n, half + rest), F32)], axis=1)
    sin_hi = jnp.concatenate([jnp.zeros((n, half), F32), sin, jnp.zeros((n, rest), F32)], axis=1)
    return cos_t, sin_lo, sin_hi


def _out_proj_kernel(*refs, n_terms):
    x_ref = refs[0]
    y_refs = refs[1:1 + n_terms]
    w_refs = refs[1 + n_terms:1 + 2 * n_terms]
    o_ref = refs[1 + 2 * n_terms]
    acc = x_ref[...]
    for y_ref, w_ref in zip(y_refs, w_refs):
        acc = acc + jnp.dot(y_ref[...].astype(BF16), w_ref[...], preferred_element_type=F32)
    o_ref[...] = acc


def _out_proj(x, ys, w, l, row_starts, tm=512):
    m = x.shape[0]
    in_specs = [pl.BlockSpec((tm, D_MODEL), lambda i: (i, 0))]
    for y in ys:
        in_specs.append(pl.BlockSpec((tm, y.shape[1]), lambda i: (i, 0)))
    for y, r0 in zip(ys, row_starts):
        width = y.shape[1]
        blk = r0 // width
        in_specs.append(pl.BlockSpec((None, width, D_MODEL), lambda i, blk=blk: (l, blk, 0)))
    return pl.pallas_call(
        functools.partial(_out_proj_kernel, n_terms=len(ys)),
        grid=(m // tm,),
        in_specs=in_specs,
        out_specs=pl.BlockSpec((tm, D_MODEL), lambda i: (i, 0)),
        out_shape=jax.ShapeDtypeStruct((m, D_MODEL), F32),
        compiler_params=_params(1),
        name="out_proj",
    )(x, *ys, *([w] * len(ys)))


def _select_blocks(gate, n_allowed):
    lane = lax.broadcasted_iota(jnp.int32, gate.shape, 1)
    allowed = lane < n_allowed
    g = jnp.where(allowed, gate, NEG_INF)
    rank = jnp.zeros(gate.shape, F32)
    for r in range(1, SUBLANES):
        lower = pltpu.roll(g, r, 1)
        higher = pltpu.roll(g, LANES - r, 1)
        rank = rank + jnp.where(lower >= g, 1.0, 0.0) + jnp.where(higher > g, 1.0, 0.0)
    return jnp.where(allowed, jnp.where(rank < MOBA_TOPK, 1.0, 0.0), 0.0)


def _dot_nt_split(a, b):
    a_hi = a.astype(BF16)
    a_lo = (a - a_hi.astype(F32)).astype(BF16)
    b_hi = b.astype(BF16)
    b_lo = (b - b_hi.astype(F32)).astype(BF16)
    m = a.shape[0]
    by_hi = lax.dot_general(jnp.concatenate([a_hi, a_lo], axis=0), b_hi, _NT, preferred_element_type=F32)
    return by_hi[0:m] + by_hi[m:2 * m] + lax.dot_general(a_hi, b_lo, _NT, preferred_element_type=F32)


def _block_expander(n_keys):
    blk = np.arange(n_keys) // MOBA_BLOCK
    return jnp.asarray((np.arange(LANES)[:, None] == blk[None, :]).astype(np.float32), dtype=BF16)


MASK_BIAS = -1e30


def _select_blocks_t(gate_t, n_allowed):
    blk = lax.broadcasted_iota(jnp.int32, gate_t.shape, 0)
    allowed = blk < n_allowed
    g = jnp.where(allowed, gate_t, NEG_INF)
    rank = jnp.zeros(gate_t.shape, F32)
    for m in range(SUBLANES):
        gm = g[m:m + 1, :]
        tie = jnp.where(gm == g, jnp.where(blk > m, 1.0, 0.0), 0.0)
        rank = rank + jnp.where(gm > g, 1.0, tie)
    return jnp.where(allowed, jnp.where(rank < MOBA_TOPK, 1.0, 0.0), 0.0)


def _moba_prompt_kernel(q_ref, k_ref, v_ref, kbias_ref, o_ref, ka_ref, vb_ref, km_ref, *, n_blocks):
    qi = pl.program_id(1)
    blk = MOBA_BLOCK
    dh = ATT_HEAD_DIM
    heads = range(ATT_HEADS)

    @pl.when(qi == 0)
    def _():
        km_ref[...] = jnp.zeros(km_ref.shape, F32)
        for h in heads:
            vb_ref[:, h * dh:(h + 1) * dh] = v_ref[pl.ds(h, n_blocks * blk, stride=ATT_HEADS), :].astype(BF16)
            ka_ref[h, :, dh:2 * dh] = kbias_ref[h]
            for n in range(n_blocks):
                kb = k_ref[pl.ds(n * blk * ATT_HEADS + h, blk, stride=ATT_HEADS), :]
                ka_ref[h, n * blk:(n + 1) * blk, 0:dh] = kb.astype(BF16)
                r = h * SUBLANES + n
                km_ref[r:r + 1, h * dh:(h + 1) * dh] = jnp.mean(kb, axis=0, keepdims=True)

    row = lax.broadcasted_iota(jnp.int32, (blk, blk), 0)
    col = lax.broadcasted_iota(jnp.int32, (blk, blk), 1)

    def step(c):
        n_keys = (c + 1) * blk
        gated = c > MOBA_TOPK
        if gated:
            gate_all = _dot_nt_split(km_ref[...], q_ref[...])
            n_rows = ATT_HEADS * SUBLANES
            flags = []
            for h in heads:
                gate_t = gate_all[h * SUBLANES:(h + 1) * SUBLANES]
                blk_id = lax.broadcasted_iota(jnp.int32, gate_t.shape, 0)
                flags.append(jnp.where(blk_id == c, 0.0, 1.0 - _select_blocks_t(gate_t, c)))
            flags.append(jnp.zeros((LANES - n_rows, blk), F32))
            unchosen = jnp.concatenate(flags, axis=0).T.astype(BF16)
        for h in heads:
            q = q_ref[:, h * dh:(h + 1) * dh]
            qs = (q * (dh ** -0.5)).astype(BF16)
            if gated:
                qa = jnp.concatenate([qs, unchosen], axis=1)
                s = lax.dot_general(qa, ka_ref[h, 0:n_keys, :], _NT, preferred_element_type=F32)
            else:
                s = lax.dot_general(qs, ka_ref[h, 0:n_keys, 0:dh], _NT, preferred_element_type=F32)
            parts = [s[:, 0:c * blk]] if c else []
            parts.append(jnp.where(row >= col, s[:, c * blk:n_keys], NEG_INF))
            s = jnp.concatenate(parts, axis=1)
            p = jnp.exp(s - jnp.max(s, axis=-1, keepdims=True))
            inv = 1.0 / jnp.sum(p, axis=-1, keepdims=True)
            o = jnp.dot(p.astype(BF16), vb_ref[0:n_keys, h * dh:(h + 1) * dh], preferred_element_type=F32)
            o_ref[:, h * dh:(h + 1) * dh] = o * inv

    for c in range(n_blocks):
        pl.when(qi == c)(functools.partial(step, c))


def _moba_prompt(q, k, v, bsz, t, l):
    assert t % MOBA_BLOCK == 0 and t // MOBA_BLOCK <= SUBLANES
    n_blocks = t // MOBA_BLOCK
    dh = ATT_HEAD_DIM
    key_blk = np.arange(t) // MOBA_BLOCK
    lane_of = np.arange(ATT_HEADS)[:, None, None] * SUBLANES + key_blk[None, :, None]
    kbias = jnp.asarray(np.where(lane_of == np.arange(LANES)[None, None, :], MASK_BIAS, 0.0), dtype=BF16)
    return pl.pallas_call(
        functools.partial(_moba_prompt_kernel, n_blocks=n_blocks),
        grid=(bsz, n_blocks),
        in_specs=[
            pl.BlockSpec((MOBA_BLOCK, ATT_WIDTH), lambda b, i: (b * n_blocks + i, 0)),
            pl.BlockSpec((None, t * ATT_HEADS, dh), lambda b, i: (l, b, 0)),
            pl.BlockSpec((None, t * ATT_HEADS, dh), lambda b, i: (l, b, 0)),
            pl.BlockSpec((ATT_HEADS, t, LANES), lambda b, i: (0, 0, 0)),
        ],
        out_specs=pl.BlockSpec((MOBA_BLOCK, ATT_WIDTH), lambda b, i: (b * n_blocks + i, 0)),
        out_shape=jax.ShapeDtypeStruct((bsz * t, ATT_WIDTH), F32),
        scratch_shapes=[
            pltpu.VMEM((ATT_HEADS, t, 2 * dh), BF16),
            pltpu.VMEM((t, ATT_WIDTH), BF16),
            pltpu.VMEM((ATT_HEADS * SUBLANES, ATT_WIDTH), F32),
        ],
        compiler_params=_params(2),
        name="moba_prompt",
    )(q, k, v, kbias)


def _moba_sample_kernel(pt_ref, q_ref, kn_ref, vn_ref, e_ref, *refs, n_pages, n_seq):
    del pt_ref
    k_refs = refs[:n_seq * n_pages]
    v_refs = refs[n_seq * n_pages:2 * n_seq * n_pages]
    o_ref = refs[2 * n_seq * n_pages]
    rows = q_ref.shape[1]
    pages_per_block = MOBA_BLOCK // PAGE_SIZE
    n_blocks = n_pages // pages_per_block
    dh = ATT_HEAD_DIM
    pairs = [(g, h) for g in range(n_seq) for h in range(ATT_HEADS)]

    def block_rows(page_refs, g, n, h):
        first = g * n_pages + n * pages_per_block
        return jnp.concatenate(
            [page_refs[j][0, pl.ds(h, PAGE_SIZE, stride=ATT_HEADS), :] for j in range(first, first + pages_per_block)],
            axis=0)

    qs, gate_rows, s_rows = [], [], []
    for g, h in pairs:
        q = q_ref[g, :, h * dh:(h + 1) * dh]
        qs.append((q * (dh ** -0.5)).astype(BF16))
        mean_rows, s_parts = [], []
        for n in range(n_blocks):
            kb = block_rows(k_refs, g, n, h)
            mean_rows.append(jnp.sum(kb, axis=0, keepdims=True) / MOBA_BLOCK)
            s_parts.append(lax.dot_general(qs[-1], kb.astype(BF16), _NT, preferred_element_type=F32))
        mean_rows.append(jnp.zeros((LANES - n_blocks, dh), F32))
        gate_rows.append(_dot_nt_split(q, jnp.concatenate(mean_rows, axis=0)))
        s_rows.append(jnp.concatenate(s_parts, axis=1))
    sel = _select_blocks(jnp.concatenate(gate_rows, axis=0), n_blocks)
    sel_keys = jnp.dot(sel.astype(BF16), e_ref[...], preferred_element_type=F32)
    s = jnp.where(sel_keys > 0.5, jnp.concatenate(s_rows, axis=0), NEG_INF)

    pad = jnp.zeros((LANES - rows, dh), F32)
    padded = lambda ref, g, h: jnp.concatenate([ref[g, :, h * dh:(h + 1) * dh], pad], axis=0).astype(BF16)
    s_own = jnp.concatenate(
        [lax.dot_general(qs[i], padded(kn_ref, g, h), _NT, preferred_element_type=F32)
         for i, (g, h) in enumerate(pairs)], axis=0)
    t_row = lax.broadcasted_iota(jnp.int32, s_own.shape, 0) % rows
    key = lax.broadcasted_iota(jnp.int32, s_own.shape, 1)
    s_own = jnp.where(key <= t_row, s_own, NEG_INF)

    m = jnp.maximum(jnp.max(s, axis=-1, keepdims=True), jnp.max(s_own, axis=-1, keepdims=True))
    p = jnp.exp(s - m)
    p_own = jnp.exp(s_own - m)
    inv = 1.0 / (jnp.sum(p, axis=-1, keepdims=True) + jnp.sum(p_own, axis=-1, keepdims=True))
    pb = p.astype(BF16)
    pb_own = p_own.astype(BF16)
    outs = []
    for i, (g, h) in enumerate(pairs):
        r0 = i * rows
        o = jnp.dot(pb_own[r0:r0 + rows], padded(vn_ref, g, h), preferred_element_type=F32)
        for n in range(n_blocks):
            o = o + jnp.dot(pb[r0:r0 + rows, n * MOBA_BLOCK:(n + 1) * MOBA_BLOCK],
                            block_rows(v_refs, g, n, h).astype(BF16), preferred_element_type=F32)
        outs.append(o * inv[r0:r0 + rows])
    for g in range(n_seq):
        o_ref[g] = jnp.concatenate(outs[g * ATT_HEADS:(g + 1) * ATT_HEADS], axis=1)


def _pool_windows_to_lanes(sums, counts, cur):
    lane = lax.broadcasted_iota(jnp.int32, cur.shape, 1)
    d = sums[-1] / counts[-1]
    for g in range(len(POOL_WINDOWS) - 2, -1, -1):
        d = jnp.where(lane < (g + 1) * POOL_GROUP_WIDTH, sums[g] / counts[g], d)
    return d - cur


def _conv_tail(y, ln_g, ln_b, pw_w, pw_b):
    yc = y - jnp.mean(y, axis=-1, keepdims=True)
    yn = yc * lax.rsqrt(jnp.mean(yc * yc, axis=-1, keepdims=True) + EPS) * ln_g + ln_b
    act = yn * jax.nn.sigmoid(yn)
    return jnp.dot(act.astype(BF16), pw_w, preferred_element_type=F32) + pw_b


POOL_HALO = 16
CONV_HALO = 32


def _poolconv_prompt_kernel(x_ref, u_ref, ag_ref, ya_ref, pw_ref, ps_ref, dw_ref, db_ref, lg_ref, lb_ref,
                            cw_ref, cb_ref, wp_ref, wc_ref, wa_ref, o_ref, ht_ref, e_ref, h_ref, hs_ref, *, tt,
                            side_work=None):
    ti = pl.program_id(1)

    @pl.when(ti == 0)
    def _():
        e_ref[0:POOL_HALO, :] = jnp.zeros((POOL_HALO, POOL_WIDTH), F32)
        h_ref[0:CONV_HALO, :] = jnp.zeros((CONV_HALO, CONV_WIDTH), F32)

    if side_work is not None:
        side_work()

    u = u_ref[...]
    e_ref[POOL_HALO:POOL_HALO + tt, :] = u
    a = ag_ref[:, 0:CONV_WIDTH]
    gate = ag_ref[:, CONV_WIDTH:2 * CONV_WIDTH]
    h_ref[CONV_HALO:CONV_HALO + tt, :] = a * jax.nn.sigmoid(gate)

    pos = ti * tt + lax.broadcasted_iota(jnp.int32, (tt, 1), 0)
    sums, counts = [], []
    run = u
    taken = 1
    for win in POOL_WINDOWS:
        while taken < win:
            run = run + e_ref[POOL_HALO - taken:POOL_HALO - taken + tt, :]
            taken += 1
        sums.append(run)
        counts.append(jnp.minimum(pos + 1, win).astype(F32))
    d = _pool_windows_to_lanes(sums, counts, u)
    y_pool = jnp.dot(d.astype(BF16), pw_ref[...], preferred_element_type=F32) * ps_ref[...]

    span = tt + CONV_HALO - SUBLANES
    for s in range(1, SUBLANES):
        hs_ref[s - 1, 0:span, :] = h_ref[s:s + span, :]
    y = jnp.zeros((tt, CONV_WIDTH), F32) + db_ref[...]
    for j in range(CONV_K):
        r0 = CONV_HALO - CONV_BUF + j
        s = r0 % SUBLANES
        rows = h_ref[r0:r0 + tt, :] if s == 0 else hs_ref[s - 1, r0 - s:r0 - s + tt, :]
        y = y + rows * dw_ref[j:j + 1, :]
    y_conv = _conv_tail(y, lg_ref[...], lb_ref[...], cw_ref[...], cb_ref[...])

    out = x_ref[...] + jnp.dot(y_pool.astype(BF16), wp_ref[...], preferred_element_type=F32)
    out = out + jnp.dot(y_conv.astype(BF16), wc_ref[...], preferred_element_type=F32)
    o_ref[...] = out + jnp.dot(ya_ref[...].astype(BF16), wa_ref[...], preferred_element_type=F32)

    ht_ref[0] = h_ref[tt:tt + CONV_HALO, :]
    e_ref[0:POOL_HALO, :] = e_ref[tt:tt + POOL_HALO, :]
    h_ref[0:CONV_HALO, :] = h_ref[tt:tt + CONV_HALO, :]


N_MIX_IN = 15


def _mix_out_moba_kernel(pt_ref, *refs, tt, n_pages, n_seq):
    n_moba_in = 4 + 2 * n_seq * n_pages
    mix_in = refs[:N_MIX_IN]
    moba_in = refs[N_MIX_IN:N_MIX_IN + n_moba_in]
    o_ref, ht_ref, att_ref = refs[N_MIX_IN + n_moba_in:N_MIX_IN + n_moba_in + 3]
    scratch = refs[N_MIX_IN + n_moba_in + 3:]
    side = functools.partial(_moba_sample_kernel, pt_ref, *moba_in, att_ref, n_pages=n_pages, n_seq=n_seq)
    _poolconv_prompt_kernel(*mix_in, o_ref, ht_ref, *scratch, tt=tt, side_work=side)


def _mix_out_prompt_moba_sample(x, u, ag, y_att, bsz, t, weights, w_out, q_s, k_new, v_new, cache_k, cache_v,
                                page_table, l, tt=256):
    pool_w, pool_s, dw_w, dw_b, ln_g, ln_b, pw_w, pw_b = weights
    nt = t // tt
    dbsz, rows, _ = q_s.shape
    n_seq = dbsz // (bsz * nt)
    assert n_seq * bsz * nt == dbsz
    n_pages = page_table.shape[1]
    n_phys = cache_k.shape[0] // DEPTH
    past_len = n_pages * PAGE_SIZE
    assert past_len % MOBA_BLOCK == 0 and past_len // MOBA_BLOCK <= SUBLANES
    base = l * n_phys
    row = lambda b, i, pt: (b * nt + i, 0)
    w_rows = lambda width, blk: pl.BlockSpec((None, width, D_MODEL), lambda b, i, pt: (l, blk, 0))
    tok_spec = pl.BlockSpec((n_seq, rows, ATT_WIDTH), lambda b, i, pt: (b * nt + i, 0, 0))
    page_specs = [
        pl.BlockSpec((1, PAGE_SIZE * ATT_HEADS, ATT_HEAD_DIM),
                     lambda b, i, pt, g=g, j=j: (base + pt[((b * nt + i) * n_seq + g) * n_pages + j], 0, 0))
        for g in range(n_seq) for j in range(n_pages)
    ]
    grid_spec = pltpu.PrefetchScalarGridSpec(
        num_scalar_prefetch=1,
        grid=(bsz, nt),
        in_specs=[
            pl.BlockSpec((tt, D_MODEL), row),
            pl.BlockSpec((tt, POOL_WIDTH), row),
            pl.BlockSpec((tt, 2 * CONV_WIDTH), row),
            pl.BlockSpec((tt, ATT_WIDTH), row),
            _layer_spec((POOL_WIDTH, POOL_WIDTH), l, 3),
            _layer_spec((1, POOL_WIDTH), l, 3),
            _layer_spec((CONV_HALO, CONV_WIDTH), l, 3),
            _layer_spec((1, CONV_WIDTH), l, 3),
            _layer_spec((1, CONV_WIDTH), l, 3),
            _layer_spec((1, CONV_WIDTH), l, 3),
            _layer_spec((CONV_WIDTH, CONV_WIDTH), l, 3),
            _layer_spec((1, CONV_WIDTH), l, 3),
            w_rows(POOL_WIDTH, 0),
            w_rows(CONV_WIDTH, POOL_WIDTH // CONV_WIDTH),
            w_rows(ATT_WIDTH, (POOL_WIDTH + CONV_WIDTH) // ATT_WIDTH),
            tok_spec, tok_spec, tok_spec,
            pl.BlockSpec((LANES, past_len), lambda b, i, pt: (0, 0)),
        ] + page_specs + page_specs,
        out_specs=[
            pl.BlockSpec((tt, D_MODEL), row),
            pl.BlockSpec((1, CONV_HALO, CONV_WIDTH), lambda b, i, pt: (b, 0, 0)),
            tok_spec,
        ],
        scratch_shapes=[
            pltpu.VMEM((POOL_HALO + tt, POOL_WIDTH), F32),
            pltpu.VMEM((CONV_HALO + tt, CONV_WIDTH), F32),
            pltpu.VMEM((SUBLANES - 1, CONV_HALO + tt, CONV_WIDTH), F32),
        ],
    )
    return pl.pallas_call(
        functools.partial(_mix_out_moba_kernel, tt=tt, n_pages=n_pages, n_seq=n_seq),
        grid_spec=grid_spec,
        out_shape=[
            jax.ShapeDtypeStruct((bsz * t, D_MODEL), F32),
            jax.ShapeDtypeStruct((bsz, CONV_HALO, CONV_WIDTH), F32),
            jax.ShapeDtypeStruct((dbsz, rows, ATT_WIDTH), F32),
        ],
        compiler_params=_params(2),
        name="mix_out_moba",
    )(page_table.reshape(-1), x, u, ag, y_att, pool_w, pool_s, dw_w, dw_b, ln_g, ln_b, pw_w, pw_b,
      w_out, w_out, w_out, q_s, k_new, v_new, _block_expander(past_len),
      *([cache_k] * (n_seq * n_pages)), *([cache_v] * (n_seq * n_pages)))


def _poolconv_sample_kernel(sp_ref, u_ref, sc_ref, ag_ref, pw_ref, ps_ref, dw_ref, db_ref, lg_ref, lb_ref,
                            cw_ref, cb_ref, yp_ref, yc_ref, h_ref, *, n_new, pos0):
    def pool_row(r):
        return sp_ref[r] if r < POOL_BUF else u_ref[r - POOL_BUF]

    for t in range(n_new):
        h_ref[t] = ag_ref[t, :, 0:CONV_WIDTH] * jax.nn.sigmoid(ag_ref[t, :, CONV_WIDTH:2 * CONV_WIDTH])

    def conv_row(r):
        return sc_ref[r] if r < CONV_BUF else h_ref[r - CONV_BUF]

    for t in range(n_new):
        cur = u_ref[t]
        sums, counts = [], []
        run = cur
        taken = 1
        for win in POOL_WINDOWS:
            while taken < win:
                run = run + pool_row(POOL_BUF + t - taken)
                taken += 1
            sums.append(run)
            counts.append(float(min(pos0 + t + 1, win)))
        d = _pool_windows_to_lanes(sums, counts, cur)
        yp_ref[t] = jnp.dot(d.astype(BF16), pw_ref[...], preferred_element_type=F32) * ps_ref[...]

        y = jnp.zeros(cur.shape, F32) + db_ref[...]
        for j in range(CONV_K):
            y = y + conv_row(t + j) * dw_ref[j:j + 1, :]
        yc_ref[t] = _conv_tail(y, lg_ref[...], lb_ref[...], cw_ref[...], cb_ref[...])


def _poolconv_sample(state_pool_t, u_t, state_conv_t, ag_t, pos0, weights, l):
    pool_w, pool_s, dw_w, dw_b, ln_g, ln_b, pw_w, pw_b = weights
    n_new, bsz, _ = u_t.shape
    full = lambda a: pl.BlockSpec(a.shape, lambda i: (0,) * a.ndim)
    out = jax.ShapeDtypeStruct((n_new, bsz, CONV_WIDTH), F32)
    return pl.pallas_call(
        functools.partial(_poolconv_sample_kernel, n_new=n_new, pos0=pos0),
        grid=(1,),
        in_specs=[
            full(state_pool_t), full(u_t), full(state_conv_t), full(ag_t),
            _layer_spec((POOL_WIDTH, POOL_WIDTH), l, 1),
            _layer_spec((1, POOL_WIDTH), l, 1),
            _layer_spec((CONV_HALO, CONV_WIDTH), l, 1),
            _layer_spec((1, CONV_WIDTH), l, 1),
            _layer_spec((1, CONV_WIDTH), l, 1),
            _layer_spec((1, CONV_WIDTH), l, 1),
            _layer_spec((CONV_WIDTH, CONV_WIDTH), l, 1),
            _layer_spec((1, CONV_WIDTH), l, 1),
        ],
        out_specs=[pl.BlockSpec(out.shape, lambda i: (0, 0, 0))] * 3,
        out_shape=[out, out, out],
        compiler_params=_params(1),
        name="poolconv_sample",
    )(state_pool_t, u_t, state_conv_t, ag_t, pool_w, pool_s, dw_w, dw_b, ln_g, ln_b, pw_w, pw_b)


def _cross_kernel(g_ref, wq_ref, hg_ref, wo_ref, xa_ref, ka_ref, va_ref, xb_ref, kb_ref, vb_ref, oa_ref, ob_ref):
    weights = (g_ref, wq_ref, hg_ref, wo_ref)
    _cross_rows(xb_ref, kb_ref, vb_ref, ob_ref, *weights)
    _cross_rows(xa_ref, ka_ref, va_ref, oa_ref, *weights)


def _cross_rows(x_ref, k_ref, v_ref, o_ref, g_ref, wq_ref, hg_ref, wo_ref):
    hd = MEM_HEAD_DIM
    n_seq, rows, _ = x_ref.shape
    x = x_ref[...].reshape(n_seq * rows, D_MODEL)
    xn = _rms(x, g_ref[...]).astype(BF16)
    q_all = jnp.dot(xn, wq_ref[...], preferred_element_type=F32)
    halves = range(hd // LANES)

    def head_slab(ref, g, h):
        return jnp.concatenate(
            [ref[g, pl.ds(half * MEM_HEADS + h, N_MEM, stride=len(halves) * MEM_HEADS), :] for half in halves],
            axis=1)

    pairs = [(g, h) for g in range(n_seq) for h in range(MEM_HEADS)]
    scores = []
    for g, h in pairs:
        q = _rms(q_all[g * rows:(g + 1) * rows, h * hd:(h + 1) * hd], hg_ref[...])
        q = (q * (hd ** -0.5)).astype(BF16)
        scores.append(lax.dot_general(q, head_slab(k_ref, g, h).astype(BF16), _NT, preferred_element_type=F32))
    s = jnp.concatenate(scores, axis=0)
    p = jnp.exp(s - jnp.max(s, axis=-1, keepdims=True))
    inv = 1.0 / jnp.sum(p, axis=-1, keepdims=True)
    o_seqs = []
    for g in range(n_seq):
        o_heads = []
        for h in range(MEM_HEADS):
            r0 = (g * MEM_HEADS + h) * rows
            o = jnp.dot(p[r0:r0 + rows].astype(BF16), head_slab(v_ref, g, h).astype(BF16),
                        preferred_element_type=F32)
            o_heads.append((o * inv[r0:r0 + rows]).astype(BF16))
        o_seqs.append(jnp.concatenate(o_heads, axis=1))
    o_all = o_seqs[0] if n_seq == 1 else jnp.concatenate(o_seqs, axis=0)
    y = x + jnp.dot(o_all, wo_ref[...], preferred_element_type=F32)
    o_ref[...] = y.reshape(n_seq, rows, D_MODEL)


def _interleaved_memory(cache):
    depth, s, n_mem, heads, hd = cache.shape
    halves = hd // LANES
    c = cache.reshape(depth, s, n_mem, heads, halves, LANES).transpose(0, 1, 2, 4, 3, 5)
    return c.reshape(depth * s, n_mem * halves * heads, LANES)


def _cross(ln, w_q, head_gain, w_o, l, xa, mem_ka, mem_va, seq0_a, tq, xb, mem_kb, mem_vb, seq0_b):
    sa, ra, _ = xa.shape
    sb, rb, _ = xb.shape
    nt = ra // tq
    n_b = sb // (sa * nt)
    assert n_b * sa * nt == sb and seq0_b % n_b == 0
    xa_spec = pl.BlockSpec((1, tq, D_MODEL), lambda s, i: (s, i, 0))
    kva_spec = pl.BlockSpec((1,) + mem_ka.shape[1:], lambda s, i: (seq0_a + s, 0, 0))
    xb_spec = pl.BlockSpec((n_b, rb, D_MODEL), lambda s, i: (s * nt + i, 0, 0))
    kvb_spec = pl.BlockSpec((n_b,) + mem_kb.shape[1:], lambda s, i: (seq0_b // n_b + s * nt + i, 0, 0))
    return pl.pallas_call(
        _cross_kernel,
        grid=(sa, nt),
        in_specs=[_layer_spec((1, D_MODEL), l, 2), _layer_spec((D_MODEL, D_MODEL), l, 2),
                  _layer_spec((1, MEM_HEAD_DIM), l, 2), _layer_spec((D_MODEL, D_MODEL), l, 2),
                  xa_spec, kva_spec, kva_spec, xb_spec, kvb_spec, kvb_spec],
        out_specs=[xa_spec, xb_spec],
        out_shape=[jax.ShapeDtypeStruct(xa.shape, F32), jax.ShapeDtypeStruct(xb.shape, F32)],
        compiler_params=_params(2),
        name="cross",
    )(ln, w_q, head_gain, w_o, xa, mem_ka, mem_va, xb, mem_kb, mem_vb)


def _pad_rows(x, rows):
    return jnp.pad(x, ((0, 0), (0, rows - x.shape[1]), (0, 0)))


def kernel(x_prompt, x_sample, cache_att_k, cache_att_v, cache_mem_k, cache_mem_v, state_pool, state_conv, page_table, mem_prompt, ln_ffn1, w_ffn1_in, w_ffn1_out, ln_mix, w_in, pool_w, pool_scale, conv_dw_w, conv_dw_b, conv_ln_g, conv_ln_b, conv_pw_w, conv_pw_b, att_q_norm, att_k_norm, w_out, ln_cross, ln_mem, w_cq, w_ckv, cq_norm, ck_norm, w_co, ln_ffn2, w_ffn2_in, w_ffn2_out):
    bsz, seq, _ = x_prompt.shape
    dbsz, dseq, _ = x_sample.shape
    n_pages = page_table.shape[1]
    past_len = n_pages * PAGE_SIZE
    n_phys = cache_att_k.shape[1]

    bf = lambda w: w.astype(BF16)
    rowv = lambda v: v[:, None, :]
    w1i, w1o, w2i, w2o = bf(w_ffn1_in), bf(w_ffn1_out), bf(w_ffn2_in), bf(w_ffn2_out)
    w_in_b, w_out_b, w_cq_b, w_ckv_b, w_co_b = bf(w_in), bf(w_out), bf(w_cq), bf(w_ckv), bf(w_co)
    groups = len(POOL_WINDOWS)
    eye = jnp.eye(groups, dtype=F32)
    pool_bd = bf((pool_w[:, :, :, None, :] * eye[None, :, None, :, None])
                 .reshape(DEPTH, POOL_WIDTH, POOL_WIDTH))
    dw_pad = jnp.pad(conv_dw_w, ((0, 0), (0, CONV_HALO - CONV_K), (0, 0)))
    mixer_w = (pool_bd, rowv(pool_scale), dw_pad, rowv(conv_dw_b), rowv(conv_ln_g), rowv(conv_ln_b),
               bf(conv_pw_w), rowv(conv_pw_b))
    ln1, lnm, lnc, lnmem, ln2 = rowv(ln_ffn1), rowv(ln_mix), rowv(ln_cross), rowv(ln_mem), rowv(ln_ffn2)

    tables_p = _rope_tables(jnp.arange(seq, dtype=jnp.int32))
    tm_s = dbsz * dseq
    tables_s = _rope_tables(past_len + jnp.arange(tm_s, dtype=jnp.int32) % dseq)

    c1 = POOL_WIDTH
    c3 = c1 + 2 * CONV_WIDTH
    c4 = c3 + ATT_WIDTH
    c5 = c4 + ATT_WIDTH
    mix_segs = [(c3, ATT_WIDTH, ATT_HEAD_DIM, 0, True, False), (c4, ATT_WIDTH, ATT_HEAD_DIM, 1, True, True),
                (c5, ATT_WIDTH, ATT_HEAD_DIM, None, False, True),
                (0, c1, None, None, False, False), (c1, 2 * CONV_WIDTH, None, None, False, False)]
    mem_segs = [(0, D_MODEL, MEM_HEAD_DIM, 0, False, True), (D_MODEL, D_MODEL, MEM_HEAD_DIM, None, False, True)]
    out_rows = (0, c1, c1 + CONV_WIDTH)

    cache_k = cache_att_k.reshape(DEPTH * n_phys, PAGE_SIZE * ATT_HEADS, ATT_HEAD_DIM)
    cache_v = cache_att_v.reshape(DEPTH * n_phys, PAGE_SIZE * ATT_HEADS, ATT_HEAD_DIM)
    mem_k_s = _interleaved_memory(cache_mem_k)
    mem_v_s = _interleaved_memory(cache_mem_v)
    mem_rows = mem_prompt.reshape(bsz * N_MEM, D_MODEL)

    xp = x_prompt.reshape(bsz * seq, D_MODEL)
    xs = x_sample.reshape(dbsz * dseq, D_MODEL)
    rows_pad = SUBLANES
    outs = {name: [] for name in ("p_pool", "p_conv", "s_pool", "s_conv")}
    cq_gain = rowv(cq_norm)
    kv_p = kv_s = mem_p = None

    for l in range(DEPTH):
        att_gains = jnp.stack([att_q_norm[l], att_k_norm[l]])

        mem_p = _norm_proj(mem_rows, lnmem, w_ckv_b, l, ck_norm[l][None], mem_segs, stacked=mem_p)
        mk, mv = (a.reshape(DEPTH * bsz, -1, LANES) for a in mem_p)
        xp, xs = _ffn(xp, xs, ln1, w1i, w1o, l)
        q, k, v, u, ag = _norm_proj(xp, lnm, w_in_b, l, att_gains, mix_segs, tables=tables_p, stacked=kv_p,
                                    tm=256)
        kv_p = (k, v)
        y_att = _moba_prompt(q, k, v, bsz, seq, l)
        p_pool = u.reshape(bsz, seq, POOL_WIDTH)[:, seq - POOL_BUF:]

        q_s, k_s, v_s, u_s, ag_s = _norm_proj(xs, lnm, w_in_b, l, att_gains, mix_segs, tables=tables_s, tm=tm_s,
                                              stacked=kv_s)
        kv_s = (k_s, v_s)
        pad3 = lambda a: _pad_rows(a.reshape(dbsz, dseq, ATT_WIDTH), 2 * SUBLANES)
        xp, h_tail, y_att_s = _mix_out_prompt_moba_sample(
            xp, u, ag, y_att, bsz, seq, mixer_w, w_out_b, pad3(q_s), pad3(k_s[l]), pad3(v_s[l]),
            cache_k, cache_v, page_table, l)
        outs["p_pool"].append(p_pool)
        outs["p_conv"].append(h_tail[:, CONV_HALO - CONV_BUF:])

        y_att = y_att_s[:, :dseq].reshape(dbsz * dseq, ATT_WIDTH)
        tmaj = lambda a: jnp.swapaxes(a, 0, 1)
        u3 = u_s.reshape(dbsz, dseq, POOL_WIDTH)
        y_pool, y_conv, h_new = _poolconv_sample(
            tmaj(state_pool[l]), tmaj(u3), tmaj(state_conv[l]), tmaj(ag_s.reshape(dbsz, dseq, 2 * CONV_WIDTH)),
            past_len, mixer_w, l)
        bmaj = lambda a: jnp.swapaxes(a, 0, 1).reshape(dbsz * dseq, a.shape[-1])
        xs = _out_proj(xs, [bmaj(y_pool), bmaj(y_conv), y_att], w_out_b, l, out_rows, tm=tm_s)
        xp, xs = _cross(lnc, w_cq_b, cq_gain, w_co_b, l, xp.reshape(bsz, seq, D_MODEL), mk, mv, l * bsz, 512,
                        _pad_rows(xs.reshape(dbsz, dseq, D_MODEL), rows_pad), mem_k_s, mem_v_s, l * dbsz)
        xp = xp.reshape(bsz * seq, D_MODEL)
        xs = xs[:, :dseq].reshape(dbsz * dseq, D_MODEL)
        xp, xs = _ffn(xp, xs, ln2, w2i, w2o, l)
        outs["s_pool"].append(jnp.concatenate([state_pool[l][:, dseq:], u3], axis=1))
        outs["s_conv"].append(jnp.concatenate([state_conv[l][:, dseq:], jnp.swapaxes(h_new, 0, 1)], axis=1))

    st = lambda name: jnp.stack(outs[name])
    heads5 = lambda a, b, t: a.reshape(DEPTH, b, t, ATT_HEADS, ATT_HEAD_DIM)
    tiles = MEM_HEAD_DIM // LANES
    mem5 = lambda a: (a.reshape(DEPTH, bsz, N_MEM, tiles, MEM_HEADS, LANES).transpose(0, 1, 2, 4, 3, 5)
                      .reshape(DEPTH, bsz, N_MEM, MEM_HEADS, MEM_HEAD_DIM))
    return (xp.reshape(bsz, seq, D_MODEL), xs.reshape(dbsz, dseq, D_MODEL),
            heads5(kv_p[0], bsz, seq), heads5(kv_p[1], bsz, seq), mem5(mem_p[0]), mem5(mem_p[1]), st("p_pool"),
            st("p_conv"), heads5(kv_s[0], dbsz, dseq), heads5(kv_s[1], dbsz, dseq), st("s_pool"), st("s_conv"))
```

```python
import functools

import jax
import jax.numpy as jnp
import numpy as np
from jax import lax
from jax.experimental import pallas as pl
from jax.experimental.pallas import tpu as pltpu

D_MODEL = 1024
DEPTH = 2
PAGE_SIZE = 128

POOL_WIDTH = D_MODEL // 4
POOL_WINDOWS = (2, 4, 8, 16)
POOL_GROUP_WIDTH = POOL_WIDTH // len(POOL_WINDOWS)
POOL_BUF = max(POOL_WINDOWS) - 1

CONV_WIDTH = D_MODEL // 4
CONV_K = 31
CONV_BUF = CONV_K - 1

ATT_HEADS = 4
ATT_HEAD_DIM = D_MODEL // 8
ATT_WIDTH = ATT_HEADS * ATT_HEAD_DIM
ROT_DIM = ATT_HEAD_DIM // 4
ROPE_THETA = 500000.0
MOBA_BLOCK = 256
MOBA_TOPK = 3

N_MEM = 256
MEM_HEADS = 4
MEM_HEAD_DIM = D_MODEL // MEM_HEADS

D_FF = ((8 * D_MODEL // 3 + 127) // 128) * 128
EPS = 1e-6

LANES = 128
SUBLANES = 8
VMEM_LIMIT_BYTES = 56 * 1024 * 1024

F32 = jnp.float32
BF16 = jnp.bfloat16
NEG_INF = float("-inf")

_NT = (((1,), (1,)), ((), ()))


def _params(n_axes):
    return pltpu.CompilerParams(dimension_semantics=("arbitrary",) * n_axes,
                                vmem_limit_bytes=VMEM_LIMIT_BYTES)


def _rms(x, g):
    return x * lax.rsqrt(jnp.mean(x * x, axis=-1, keepdims=True) + EPS) * g


def _layer_spec(shape, l, n_grid):
    zeros = (0,) * len(shape)
    once = pl.Buffered(1)
    if n_grid == 1:
        return pl.BlockSpec((None,) + tuple(shape), lambda i: (l,) + zeros, pipeline_mode=once)
    if n_grid == 2:
        return pl.BlockSpec((None,) + tuple(shape), lambda i, j: (l,) + zeros, pipeline_mode=once)
    return pl.BlockSpec((None,) + tuple(shape), lambda i, j, k: (l,) + zeros, pipeline_mode=once)


FFN_CHUNK = 256


def _ffn_kernel(xa_ref, xb_ref, g_ref, wi_ref, wo_ref, oa_ref, ob_ref):
    i = pl.program_id(0)
    x = jnp.where(i == 0, xb_ref[...], xa_ref[...])
    xn = _rms(x, g_ref[...]).astype(BF16)
    acc = None
    for c in range(D_FF // FFN_CHUNK):
        lo = c * FFN_CHUNK
        gate = jnp.dot(xn, wi_ref[:, lo:lo + FFN_CHUNK], preferred_element_type=F32)
        up = jnp.dot(xn, wi_ref[:, D_FF + lo:D_FF + lo + FFN_CHUNK], preferred_element_type=F32)
        act = (gate * jax.nn.sigmoid(gate) * up).astype(BF16)
        part = jnp.dot(act, wo_ref[lo:lo + FFN_CHUNK, :], preferred_element_type=F32)
        acc = part if acc is None else acc + part
    oa_ref[...] = x + 0.5 * acc

    @pl.when(i == 0)
    def _():
        ob_ref[...] = oa_ref[...]


def _ffn(xa, xb, ln, w_in, w_out, l, tm=512):
    n_a = xa.shape[0] // tm
    assert xa.shape[0] == n_a * tm and xb.shape[0] == tm
    a_spec = pl.BlockSpec((tm, D_MODEL), lambda i: (jnp.maximum(i - 1, 0), 0))
    b_spec = pl.BlockSpec((tm, D_MODEL), lambda i: (0, 0))
    return pl.pallas_call(
        _ffn_kernel,
        grid=(n_a + 1,),
        in_specs=[
            a_spec, b_spec,
            _layer_spec((1, D_MODEL), l, 1),
            _layer_spec((D_MODEL, 2 * D_FF), l, 1),
            _layer_spec((D_FF, D_MODEL), l, 1),
        ],
        out_specs=[a_spec, b_spec],
        out_shape=[jax.ShapeDtypeStruct(xa.shape, F32), jax.ShapeDtypeStruct(xb.shape, F32)],
        compiler_params=_params(1),
        name="ffn",
    )(xa, xb, ln, w_in, w_out)


def _norm_proj_kernel(*refs, segs, rope):
    x_ref, g_ref, w_ref, hg_ref = refs[:4]
    if rope:
        cos_ref, sin_lo_ref, sin_hi_ref = refs[4:7]
    out_refs = refs[len(refs) - len(segs):]
    tm = x_ref.shape[0]
    xn = _rms(x_ref[...], g_ref[...]).astype(BF16)
    for (c0, width, head_dim, gain_row, use_rope, head_rows), o_ref in zip(segs, out_refs):
        z = jnp.dot(xn, w_ref[:, c0:c0 + width], preferred_element_type=F32)
        if head_dim is None:
            o_ref[...] = z
            continue
        n_heads = width // head_dim
        for h in range(n_heads):
            zh = z[:, h * head_dim:(h + 1) * head_dim]
            if gain_row is not None:
                zh = _rms(zh, hg_ref[gain_row:gain_row + 1, :])
            if use_rope:
                half = ROT_DIM // 2
                zh = (zh * cos_ref[...]
                      + pltpu.roll(zh, half, 1) * sin_hi_ref[...]
                      + pltpu.roll(zh, LANES - half, 1) * sin_lo_ref[...])
            if not head_rows:
                o_ref[:, h * head_dim:(h + 1) * head_dim] = zh
                continue
            tiles = head_dim // LANES
            for tile in range(tiles):
                rows = pl.ds(tile * n_heads + h, tm, stride=n_heads * tiles)
                piece = zh[:, tile * LANES:(tile + 1) * LANES]
                if len(o_ref.shape) == 2:
                    o_ref[rows, :] = piece
                else:
                    for d in range(o_ref.shape[0]):
                        o_ref[d, rows, :] = piece


def _norm_proj(x, ln, w, l, head_gains, segs, tables=None, tm=512, stacked=None):
    m = x.shape[0]
    n_cols = w.shape[-1]
    rope = tables is not None
    in_specs = [
        pl.BlockSpec((tm, D_MODEL), lambda i: (i, 0)),
        _layer_spec((1, D_MODEL), l, 1),
        _layer_spec((D_MODEL, n_cols), l, 1),
        pl.BlockSpec(head_gains.shape, lambda i: (0, 0)),
    ]
    args = [x, ln, w, head_gains]
    if rope:
        period = tables[0].shape[0] // tm
        for t in tables:
            in_specs.append(pl.BlockSpec((tm, LANES), lambda i: (i % period, 0)))
            args.append(t)
    out_specs, out_shape, aliases = [], [], {}
    for k, s in enumerate(segs):
        width, head_dim, head_rows = s[1], s[2], s[5]
        if not head_rows:
            out_specs.append(pl.BlockSpec((tm, width), lambda i: (i, 0)))
            out_shape.append(jax.ShapeDtypeStruct((m, width), F32))
            continue
        per_token = width // LANES
        out_shape.append(jax.ShapeDtypeStruct((DEPTH, m * per_token, LANES), F32))
        if stacked is None:
            out_specs.append(pl.BlockSpec((DEPTH, tm * per_token, LANES), lambda i: (0, i, 0)))
        else:
            out_specs.append(pl.BlockSpec((None, tm * per_token, LANES), lambda i: (l, i, 0)))
            aliases[len(args)] = k
            in_specs.append(pl.BlockSpec(memory_space=pl.ANY))
            args.append(stacked[len(aliases) - 1])
    return pl.pallas_call(
        functools.partial(_norm_proj_kernel, segs=tuple(segs), rope=rope),
        grid=(m // tm,),
        in_specs=in_specs,
        out_specs=out_specs,
        out_shape=out_shape,
        input_output_aliases=aliases,
        compiler_params=_params(1),
        name="norm_proj",
    )(*args)


def _rope_tables(positions):
    half = ROT_DIM // 2
    inv_freq = jnp.power(jnp.float32(ROPE_THETA), -jnp.arange(half, dtype=F32) / half)
    ang = positions.astype(F32)[:, None] * inv_freq[None, :]
    cos, sin = jnp.cos(ang), jnp.sin(ang)
    n = positions.shape[0]
    rest = ATT_HEAD_DIM - ROT_DIM
    cos_t = jnp.concatenate([cos, cos, jnp.ones((n, rest), F32)], axis=1)
    sin_lo = jnp.concatenate([-sin, jnp.zeros((n, half + rest), F32)], axis=1)
    sin_hi = jnp.concatenate([jnp.zeros((n, half), F32), sin, jnp.zeros((n, rest), F32)], axis=1)
    return cos_t, sin_lo, sin_hi


def _out_proj_kernel(*refs, n_terms):
    x_ref = refs[0]
    y_refs = refs[1:1 + n_terms]
    w_refs = refs[1 + n_terms:1 + 2 * n_terms]
    o_ref = refs[1 + 2 * n_terms]
    acc = x_ref[...]
    for y_ref, w_ref in zip(y_refs, w_refs):
        acc = acc + jnp.dot(y_ref[...].astype(BF16), w_ref[...], preferred_element_type=F32)
    o_ref[...] = acc


def _out_proj(x, ys, w, l, row_starts, tm=512):
    m = x.shape[0]
    in_specs = [pl.BlockSpec((tm, D_MODEL), lambda i: (i, 0))]
    for y in ys:
        in_specs.append(pl.BlockSpec((tm, y.shape[1]), lambda i: (i, 0)))
    for y, r0 in zip(ys, row_starts):
        width = y.shape[1]
        blk = r0 // width
        in_specs.append(pl.BlockSpec((None, width, D_MODEL), lambda i, blk=blk: (l, blk, 0)))
    return pl.pallas_call(
        functools.partial(_out_proj_kernel, n_terms=len(ys)),
        grid=(m // tm,),
        in_specs=in_specs,
        out_specs=pl.BlockSpec((tm, D_MODEL), lambda i: (i, 0)),
        out_shape=jax.ShapeDtypeStruct((m, D_MODEL), F32),
        compiler_params=_params(1),
        name="out_proj",
    )(x, *ys, *([w] * len(ys)))


def _select_blocks(gate, n_allowed):
    lane = lax.broadcasted_iota(jnp.int32, gate.shape, 1)
    allowed = lane < n_allowed
    g = jnp.where(allowed, gate, NEG_INF)
    rank = jnp.zeros(gate.shape, F32)
    for r in range(1, SUBLANES):
        lower = pltpu.roll(g, r, 1)
        higher = pltpu.roll(g, LANES - r, 1)
        rank = rank + jnp.where(lower >= g, 1.0, 0.0) + jnp.where(higher > g, 1.0, 0.0)
    return jnp.where(allowed, jnp.where(rank < MOBA_TOPK, 1.0, 0.0), 0.0)


def _dot_nt_split(a, b):
    a_hi = a.astype(BF16)
    a_lo = (a - a_hi.astype(F32)).astype(BF16)
    b_hi = b.astype(BF16)
    b_lo = (b - b_hi.astype(F32)).astype(BF16)
    m = a.shape[0]
    by_hi = lax.dot_general(jnp.concatenate([a_hi, a_lo], axis=0), b_hi, _NT, preferred_element_type=F32)
    return by_hi[0:m] + by_hi[m:2 * m] + lax.dot_general(a_hi, b_lo, _NT, preferred_element_type=F32)


def _block_expander(n_keys):
    blk = np.arange(n_keys) // MOBA_BLOCK
    return jnp.asarray((np.arange(LANES)[:, None] == blk[None, :]).astype(np.float32), dtype=BF16)


MASK_BIAS = -1e30


def _select_blocks_t(gate_t, n_allowed):
    blk = lax.broadcasted_iota(jnp.int32, gate_t.shape, 0)
    allowed = blk < n_allowed
    g = jnp.where(allowed, gate_t, NEG_INF)
    rank = jnp.zeros(gate_t.shape, F32)
    for m in range(SUBLANES):
        gm = g[m:m + 1, :]
        tie = jnp.where(gm == g, jnp.where(blk > m, 1.0, 0.0), 0.0)
        rank = rank + jnp.where(gm > g, 1.0, tie)
    return jnp.where(allowed, jnp.where(rank < MOBA_TOPK, 1.0, 0.0), 0.0)


def _moba_prompt_kernel(q_ref, k_ref, v_ref, kbias_ref, o_ref, ka_ref, vb_ref, km_ref, *, n_blocks, per_step):
    qi = pl.program_id(1)
    blk = MOBA_BLOCK
    dh = ATT_HEAD_DIM
    heads = range(ATT_HEADS)

    @pl.when(qi == 0)
    def _():
        km_ref[...] = jnp.zeros(km_ref.shape, F32)
        for h in heads:
            vb_ref[:, h * dh:(h + 1) * dh] = v_ref[pl.ds(h, n_blocks * blk, stride=ATT_HEADS), :].astype(BF16)
            ka_ref[h, :, dh:2 * dh] = kbias_ref[h]
            for n in range(n_blocks):
                kb = k_ref[pl.ds(n * blk * ATT_HEADS + h, blk, stride=ATT_HEADS), :]
                ka_ref[h, n * blk:(n + 1) * blk, 0:dh] = kb.astype(BF16)
                r = h * SUBLANES + n
                km_ref[r:r + 1, h * dh:(h + 1) * dh] = jnp.mean(kb, axis=0, keepdims=True)

    row = lax.broadcasted_iota(jnp.int32, (blk, blk), 0)
    col = lax.broadcasted_iota(jnp.int32, (blk, blk), 1)

    def step(c, r0):
        n_keys = (c + 1) * blk
        gated = c > MOBA_TOPK
        if gated:
            gate_all = _dot_nt_split(km_ref[...], q_ref[r0:r0 + blk, :])
            n_rows = ATT_HEADS * SUBLANES
            flags = []
            for h in heads:
                gate_t = gate_all[h * SUBLANES:(h + 1) * SUBLANES]
                blk_id = lax.broadcasted_iota(jnp.int32, gate_t.shape, 0)
                flags.append(jnp.where(blk_id == c, 0.0, 1.0 - _select_blocks_t(gate_t, c)))
            flags.append(jnp.zeros((LANES - n_rows, blk), F32))
            unchosen = jnp.concatenate(flags, axis=0).T.astype(BF16)
        for h in heads:
            q = q_ref[r0:r0 + blk, h * dh:(h + 1) * dh]
            qs = (q * (dh ** -0.5)).astype(BF16)
            if gated:
                qa = jnp.concatenate([qs, unchosen], axis=1)
                s = lax.dot_general(qa, ka_ref[h, 0:n_keys, :], _NT, preferred_element_type=F32)
            else:
                s = lax.dot_general(qs, ka_ref[h, 0:n_keys, 0:dh], _NT, preferred_element_type=F32)
            parts = [s[:, 0:c * blk]] if c else []
            parts.append(jnp.where(row >= col, s[:, c * blk:n_keys], NEG_INF))
            s = jnp.concatenate(parts, axis=1)
            p = jnp.exp(s - jnp.max(s, axis=-1, keepdims=True))
            inv = 1.0 / jnp.sum(p, axis=-1, keepdims=True)
            o = jnp.dot(p.astype(BF16), vb_ref[0:n_keys, h * dh:(h + 1) * dh], preferred_element_type=F32)
            o_ref[r0:r0 + blk, h * dh:(h + 1) * dh] = o * inv

    def tile(j):
        for r in range(per_step):
            step(j * per_step + r, r * blk)

    for j in range(n_blocks // per_step):
        pl.when(qi == j)(functools.partial(tile, j))


def _moba_prompt(q, k, v, bsz, t, l):
    assert t % MOBA_BLOCK == 0 and t // MOBA_BLOCK <= SUBLANES
    n_blocks = t // MOBA_BLOCK
    dh = ATT_HEAD_DIM
    key_blk = np.arange(t) // MOBA_BLOCK
    lane_of = np.arange(ATT_HEADS)[:, None, None] * SUBLANES + key_blk[None, :, None]
    kbias = jnp.asarray(np.where(lane_of == np.arange(LANES)[None, None, :], MASK_BIAS, 0.0), dtype=BF16)
    per_step = 2 if n_blocks % 2 == 0 else 1
    nt = n_blocks // per_step
    tile_spec = pl.BlockSpec((per_step * MOBA_BLOCK, ATT_WIDTH), lambda b, i: (b * nt + i, 0))
    return pl.pallas_call(
        functools.partial(_moba_prompt_kernel, n_blocks=n_blocks, per_step=per_step),
        grid=(bsz, nt),
        in_specs=[
            tile_spec,
            pl.BlockSpec((None, t * ATT_HEADS, dh), lambda b, i: (l, b, 0)),
            pl.BlockSpec((None, t * ATT_HEADS, dh), lambda b, i: (l, b, 0)),
            pl.BlockSpec((ATT_HEADS, t, LANES), lambda b, i: (0, 0, 0)),
        ],
        out_specs=tile_spec,
        out_shape=jax.ShapeDtypeStruct((bsz * t, ATT_WIDTH), F32),
        scratch_shapes=[
            pltpu.VMEM((ATT_HEADS, t, 2 * dh), BF16),
            pltpu.VMEM((t, ATT_WIDTH), BF16),
            pltpu.VMEM((ATT_HEADS * SUBLANES, ATT_WIDTH), F32),
        ],
        compiler_params=_params(2),
        name="moba_prompt",
    )(q, k, v, kbias)


def _moba_sample_kernel(pt_ref, q_ref, kn_ref, vn_ref, e_ref, *refs, n_pages, n_seq):
    del pt_ref
    k_refs = refs[:n_seq * n_pages]
    v_refs = refs[n_seq * n_pages:2 * n_seq * n_pages]
    o_ref = refs[2 * n_seq * n_pages]
    rows = q_ref.shape[1]
    pages_per_block = MOBA_BLOCK // PAGE_SIZE
    n_blocks = n_pages // pages_per_block
    dh = ATT_HEAD_DIM
    pairs = [(g, h) for g in range(n_seq) for h in range(ATT_HEADS)]

    def block_rows(page_refs, g, n, h):
        first = g * n_pages + n * pages_per_block
        return jnp.concatenate(
            [page_refs[j][0, pl.ds(h, PAGE_SIZE, stride=ATT_HEADS), :] for j in range(first, first + pages_per_block)],
            axis=0)

    qs, gate_rows, s_rows = [], [], []
    for g, h in pairs:
        q = q_ref[g, :, h * dh:(h + 1) * dh]
        qs.append((q * (dh ** -0.5)).astype(BF16))
        mean_rows, s_parts = [], []
        for n in range(n_blocks):
            kb = block_rows(k_refs, g, n, h)
            mean_rows.append(jnp.sum(kb, axis=0, keepdims=True) / MOBA_BLOCK)
            s_parts.append(lax.dot_general(qs[-1], kb.astype(BF16), _NT, preferred_element_type=F32))
        mean_rows.append(jnp.zeros((LANES - n_blocks, dh), F32))
        gate_rows.append(_dot_nt_split(q, jnp.concatenate(mean_rows, axis=0)))
        s_rows.append(jnp.concatenate(s_parts, axis=1))
    sel = _select_blocks(jnp.concatenate(gate_rows, axis=0), n_blocks)
    sel_keys = jnp.dot(sel.astype(BF16), e_ref[...], preferred_element_type=F32)
    s = jnp.where(sel_keys > 0.5, jnp.concatenate(s_rows, axis=0), NEG_INF)

    pad = jnp.zeros((LANES - rows, dh), F32)
    padded = lambda ref, g, h: jnp.concatenate([ref[g, :, h * dh:(h + 1) * dh], pad], axis=0).astype(BF16)
    s_own = jnp.concatenate(
        [lax.dot_general(qs[i], padded(kn_ref, g, h), _NT, preferred_element_type=F32)
         for i, (g, h) in enumerate(pairs)], axis=0)
    t_row = lax.broadcasted_iota(jnp.int32, s_own.shape, 0) % rows
    key = lax.broadcasted_iota(jnp.int32, s_own.shape, 1)
    s_own = jnp.where(key <= t_row, s_own, NEG_INF)

    m = jnp.maximum(jnp.max(s, axis=-1, keepdims=True), jnp.max(s_own, axis=-1, keepdims=True))
    p = jnp.exp(s - m)
    p_own = jnp.exp(s_own - m)
    inv = 1.0 / (jnp.sum(p, axis=-1, keepdims=True) + jnp.sum(p_own, axis=-1, keepdims=True))
    pb = p.astype(BF16)
    pb_own = p_own.astype(BF16)
    outs = []
    for i, (g, h) in enumerate(pairs):
        r0 = i * rows
        o = jnp.dot(pb_own[r0:r0 + rows], padded(vn_ref, g, h), preferred_element_type=F32)
        for n in range(n_blocks):
            o = o + jnp.dot(pb[r0:r0 + rows, n * MOBA_BLOCK:(n + 1) * MOBA_BLOCK],
                            block_rows(v_refs, g, n, h).astype(BF16), preferred_element_type=F32)
        outs.append(o * inv[r0:r0 + rows])
    for g in range(n_seq):
        o_ref[g] = jnp.concatenate(outs[g * ATT_HEADS:(g + 1) * ATT_HEADS], axis=1)


def _pool_windows_to_lanes(sums, counts, cur):
    lane = lax.broadcasted_iota(jnp.int32, cur.shape, 1)
    d = sums[-1] / counts[-1]
    for g in range(len(POOL_WINDOWS) - 2, -1, -1):
        d = jnp.where(lane < (g + 1) * POOL_GROUP_WIDTH, sums[g] / counts[g], d)
    return d - cur


def _conv_tail(y, ln_g, ln_b, pw_w, pw_b):
    yc = y - jnp.mean(y, axis=-1, keepdims=True)
    yn = yc * lax.rsqrt(jnp.mean(yc * yc, axis=-1, keepdims=True) + EPS) * ln_g + ln_b
    act = yn * jax.nn.sigmoid(yn)
    return jnp.dot(act.astype(BF16), pw_w, preferred_element_type=F32) + pw_b


POOL_HALO = 16
CONV_HALO = 32


def _poolconv_prompt_kernel(x_ref, u_ref, ag_ref, ya_ref, pw_ref, ps_ref, dw_ref, db_ref, lg_ref, lb_ref,
                            cw_ref, cb_ref, wp_ref, wc_ref, wa_ref, o_ref, ht_ref, e_ref, h_ref, hs_ref, *, tt,
                            side_work=None):
    ti = pl.program_id(1)

    @pl.when(ti == 0)
    def _():
        e_ref[0:POOL_HALO, :] = jnp.zeros((POOL_HALO, POOL_WIDTH), F32)
        h_ref[0:CONV_HALO, :] = jnp.zeros((CONV_HALO, CONV_WIDTH), F32)

    if side_work is not None:
        side_work()

    u = u_ref[...]
    e_ref[POOL_HALO:POOL_HALO + tt, :] = u
    a = ag_ref[:, 0:CONV_WIDTH]
    gate = ag_ref[:, CONV_WIDTH:2 * CONV_WIDTH]
    h_ref[CONV_HALO:CONV_HALO + tt, :] = a * jax.nn.sigmoid(gate)

    pos = ti * tt + lax.broadcasted_iota(jnp.int32, (tt, 1), 0)
    sums, counts = [], []
    run = u
    taken = 1
    for win in POOL_WINDOWS:
        while taken < win:
            run = run + e_ref[POOL_HALO - taken:POOL_HALO - taken + tt, :]
            taken += 1
        sums.append(run)
        counts.append(jnp.minimum(pos + 1, win).astype(F32))
    d = _pool_windows_to_lanes(sums, counts, u)
    y_pool = jnp.dot(d.astype(BF16), pw_ref[...], preferred_element_type=F32) * ps_ref[...]

    span = tt + CONV_HALO - SUBLANES
    for s in range(1, SUBLANES):
        hs_ref[s - 1, 0:span, :] = h_ref[s:s + span, :]
    y = jnp.zeros((tt, CONV_WIDTH), F32) + db_ref[...]
    for j in range(CONV_K):
        r0 = CONV_HALO - CONV_BUF + j
        s = r0 % SUBLANES
        rows = h_ref[r0:r0 + tt, :] if s == 0 else hs_ref[s - 1, r0 - s:r0 - s + tt, :]
        y = y + rows * dw_ref[j:j + 1, :]
    y_conv = _conv_tail(y, lg_ref[...], lb_ref[...], cw_ref[...], cb_ref[...])

    out = x_ref[...] + jnp.dot(y_pool.astype(BF16), wp_ref[...], preferred_element_type=F32)
    out = out + jnp.dot(y_conv.astype(BF16), wc_ref[...], preferred_element_type=F32)
    o_ref[...] = out + jnp.dot(ya_ref[...].astype(BF16), wa_ref[...], preferred_element_type=F32)

    ht_ref[0] = h_ref[tt:tt + CONV_HALO, :]
    e_ref[0:POOL_HALO, :] = e_ref[tt:tt + POOL_HALO, :]
    h_ref[0:CONV_HALO, :] = h_ref[tt:tt + CONV_HALO, :]


N_MIX_IN = 15


def _mix_out_moba_kernel(pt_ref, *refs, tt, n_pages, n_seq):
    n_moba_in = 4 + 2 * n_seq * n_pages
    mix_in = refs[:N_MIX_IN]
    moba_in = refs[N_MIX_IN:N_MIX_IN + n_moba_in]
    o_ref, ht_ref, att_ref = refs[N_MIX_IN + n_moba_in:N_MIX_IN + n_moba_in + 3]
    scratch = refs[N_MIX_IN + n_moba_in + 3:]
    side = functools.partial(_moba_sample_kernel, pt_ref, *moba_in, att_ref, n_pages=n_pages, n_seq=n_seq)
    _poolconv_prompt_kernel(*mix_in, o_ref, ht_ref, *scratch, tt=tt, side_work=side)


def _mix_out_prompt_moba_sample(x, u, ag, y_att, bsz, t, weights, w_out, q_s, k_new, v_new, cache_k, cache_v,
                                page_table, l, tt=256):
    pool_w, pool_s, dw_w, dw_b, ln_g, ln_b, pw_w, pw_b = weights
    nt = t // tt
    dbsz, rows, _ = q_s.shape
    n_seq = dbsz // (bsz * nt)
    assert n_seq * bsz * nt == dbsz
    n_pages = page_table.shape[1]
    n_phys = cache_k.shape[0] // DEPTH
    past_len = n_pages * PAGE_SIZE
    assert past_len % MOBA_BLOCK == 0 and past_len // MOBA_BLOCK <= SUBLANES
    base = l * n_phys
    row = lambda b, i, pt: (b * nt + i, 0)
    w_rows = lambda width, blk: pl.BlockSpec((None, width, D_MODEL), lambda b, i, pt: (l, blk, 0))
    tok_spec = pl.BlockSpec((n_seq, rows, ATT_WIDTH), lambda b, i, pt: (b * nt + i, 0, 0))
    page_specs = [
        pl.BlockSpec((1, PAGE_SIZE * ATT_HEADS, ATT_HEAD_DIM),
                     lambda b, i, pt, g=g, j=j: (base + pt[((b * nt + i) * n_seq + g) * n_pages + j], 0, 0))
        for g in range(n_seq) for j in range(n_pages)
    ]
    grid_spec = pltpu.PrefetchScalarGridSpec(
        num_scalar_prefetch=1,
        grid=(bsz, nt),
        in_specs=[
            pl.BlockSpec((tt, D_MODEL), row),
            pl.BlockSpec((tt, POOL_WIDTH), row),
            pl.BlockSpec((tt, 2 * CONV_WIDTH), row),
            pl.BlockSpec((tt, ATT_WIDTH), row),
            _layer_spec((POOL_WIDTH, POOL_WIDTH), l, 3),
            _layer_spec((1, POOL_WIDTH), l, 3),
            _layer_spec((CONV_HALO, CONV_WIDTH), l, 3),
            _layer_spec((1, CONV_WIDTH), l, 3),
            _layer_spec((1, CONV_WIDTH), l, 3),
            _layer_spec((1, CONV_WIDTH), l, 3),
            _layer_spec((CONV_WIDTH, CONV_WIDTH), l, 3),
            _layer_spec((1, CONV_WIDTH), l, 3),
            w_rows(POOL_WIDTH, 0),
            w_rows(CONV_WIDTH, POOL_WIDTH // CONV_WIDTH),
            w_rows(ATT_WIDTH, (POOL_WIDTH + CONV_WIDTH) // ATT_WIDTH),
            tok_spec, tok_spec, tok_spec,
            pl.BlockSpec((LANES, past_len), lambda b, i, pt: (0, 0)),
        ] + page_specs + page_specs,
        out_specs=[
            pl.BlockSpec((tt, D_MODEL), row),
            pl.BlockSpec((1, CONV_HALO, CONV_WIDTH), lambda b, i, pt: (b, 0, 0)),
            tok_spec,
        ],
        scratch_shapes=[
            pltpu.VMEM((POOL_HALO + tt, POOL_WIDTH), F32),
            pltpu.VMEM((CONV_HALO + tt, CONV_WIDTH), F32),
            pltpu.VMEM((SUBLANES - 1, CONV_HALO + tt, CONV_WIDTH), F32),
        ],
    )
    return pl.pallas_call(
        functools.partial(_mix_out_moba_kernel, tt=tt, n_pages=n_pages, n_seq=n_seq),
        grid_spec=grid_spec,
        out_shape=[
            jax.ShapeDtypeStruct((bsz * t, D_MODEL), F32),
            jax.ShapeDtypeStruct((bsz, CONV_HALO, CONV_WIDTH), F32),
            jax.ShapeDtypeStruct((dbsz, rows, ATT_WIDTH), F32),
        ],
        compiler_params=_params(2),
        name="mix_out_moba",
    )(page_table.reshape(-1), x, u, ag, y_att, pool_w, pool_s, dw_w, dw_b, ln_g, ln_b, pw_w, pw_b,
      w_out, w_out, w_out, q_s, k_new, v_new, _block_expander(past_len),
      *([cache_k] * (n_seq * n_pages)), *([cache_v] * (n_seq * n_pages)))


def _poolconv_sample_kernel(sp_ref, u_ref, sc_ref, ag_ref, pw_ref, ps_ref, dw_ref, db_ref, lg_ref, lb_ref,
                            cw_ref, cb_ref, yp_ref, yc_ref, h_ref, *, n_new, pos0):
    def pool_row(r):
        return sp_ref[r] if r < POOL_BUF else u_ref[r - POOL_BUF]

    for t in range(n_new):
        h_ref[t] = ag_ref[t, :, 0:CONV_WIDTH] * jax.nn.sigmoid(ag_ref[t, :, CONV_WIDTH:2 * CONV_WIDTH])

    def conv_row(r):
        return sc_ref[r] if r < CONV_BUF else h_ref[r - CONV_BUF]

    for t in range(n_new):
        cur = u_ref[t]
        sums, counts = [], []
        run = cur
        taken = 1
        for win in POOL_WINDOWS:
            while taken < win:
                run = run + pool_row(POOL_BUF + t - taken)
                taken += 1
            sums.append(run)
            counts.append(float(min(pos0 + t + 1, win)))
        d = _pool_windows_to_lanes(sums, counts, cur)
        yp_ref[t] = jnp.dot(d.astype(BF16), pw_ref[...], preferred_element_type=F32) * ps_ref[...]

        y = jnp.zeros(cur.shape, F32) + db_ref[...]
        for j in range(CONV_K):
            y = y + conv_row(t + j) * dw_ref[j:j + 1, :]
        yc_ref[t] = _conv_tail(y, lg_ref[...], lb_ref[...], cw_ref[...], cb_ref[...])


def _poolconv_sample(state_pool_t, u_t, state_conv_t, ag_t, pos0, weights, l):
    pool_w, pool_s, dw_w, dw_b, ln_g, ln_b, pw_w, pw_b = weights
    n_new, bsz, _ = u_t.shape
    full = lambda a: pl.BlockSpec(a.shape, lambda i: (0,) * a.ndim)
    out = jax.ShapeDtypeStruct((n_new, bsz, CONV_WIDTH), F32)
    return pl.pallas_call(
        functools.partial(_poolconv_sample_kernel, n_new=n_new, pos0=pos0),
        grid=(1,),
        in_specs=[
            full(state_pool_t), full(u_t), full(state_conv_t), full(ag_t),
            _layer_spec((POOL_WIDTH, POOL_WIDTH), l, 1),
            _layer_spec((1, POOL_WIDTH), l, 1),
            _layer_spec((CONV_HALO, CONV_WIDTH), l, 1),
            _layer_spec((1, CONV_WIDTH), l, 1),
            _layer_spec((1, CONV_WIDTH), l, 1),
            _layer_spec((1, CONV_WIDTH), l, 1),
            _layer_spec((CONV_WIDTH, CONV_WIDTH), l, 1),
            _layer_spec((1, CONV_WIDTH), l, 1),
        ],
        out_specs=[pl.BlockSpec(out.shape, lambda i: (0, 0, 0))] * 3,
        out_shape=[out, out, out],
        compiler_params=_params(1),
        name="poolconv_sample",
    )(state_pool_t, u_t, state_conv_t, ag_t, pool_w, pool_s, dw_w, dw_b, ln_g, ln_b, pw_w, pw_b)


def _cross_kernel(g_ref, wq_ref, hg_ref, wo_ref, xa_ref, ka_ref, va_ref, xb_ref, kb_ref, vb_ref, oa_ref, ob_ref):
    weights = (g_ref, wq_ref, hg_ref, wo_ref)
    _cross_rows(xb_ref, kb_ref, vb_ref, ob_ref, *weights)
    _cross_rows(xa_ref, ka_ref, va_ref, oa_ref, *weights)


def _cross_rows(x_ref, k_ref, v_ref, o_ref, g_ref, wq_ref, hg_ref, wo_ref):
    hd = MEM_HEAD_DIM
    n_seq, rows, _ = x_ref.shape
    x = x_ref[...].reshape(n_seq * rows, D_MODEL)
    xn = _rms(x, g_ref[...]).astype(BF16)
    q_all = jnp.dot(xn, wq_ref[...], preferred_element_type=F32)
    halves = range(hd // LANES)

    def head_slab(ref, g, h):
        return jnp.concatenate(
            [ref[g, pl.ds(half * MEM_HEADS + h, N_MEM, stride=len(halves) * MEM_HEADS), :] for half in halves],
            axis=1)

    pairs = [(g, h) for g in range(n_seq) for h in range(MEM_HEADS)]
    scores = []
    for g, h in pairs:
        q = _rms(q_all[g * rows:(g + 1) * rows, h * hd:(h + 1) * hd], hg_ref[...])
        q = (q * (hd ** -0.5)).astype(BF16)
        scores.append(lax.dot_general(q, head_slab(k_ref, g, h).astype(BF16), _NT, preferred_element_type=F32))
    s = jnp.concatenate(scores, axis=0)
    p = jnp.exp(s - jnp.max(s, axis=-1, keepdims=True))
    inv = 1.0 / jnp.sum(p, axis=-1, keepdims=True)
    o_seqs = []
    for g in range(n_seq):
        o_heads = []
        for h in range(MEM_HEADS):
            r0 = (g * MEM_HEADS + h) * rows
            o = jnp.dot(p[r0:r0 + rows].astype(BF16), head_slab(v_ref, g, h).astype(BF16),
                        preferred_element_type=F32)
            o_heads.append((o * inv[r0:r0 + rows]).astype(BF16))
        o_seqs.append(jnp.concatenate(o_heads, axis=1))
    o_all = o_seqs[0] if n_seq == 1 else jnp.concatenate(o_seqs, axis=0)
    y = x + jnp.dot(o_all, wo_ref[...], preferred_element_type=F32)
    o_ref[...] = y.reshape(n_seq, rows, D_MODEL)


def _interleaved_memory(cache):
    depth, s, n_mem, heads, hd = cache.shape
    halves = hd // LANES
    c = cache.reshape(depth, s, n_mem, heads, halves, LANES).transpose(0, 1, 2, 4, 3, 5)
    return c.reshape(depth * s, n_mem * halves * heads, LANES)


def _cross(ln, w_q, head_gain, w_o, l, xa, mem_ka, mem_va, seq0_a, tq, xb, mem_kb, mem_vb, seq0_b):
    sa, ra, _ = xa.shape
    sb, rb, _ = xb.shape
    nt = ra // tq
    n_b = sb // (sa * nt)
    assert n_b * sa * nt == sb and seq0_b % n_b == 0
    xa_spec = pl.BlockSpec((1, tq, D_MODEL), lambda s, i: (s, i, 0))
    kva_spec = pl.BlockSpec((1,) + mem_ka.shape[1:], lambda s, i: (seq0_a + s, 0, 0))
    xb_spec = pl.BlockSpec((n_b, rb, D_MODEL), lambda s, i: (s * nt + i, 0, 0))
    kvb_spec = pl.BlockSpec((n_b,) + mem_kb.shape[1:], lambda s, i: (seq0_b // n_b + s * nt + i, 0, 0))
    return pl.pallas_call(
        _cross_kernel,
        grid=(sa, nt),
        in_specs=[_layer_spec((1, D_MODEL), l, 2), _layer_spec((D_MODEL, D_MODEL), l, 2),
                  _layer_spec((1, MEM_HEAD_DIM), l, 2), _layer_spec((D_MODEL, D_MODEL), l, 2),
                  xa_spec, kva_spec, kva_spec, xb_spec, kvb_spec, kvb_spec],
        out_specs=[xa_spec, xb_spec],
        out_shape=[jax.ShapeDtypeStruct(xa.shape, F32), jax.ShapeDtypeStruct(xb.shape, F32)],
        compiler_params=_params(2),
        name="cross",
    )(ln, w_q, head_gain, w_o, xa, mem_ka, mem_va, xb, mem_kb, mem_vb)


def _pad_rows(x, rows):
    return jnp.pad(x, ((0, 0), (0, rows - x.shape[1]), (0, 0)))


def kernel(x_prompt, x_sample, cache_att_k, cache_att_v, cache_mem_k, cache_mem_v, state_pool, state_conv, page_table, mem_prompt, ln_ffn1, w_ffn1_in, w_ffn1_out, ln_mix, w_in, pool_w, pool_scale, conv_dw_w, conv_dw_b, conv_ln_g, conv_ln_b, conv_pw_w, conv_pw_b, att_q_norm, att_k_norm, w_out, ln_cross, ln_mem, w_cq, w_ckv, cq_norm, ck_norm, w_co, ln_ffn2, w_ffn2_in, w_ffn2_out):
    bsz, seq, _ = x_prompt.shape
    dbsz, dseq, _ = x_sample.shape
    n_pages = page_table.shape[1]
    past_len = n_pages * PAGE_SIZE
    n_phys = cache_att_k.shape[1]

    bf = lambda w: w.astype(BF16)
    rowv = lambda v: v[:, None, :]
    w1i, w1o, w2i, w2o = bf(w_ffn1_in), bf(w_ffn1_out), bf(w_ffn2_in), bf(w_ffn2_out)
    w_in_b, w_out_b, w_cq_b, w_ckv_b, w_co_b = bf(w_in), bf(w_out), bf(w_cq), bf(w_ckv), bf(w_co)
    groups = len(POOL_WINDOWS)
    eye = jnp.eye(groups, dtype=F32)
    pool_bd = bf((pool_w[:, :, :, None, :] * eye[None, :, None, :, None])
                 .reshape(DEPTH, POOL_WIDTH, POOL_WIDTH))
    dw_pad = jnp.pad(conv_dw_w, ((0, 0), (0, CONV_HALO - CONV_K), (0, 0)))
    mixer_w = (pool_bd, rowv(pool_scale), dw_pad, rowv(conv_dw_b), rowv(conv_ln_g), rowv(conv_ln_b),
               bf(conv_pw_w), rowv(conv_pw_b))
    ln1, lnm, lnc, lnmem, ln2 = rowv(ln_ffn1), rowv(ln_mix), rowv(ln_cross), rowv(ln_mem), rowv(ln_ffn2)

    tables_p = _rope_tables(jnp.arange(seq, dtype=jnp.int32))
    tm_s = dbsz * dseq
    tables_s = _rope_tables(past_len + jnp.arange(tm_s, dtype=jnp.int32) % dseq)

    c1 = POOL_WIDTH
    c3 = c1 + 2 * CONV_WIDTH
    c4 = c3 + ATT_WIDTH
    c5 = c4 + ATT_WIDTH
    mix_segs = [(c3, ATT_WIDTH, ATT_HEAD_DIM, 0, True, False), (c4, ATT_WIDTH, ATT_HEAD_DIM, 1, True, True),
                (c5, ATT_WIDTH, ATT_HEAD_DIM, None, False, True),
                (0, c1, None, None, False, False), (c1, 2 * CONV_WIDTH, None, None, False, False)]
    mem_segs = [(0, D_MODEL, MEM_HEAD_DIM, 0, False, True), (D_MODEL, D_MODEL, MEM_HEAD_DIM, None, False, True)]
    out_rows = (0, c1, c1 + CONV_WIDTH)

    cache_k = cache_att_k.reshape(DEPTH * n_phys, PAGE_SIZE * ATT_HEADS, ATT_HEAD_DIM)
    cache_v = cache_att_v.reshape(DEPTH * n_phys, PAGE_SIZE * ATT_HEADS, ATT_HEAD_DIM)
    mem_k_s = _interleaved_memory(cache_mem_k)
    mem_v_s = _interleaved_memory(cache_mem_v)
    mem_rows = mem_prompt.reshape(bsz * N_MEM, D_MODEL)

    xp = x_prompt.reshape(bsz * seq, D_MODEL)
    xs = x_sample.reshape(dbsz * dseq, D_MODEL)
    rows_pad = SUBLANES
    outs = {name: [] for name in ("p_pool", "p_conv", "s_pool", "s_conv")}
    cq_gain = rowv(cq_norm)
    kv_p = kv_s = mem_p = None

    for l in range(DEPTH):
        att_gains = jnp.stack([att_q_norm[l], att_k_norm[l]])

        mem_p = _norm_proj(mem_rows, lnmem, w_ckv_b, l, ck_norm[l][None], mem_segs, stacked=mem_p)
        mk, mv = (a.reshape(DEPTH * bsz, -1, LANES) for a in mem_p)
        xp, xs = _ffn(xp, xs, ln1, w1i, w1o, l)
        q, k, v, u, ag = _norm_proj(xp, lnm, w_in_b, l, att_gains, mix_segs, tables=tables_p, stacked=kv_p,
                                    tm=256)
        kv_p = (k, v)
        y_att = _moba_prompt(q, k, v, bsz, seq, l)
        p_pool = u.reshape(bsz, seq, POOL_WIDTH)[:, seq - POOL_BUF:]

        q_s, k_s, v_s, u_s, ag_s = _norm_proj(xs, lnm, w_in_b, l, att_gains, mix_segs, tables=tables_s, tm=tm_s,
                                              stacked=kv_s)
        kv_s = (k_s, v_s)
        pad3 = lambda a: _pad_rows(a.reshape(dbsz, dseq, ATT_WIDTH), 2 * SUBLANES)
        xp, h_tail, y_att_s = _mix_out_prompt_moba_sample(
            xp, u, ag, y_att, bsz, seq, mixer_w, w_out_b, pad3(q_s), pad3(k_s[l]), pad3(v_s[l]),
            cache_k, cache_v, page_table, l)
        outs["p_pool"].append(p_pool)
        outs["p_conv"].append(h_tail[:, CONV_HALO - CONV_BUF:])

        y_att = y_att_s[:, :dseq].reshape(dbsz * dseq, ATT_WIDTH)
        tmaj = lambda a: jnp.swapaxes(a, 0, 1)
        u3 = u_s.reshape(dbsz, dseq, POOL_WIDTH)
        y_pool, y_conv, h_new = _poolconv_sample(
            tmaj(state_pool[l]), tmaj(u3), tmaj(state_conv[l]), tmaj(ag_s.reshape(dbsz, dseq, 2 * CONV_WIDTH)),
            past_len, mixer_w, l)
        bmaj = lambda a: jnp.swapaxes(a, 0, 1).reshape(dbsz * dseq, a.shape[-1])
        xs = _out_proj(xs, [bmaj(y_pool), bmaj(y_conv), y_att], w_out_b, l, out_rows, tm=tm_s)
        xp, xs = _cross(lnc, w_cq_b, cq_gain, w_co_b, l, xp.reshape(bsz, seq, D_MODEL), mk, mv, l * bsz, 512,
                        _pad_rows(xs.reshape(dbsz, dseq, D_MODEL), rows_pad), mem_k_s, mem_v_s, l * dbsz)
        xp = xp.reshape(bsz * seq, D_MODEL)
        xs = xs[:, :dseq].reshape(dbsz * dseq, D_MODEL)
        xp, xs = _ffn(xp, xs, ln2, w2i, w2o, l)
        outs["s_pool"].append(jnp.concatenate([state_pool[l][:, dseq:], u3], axis=1))
        outs["s_conv"].append(jnp.concatenate([state_conv[l][:, dseq:], jnp.swapaxes(h_new, 0, 1)], axis=1))

    st = lambda name: jnp.stack(outs[name])
    heads5 = lambda a, b, t: a.reshape(DEPTH, b, t, ATT_HEADS, ATT_HEAD_DIM)
    tiles = MEM_HEAD_DIM // LANES
    mem5 = lambda a: (a.reshape(DEPTH, bsz, N_MEM, tiles, MEM_HEADS, LANES).transpose(0, 1, 2, 4, 3, 5)
                      .reshape(DEPTH, bsz, N_MEM, MEM_HEADS, MEM_HEAD_DIM))
    return (xp.reshape(bsz, seq, D_MODEL), xs.reshape(dbsz, dseq, D_MODEL),
            heads5(kv_p[0], bsz, seq), heads5(kv_p[1], bsz, seq), mem5(mem_p[0]), mem5(mem_p[1]), st("p_pool"),
            st("p_conv"), heads5(kv_s[0], dbsz, dseq), heads5(kv_s[1], dbsz, dseq), st("s_pool"), st("s_conv"))
```

```python
import functools

import jax
import jax.numpy as jnp
import numpy as np
from jax import lax
from jax.experimental import pallas as pl
from jax.experimental.pallas import tpu as pltpu

D_MODEL = 1024
DEPTH = 2
PAGE_SIZE = 128

POOL_WIDTH = D_MODEL // 4
POOL_WINDOWS = (2, 4, 8, 16)
POOL_GROUP_WIDTH = POOL_WIDTH // len(POOL_WINDOWS)
POOL_BUF = max(POOL_WINDOWS) - 1

CONV_WIDTH = D_MODEL // 4
CONV_K = 31
CONV_BUF = CONV_K - 1

ATT_HEADS = 4
ATT_HEAD_DIM = D_MODEL // 8
ATT_WIDTH = ATT_HEADS * ATT_HEAD_DIM
ROT_DIM = ATT_HEAD_DIM // 4
ROPE_THETA = 500000.0
MOBA_BLOCK = 256
MOBA_TOPK = 3

N_MEM = 256
MEM_HEADS = 4
MEM_HEAD_DIM = D_MODEL // MEM_HEADS

D_FF = ((8 * D_MODEL // 3 + 127) // 128) * 128
EPS = 1e-6

LANES = 128
SUBLANES = 8
VMEM_LIMIT_BYTES = 56 * 1024 * 1024

F32 = jnp.float32
BF16 = jnp.bfloat16
NEG_INF = float("-inf")

_NT = (((1,), (1,)), ((), ()))


def _params(n_axes):
    return pltpu.CompilerParams(dimension_semantics=("arbitrary",) * n_axes,
                                vmem_limit_bytes=VMEM_LIMIT_BYTES)


def _rms(x, g):
    return x * lax.rsqrt(jnp.mean(x * x, axis=-1, keepdims=True) + EPS) * g


def _layer_spec(shape, l, n_grid):
    zeros = (0,) * len(shape)
    once = pl.Buffered(1)
    if n_grid == 1:
        return pl.BlockSpec((None,) + tuple(shape), lambda i: (l,) + zeros, pipeline_mode=once)
    if n_grid == 2:
        return pl.BlockSpec((None,) + tuple(shape), lambda i, j: (l,) + zeros, pipeline_mode=once)
    return pl.BlockSpec((None,) + tuple(shape), lambda i, j, k: (l,) + zeros, pipeline_mode=once)


FFN_CHUNK = 256


def _ffn_kernel(xa_ref, xb_ref, g_ref, wi_ref, wo_ref, oa_ref, ob_ref):
    i = pl.program_id(0)
    x = jnp.where(i == 0, xb_ref[...], xa_ref[...])
    xn = _rms(x, g_ref[...]).astype(BF16)
    acc = None
    for c in range(D_FF // FFN_CHUNK):
        lo = c * FFN_CHUNK
        gate = jnp.dot(xn, wi_ref[:, lo:lo + FFN_CHUNK], preferred_element_type=F32)
        up = jnp.dot(xn, wi_ref[:, D_FF + lo:D_FF + lo + FFN_CHUNK], preferred_element_type=F32)
        act = (gate * jax.nn.sigmoid(gate) * up).astype(BF16)
        part = jnp.dot(act, wo_ref[lo:lo + FFN_CHUNK, :], preferred_element_type=F32)
        acc = part if acc is None else acc + part
    oa_ref[...] = x + 0.5 * acc

    @pl.when(i == 0)
    def _():
        ob_ref[...] = oa_ref[...]


def _ffn(xa, xb, ln, w_in, w_out, l, tm=512):
    n_a = xa.shape[0] // tm
    assert xa.shape[0] == n_a * tm and xb.shape[0] == tm
    a_spec = pl.BlockSpec((tm, D_MODEL), lambda i: (jnp.maximum(i - 1, 0), 0))
    b_spec = pl.BlockSpec((tm, D_MODEL), lambda i: (0, 0))
    return pl.pallas_call(
        _ffn_kernel,
        grid=(n_a + 1,),
        in_specs=[
            a_spec, b_spec,
            _layer_spec((1, D_MODEL), l, 1),
            _layer_spec((D_MODEL, 2 * D_FF), l, 1),
            _layer_spec((D_FF, D_MODEL), l, 1),
        ],
        out_specs=[a_spec, b_spec],
        out_shape=[jax.ShapeDtypeStruct(xa.shape, F32), jax.ShapeDtypeStruct(xb.shape, F32)],
        compiler_params=_params(1),
        name="ffn",
    )(xa, xb, ln, w_in, w_out)


def _norm_proj_kernel(*refs, segs, rope):
    x_ref, g_ref, w_ref, hg_ref = refs[:4]
    if rope:
        cos_ref, sin_lo_ref, sin_hi_ref = refs[4:7]
    out_refs = refs[len(refs) - len(segs):]
    tm = x_ref.shape[0]
    xn = _rms(x_ref[...], g_ref[...]).astype(BF16)
    for (c0, width, head_dim, gain_row, use_rope, head_rows), o_ref in zip(segs, out_refs):
        z = jnp.dot(xn, w_ref[:, c0:c0 + width], preferred_element_type=F32)
        if head_dim is None:
            o_ref[...] = z
            continue
        n_heads = width // head_dim
        for h in range(n_heads):
            zh = z[:, h * head_dim:(h + 1) * head_dim]
            if gain_row is not None:
                zh = _rms(zh, hg_ref[gain_row:gain_row + 1, :])
            if use_rope:
                half = ROT_DIM // 2
                zh = (zh * cos_ref[...]
                      + pltpu.roll(zh, half, 1) * sin_hi_ref[...]
                      + pltpu.roll(zh, LANES - half, 1) * sin_lo_ref[...])
            if not head_rows:
                o_ref[:, h * head_dim:(h + 1) * head_dim] = zh
                continue
            tiles = head_dim // LANES
            for tile in range(tiles):
                rows = pl.ds(tile * n_heads + h, tm, stride=n_heads * tiles)
                piece = zh[:, tile * LANES:(tile + 1) * LANES]
                if len(o_ref.shape) == 2:
                    o_ref[rows, :] = piece
                else:
                    for d in range(o_ref.shape[0]):
                        o_ref[d, rows, :] = piece


def _norm_proj(x, ln, w, l, head_gains, segs, tables=None, tm=512, stacked=None):
    m = x.shape[0]
    n_cols = w.shape[-1]
    rope = tables is not None
    in_specs = [
        pl.BlockSpec((tm, D_MODEL), lambda i: (i, 0)),
        _layer_spec((1, D_MODEL), l, 1),
        _layer_spec((D_MODEL, n_cols), l, 1),
        pl.BlockSpec(head_gains.shape, lambda i: (0, 0)),
    ]
    args = [x, ln, w, head_gains]
    if rope:
        period = tables[0].shape[0] // tm
        for t in tables:
            in_specs.append(pl.BlockSpec((tm, LANES), lambda i: (i % period, 0)))
            args.append(t)
    out_specs, out_shape, aliases = [], [], {}
    for k, s in enumerate(segs):
        width, head_dim, head_rows = s[1], s[2], s[5]
        if not head_rows:
            out_specs.append(pl.BlockSpec((tm, width), lambda i: (i, 0)))
            out_shape.append(jax.ShapeDtypeStruct((m, width), F32))
            continue
        per_token = width // LANES
        out_shape.append(jax.ShapeDtypeStruct((DEPTH, m * per_token, LANES), F32))
        if stacked is None:
            out_specs.append(pl.BlockSpec((DEPTH, tm * per_token, LANES), lambda i: (0, i, 0)))
        else:
            out_specs.append(pl.BlockSpec((None, tm * per_token, LANES), lambda i: (l, i, 0)))
            aliases[len(args)] = k
            in_specs.append(pl.BlockSpec(memory_space=pl.ANY))
            args.append(stacked[len(aliases) - 1])
    return pl.pallas_call(
        functools.partial(_norm_proj_kernel, segs=tuple(segs), rope=rope),
        grid=(m // tm,),
        in_specs=in_specs,
        out_specs=out_specs,
        out_shape=out_shape,
        input_output_aliases=aliases,
        compiler_params=_params(1),
        name="norm_proj",
    )(*args)


def _rope_tables(positions):
    half = ROT_DIM // 2
    inv_freq = jnp.power(jnp.float32(ROPE_THETA), -jnp.arange(half, dtype=F32) / half)
    ang = positions.astype(F32)[:, None] * inv_freq[None, :]
    cos, sin = jnp.cos(ang), jnp.sin(ang)
    n = positions.shape[0]
    rest = ATT_HEAD_DIM - ROT_DIM
    cos_t = jnp.concatenate([cos, cos, jnp.ones((n, rest), F32)], axis=1)
    sin_lo = jnp.concatenate([-sin, jnp.zeros((n, half + rest), F32)], axis=1)
    sin_hi = jnp.concatenate([jnp.zeros((n, half), F32), sin, jnp.zeros((n, rest), F32)], axis=1)
    return cos_t, sin_lo, sin_hi


def _out_proj_kernel(*refs, n_terms):
    x_ref = refs[0]
    y_refs = refs[1:1 + n_terms]
    w_refs = refs[1 + n_terms:1 + 2 * n_terms]
    o_ref = refs[1 + 2 * n_terms]
    acc = x_ref[...]
    for y_ref, w_ref in zip(y_refs, w_refs):
        acc = acc + jnp.dot(y_ref[...].astype(BF16), w_ref[...], preferred_element_type=F32)
    o_ref[...] = acc


def _out_proj(x, ys, w, l, row_starts, tm=512):
    m = x.shape[0]
    in_specs = [pl.BlockSpec((tm, D_MODEL), lambda i: (i, 0))]
    for y in ys:
        in_specs.append(pl.BlockSpec((tm, y.shape[1]), lambda i: (i, 0)))
    for y, r0 in zip(ys, row_starts):
        width = y.shape[1]
        blk = r0 // width
        in_specs.append(pl.BlockSpec((None, width, D_MODEL), lambda i, blk=blk: (l, blk, 0)))
    return pl.pallas_call(
        functools.partial(_out_proj_kernel, n_terms=len(ys)),
        grid=(m // tm,),
        in_specs=in_specs,
        out_specs=pl.BlockSpec((tm, D_MODEL), lambda i: (i, 0)),
        out_shape=jax.ShapeDtypeStruct((m, D_MODEL), F32),
        compiler_params=_params(1),
        name="out_proj",
    )(x, *ys, *([w] * len(ys)))


def _select_blocks(gate, n_allowed):
    lane = lax.broadcasted_iota(jnp.int32, gate.shape, 1)
    allowed = lane < n_allowed
    g = jnp.where(allowed, gate, NEG_INF)
    rank = jnp.zeros(gate.shape, F32)
    for r in range(1, SUBLANES):
        lower = pltpu.roll(g, r, 1)
        higher = pltpu.roll(g, LANES - r, 1)
        rank = rank + jnp.where(lower >= g, 1.0, 0.0) + jnp.where(higher > g, 1.0, 0.0)
    return jnp.where(allowed, jnp.where(rank < MOBA_TOPK, 1.0, 0.0), 0.0)


def _dot_nt_split(a, b):
    a_hi = a.astype(BF16)
    a_lo = (a - a_hi.astype(F32)).astype(BF16)
    b_hi = b.astype(BF16)
    b_lo = (b - b_hi.astype(F32)).astype(BF16)
    m = a.shape[0]
    by_hi = lax.dot_general(jnp.concatenate([a_hi, a_lo], axis=0), b_hi, _NT, preferred_element_type=F32)
    return by_hi[0:m] + by_hi[m:2 * m] + lax.dot_general(a_hi, b_lo, _NT, preferred_element_type=F32)


def _block_expander(n_keys):
    blk = np.arange(n_keys) // MOBA_BLOCK
    return jnp.asarray((np.arange(LANES)[:, None] == blk[None, :]).astype(np.float32), dtype=BF16)


MASK_BIAS = -1e30


def _select_blocks_t(gate_t, n_allowed):
    blk = lax.broadcasted_iota(jnp.int32, gate_t.shape, 0)
    allowed = blk < n_allowed
    g = jnp.where(allowed, gate_t, NEG_INF)
    rank = jnp.zeros(gate_t.shape, F32)
    for m in range(SUBLANES):
        gm = g[m:m + 1, :]
        tie = jnp.where(gm == g, jnp.where(blk > m, 1.0, 0.0), 0.0)
        rank = rank + jnp.where(gm > g, 1.0, tie)
    return jnp.where(allowed, jnp.where(rank < MOBA_TOPK, 1.0, 0.0), 0.0)


def _moba_prompt_kernel(q_ref, k_ref, v_ref, kbias_ref, o_ref, ka_ref, vb_ref, km_ref, *, n_blocks, per_step):
    qi = pl.program_id(1)
    blk = MOBA_BLOCK
    dh = ATT_HEAD_DIM
    heads = range(ATT_HEADS)

    @pl.when(qi == 0)
    def _():
        km_ref[...] = jnp.zeros(km_ref.shape, F32)
        for h in heads:
            vb_ref[:, h * dh:(h + 1) * dh] = v_ref[pl.ds(h, n_blocks * blk, stride=ATT_HEADS), :].astype(BF16)
            ka_ref[h, :, dh:2 * dh] = kbias_ref[h]
            for n in range(n_blocks):
                kb = k_ref[pl.ds(n * blk * ATT_HEADS + h, blk, stride=ATT_HEADS), :]
                ka_ref[h, n * blk:(n + 1) * blk, 0:dh] = kb.astype(BF16)
                r = h * SUBLANES + n
                km_ref[r:r + 1, h * dh:(h + 1) * dh] = jnp.mean(kb, axis=0, keepdims=True)

    row = lax.broadcasted_iota(jnp.int32, (blk, blk), 0)
    col = lax.broadcasted_iota(jnp.int32, (blk, blk), 1)

    def step(c, r0):
        n_keys = (c + 1) * blk
        gated = c > MOBA_TOPK
        if gated:
            gate_all = _dot_nt_split(km_ref[...], q_ref[r0:r0 + blk, :])
            n_rows = ATT_HEADS * SUBLANES
            flags = []
            for h in heads:
                gate_t = gate_all[h * SUBLANES:(h + 1) * SUBLANES]
                blk_id = lax.broadcasted_iota(jnp.int32, gate_t.shape, 0)
                flags.append(jnp.where(blk_id == c, 0.0, 1.0 - _select_blocks_t(gate_t, c)))
            flags.append(jnp.zeros((LANES - n_rows, blk), F32))
            unchosen = jnp.concatenate(flags, axis=0).T.astype(BF16)
        for h in heads:
            q = q_ref[r0:r0 + blk, h * dh:(h + 1) * dh]
            qs = (q * (dh ** -0.5)).astype(BF16)
            if gated:
                qa = jnp.concatenate([qs, unchosen], axis=1)
                s = lax.dot_general(qa, ka_ref[h, 0:n_keys, :], _NT, preferred_element_type=F32)
            else:
                s = lax.dot_general(qs, ka_ref[h, 0:n_keys, 0:dh], _NT, preferred_element_type=F32)
            parts = [s[:, 0:c * blk]] if c else []
            parts.append(jnp.where(row >= col, s[:, c * blk:n_keys], NEG_INF))
            s = jnp.concatenate(parts, axis=1)
            p = jnp.exp(s - jnp.max(s, axis=-1, keepdims=True))
            inv = 1.0 / jnp.sum(p, axis=-1, keepdims=True)
            o = jnp.dot(p.astype(BF16), vb_ref[0:n_keys, h * dh:(h + 1) * dh], preferred_element_type=F32)
            o_ref[r0:r0 + blk, h * dh:(h + 1) * dh] = o * inv

    def tile(j):
        for r in range(per_step):
            step(j * per_step + r, r * blk)

    for j in range(n_blocks // per_step):
        pl.when(qi == j)(functools.partial(tile, j))


def _moba_prompt(q, k, v, bsz, t, l):
    assert t % MOBA_BLOCK == 0 and t // MOBA_BLOCK <= SUBLANES
    n_blocks = t // MOBA_BLOCK
    dh = ATT_HEAD_DIM
    key_blk = np.arange(t) // MOBA_BLOCK
    lane_of = np.arange(ATT_HEADS)[:, None, None] * SUBLANES + key_blk[None, :, None]
    kbias = jnp.asarray(np.where(lane_of == np.arange(LANES)[None, None, :], MASK_BIAS, 0.0), dtype=BF16)
    per_step = 2 if n_blocks % 2 == 0 else 1
    nt = n_blocks // per_step
    tile_spec = pl.BlockSpec((per_step * MOBA_BLOCK, ATT_WIDTH), lambda b, i: (b * nt + i, 0))
    return pl.pallas_call(
        functools.partial(_moba_prompt_kernel, n_blocks=n_blocks, per_step=per_step),
        grid=(bsz, nt),
        in_specs=[
            tile_spec,
            pl.BlockSpec((None, t * ATT_HEADS, dh), lambda b, i: (l, b, 0)),
            pl.BlockSpec((None, t * ATT_HEADS, dh), lambda b, i: (l, b, 0)),
            pl.BlockSpec((ATT_HEADS, t, LANES), lambda b, i: (0, 0, 0)),
        ],
        out_specs=tile_spec,
        out_shape=jax.ShapeDtypeStruct((bsz * t, ATT_WIDTH), F32),
        scratch_shapes=[
            pltpu.VMEM((ATT_HEADS, t, 2 * dh), BF16),
            pltpu.VMEM((t, ATT_WIDTH), BF16),
            pltpu.VMEM((ATT_HEADS * SUBLANES, ATT_WIDTH), F32),
        ],
        compiler_params=_params(2),
        name="moba_prompt",
    )(q, k, v, kbias)


def _moba_sample_kernel(pt_ref, q_ref, kn_ref, vn_ref, e_ref, *refs, n_pages, n_seq):
    del pt_ref
    k_refs = refs[:n_seq * n_pages]
    v_refs = refs[n_seq * n_pages:2 * n_seq * n_pages]
    o_ref, tok_ref = refs[2 * n_seq * n_pages:]
    n_new = q_ref.shape[1]
    rows = tok_ref.shape[2]
    tok_ref[...] = jnp.zeros(tok_ref.shape, F32)
    tok_ref[0, :, 0:n_new, :] = q_ref[...]
    tok_ref[1, :, 0:n_new, :] = kn_ref[...]
    tok_ref[2, :, 0:n_new, :] = vn_ref[...]
    q_ref, kn_ref, vn_ref = tok_ref.at[0], tok_ref.at[1], tok_ref.at[2]
    pages_per_block = MOBA_BLOCK // PAGE_SIZE
    n_blocks = n_pages // pages_per_block
    dh = ATT_HEAD_DIM
    pairs = [(g, h) for g in range(n_seq) for h in range(ATT_HEADS)]

    def block_rows(page_refs, g, n, h):
        first = g * n_pages + n * pages_per_block
        return jnp.concatenate(
            [page_refs[j][0, pl.ds(h, PAGE_SIZE, stride=ATT_HEADS), :] for j in range(first, first + pages_per_block)],
            axis=0)

    qs, gate_rows, s_rows = [], [], []
    for g, h in pairs:
        q = q_ref[g, :, h * dh:(h + 1) * dh]
        qs.append((q * (dh ** -0.5)).astype(BF16))
        mean_rows, s_parts = [], []
        for n in range(n_blocks):
            kb = block_rows(k_refs, g, n, h)
            mean_rows.append(jnp.sum(kb, axis=0, keepdims=True) / MOBA_BLOCK)
            s_parts.append(lax.dot_general(qs[-1], kb.astype(BF16), _NT, preferred_element_type=F32))
        mean_rows.append(jnp.zeros((LANES - n_blocks, dh), F32))
        gate_rows.append(_dot_nt_split(q, jnp.concatenate(mean_rows, axis=0)))
        s_rows.append(jnp.concatenate(s_parts, axis=1))
    sel = _select_blocks(jnp.concatenate(gate_rows, axis=0), n_blocks)
    sel_keys = jnp.dot(sel.astype(BF16), e_ref[...], preferred_element_type=F32)
    s = jnp.where(sel_keys > 0.5, jnp.concatenate(s_rows, axis=0), NEG_INF)

    pad = jnp.zeros((LANES - rows, dh), F32)
    padded = lambda ref, g, h: jnp.concatenate([ref[g, :, h * dh:(h + 1) * dh], pad], axis=0).astype(BF16)
    s_own = jnp.concatenate(
        [lax.dot_general(qs[i], padded(kn_ref, g, h), _NT, preferred_element_type=F32)
         for i, (g, h) in enumerate(pairs)], axis=0)
    t_row = lax.broadcasted_iota(jnp.int32, s_own.shape, 0) % rows
    key = lax.broadcasted_iota(jnp.int32, s_own.shape, 1)
    s_own = jnp.where(key <= t_row, s_own, NEG_INF)

    m = jnp.maximum(jnp.max(s, axis=-1, keepdims=True), jnp.max(s_own, axis=-1, keepdims=True))
    p = jnp.exp(s - m)
    p_own = jnp.exp(s_own - m)
    inv = 1.0 / (jnp.sum(p, axis=-1, keepdims=True) + jnp.sum(p_own, axis=-1, keepdims=True))
    pb = p.astype(BF16)
    pb_own = p_own.astype(BF16)
    outs = []
    for i, (g, h) in enumerate(pairs):
        r0 = i * rows
        o = jnp.dot(pb_own[r0:r0 + rows], padded(vn_ref, g, h), preferred_element_type=F32)
        for n in range(n_blocks):
            o = o + jnp.dot(pb[r0:r0 + rows, n * MOBA_BLOCK:(n + 1) * MOBA_BLOCK],
                            block_rows(v_refs, g, n, h).astype(BF16), preferred_element_type=F32)
        outs.append(o * inv[r0:r0 + rows])
    for g in range(n_seq):
        o_ref[g] = jnp.concatenate(outs[g * ATT_HEADS:(g + 1) * ATT_HEADS], axis=1)[0:n_new]


def _pool_windows_to_lanes(sums, counts, cur):
    lane = lax.broadcasted_iota(jnp.int32, cur.shape, 1)
    d = sums[-1] / counts[-1]
    for g in range(len(POOL_WINDOWS) - 2, -1, -1):
        d = jnp.where(lane < (g + 1) * POOL_GROUP_WIDTH, sums[g] / counts[g], d)
    return d - cur


def _conv_tail(y, ln_g, ln_b, pw_w, pw_b):
    yc = y - jnp.mean(y, axis=-1, keepdims=True)
    yn = yc * lax.rsqrt(jnp.mean(yc * yc, axis=-1, keepdims=True) + EPS) * ln_g + ln_b
    act = yn * jax.nn.sigmoid(yn)
    return jnp.dot(act.astype(BF16), pw_w, preferred_element_type=F32) + pw_b


POOL_HALO = 16
CONV_HALO = 32


def _poolconv_prompt_kernel(x_ref, u_ref, ag_ref, ya_ref, pw_ref, ps_ref, dw_ref, db_ref, lg_ref, lb_ref,
                            cw_ref, cb_ref, wp_ref, wc_ref, wa_ref, o_ref, ht_ref, e_ref, h_ref, hs_ref, *, tt,
                            side_work=None):
    ti = pl.program_id(1)

    @pl.when(ti == 0)
    def _():
        e_ref[0:POOL_HALO, :] = jnp.zeros((POOL_HALO, POOL_WIDTH), F32)
        h_ref[0:CONV_HALO, :] = jnp.zeros((CONV_HALO, CONV_WIDTH), F32)

    if side_work is not None:
        side_work()

    u = u_ref[...]
    e_ref[POOL_HALO:POOL_HALO + tt, :] = u
    a = ag_ref[:, 0:CONV_WIDTH]
    gate = ag_ref[:, CONV_WIDTH:2 * CONV_WIDTH]
    h_ref[CONV_HALO:CONV_HALO + tt, :] = a * jax.nn.sigmoid(gate)

    pos = ti * tt + lax.broadcasted_iota(jnp.int32, (tt, 1), 0)
    sums, counts = [], []
    run = u
    taken = 1
    for win in POOL_WINDOWS:
        while taken < win:
            run = run + e_ref[POOL_HALO - taken:POOL_HALO - taken + tt, :]
            taken += 1
        sums.append(run)
        counts.append(jnp.minimum(pos + 1, win).astype(F32))
    d = _pool_windows_to_lanes(sums, counts, u)
    y_pool = jnp.dot(d.astype(BF16), pw_ref[...], preferred_element_type=F32) * ps_ref[...]

    span = tt + CONV_HALO - SUBLANES
    for s in range(1, SUBLANES):
        hs_ref[s - 1, 0:span, :] = h_ref[s:s + span, :]
    y = jnp.zeros((tt, CONV_WIDTH), F32) + db_ref[...]
    for j in range(CONV_K):
        r0 = CONV_HALO - CONV_BUF + j
        s = r0 % SUBLANES
        rows = h_ref[r0:r0 + tt, :] if s == 0 else hs_ref[s - 1, r0 - s:r0 - s + tt, :]
        y = y + rows * dw_ref[j:j + 1, :]
    y_conv = _conv_tail(y, lg_ref[...], lb_ref[...], cw_ref[...], cb_ref[...])

    out = x_ref[...] + jnp.dot(y_pool.astype(BF16), wp_ref[...], preferred_element_type=F32)
    out = out + jnp.dot(y_conv.astype(BF16), wc_ref[...], preferred_element_type=F32)
    o_ref[...] = out + jnp.dot(ya_ref[...].astype(BF16), wa_ref[...], preferred_element_type=F32)

    ht_ref[0] = h_ref[tt:tt + CONV_HALO, :]
    e_ref[0:POOL_HALO, :] = e_ref[tt:tt + POOL_HALO, :]
    h_ref[0:CONV_HALO, :] = h_ref[tt:tt + CONV_HALO, :]


N_MIX_IN = 15


def _mix_out_moba_kernel(pt_ref, *refs, tt, n_pages, n_seq):
    n_moba_in = 4 + 2 * n_seq * n_pages
    mix_in = refs[:N_MIX_IN]
    moba_in = refs[N_MIX_IN:N_MIX_IN + n_moba_in]
    o_ref, ht_ref, att_ref = refs[N_MIX_IN + n_moba_in:N_MIX_IN + n_moba_in + 3]
    *mix_scratch, tok_ref = refs[N_MIX_IN + n_moba_in + 3:]
    side = functools.partial(_moba_sample_kernel, pt_ref, *moba_in, att_ref, tok_ref, n_pages=n_pages,
                             n_seq=n_seq)
    _poolconv_prompt_kernel(*mix_in, o_ref, ht_ref, *mix_scratch, tt=tt, side_work=side)


def _mix_out_prompt_moba_sample(x, u, ag, y_att, bsz, t, weights, w_out, q_s, k_new, v_new, cache_k, cache_v,
                                page_table, l, tt=256):
    pool_w, pool_s, dw_w, dw_b, ln_g, ln_b, pw_w, pw_b = weights
    nt = t // tt
    dbsz, rows, _ = q_s.shape
    n_seq = dbsz // (bsz * nt)
    assert n_seq * bsz * nt == dbsz
    n_pages = page_table.shape[1]
    n_phys = cache_k.shape[0] // DEPTH
    past_len = n_pages * PAGE_SIZE
    assert past_len % MOBA_BLOCK == 0 and past_len // MOBA_BLOCK <= SUBLANES
    base = l * n_phys
    row = lambda b, i, pt: (b * nt + i, 0)
    w_rows = lambda width, blk: pl.BlockSpec((None, width, D_MODEL), lambda b, i, pt: (l, blk, 0))
    tok_spec = pl.BlockSpec((n_seq, rows, ATT_WIDTH), lambda b, i, pt: (b * nt + i, 0, 0))
    page_specs = [
        pl.BlockSpec((1, PAGE_SIZE * ATT_HEADS, ATT_HEAD_DIM),
                     lambda b, i, pt, g=g, j=j: (base + pt[((b * nt + i) * n_seq + g) * n_pages + j], 0, 0))
        for g in range(n_seq) for j in range(n_pages)
    ]
    grid_spec = pltpu.PrefetchScalarGridSpec(
        num_scalar_prefetch=1,
        grid=(bsz, nt),
        in_specs=[
            pl.BlockSpec((tt, D_MODEL), row),
            pl.BlockSpec((tt, POOL_WIDTH), row),
            pl.BlockSpec((tt, 2 * CONV_WIDTH), row),
            pl.BlockSpec((tt, ATT_WIDTH), row),
            _layer_spec((POOL_WIDTH, POOL_WIDTH), l, 3),
            _layer_spec((1, POOL_WIDTH), l, 3),
            _layer_spec((CONV_HALO, CONV_WIDTH), l, 3),
            _layer_spec((1, CONV_WIDTH), l, 3),
            _layer_spec((1, CONV_WIDTH), l, 3),
            _layer_spec((1, CONV_WIDTH), l, 3),
            _layer_spec((CONV_WIDTH, CONV_WIDTH), l, 3),
            _layer_spec((1, CONV_WIDTH), l, 3),
            w_rows(POOL_WIDTH, 0),
            w_rows(CONV_WIDTH, POOL_WIDTH // CONV_WIDTH),
            w_rows(ATT_WIDTH, (POOL_WIDTH + CONV_WIDTH) // ATT_WIDTH),
            tok_spec, tok_spec, tok_spec,
            pl.BlockSpec((LANES, past_len), lambda b, i, pt: (0, 0)),
        ] + page_specs + page_specs,
        out_specs=[
            pl.BlockSpec((tt, D_MODEL), row),
            pl.BlockSpec((1, CONV_HALO, CONV_WIDTH), lambda b, i, pt: (b, 0, 0)),
            tok_spec,
        ],
        scratch_shapes=[
            pltpu.VMEM((POOL_HALO + tt, POOL_WIDTH), F32),
            pltpu.VMEM((CONV_HALO + tt, CONV_WIDTH), F32),
            pltpu.VMEM((SUBLANES - 1, CONV_HALO + tt, CONV_WIDTH), F32),
            pltpu.VMEM((3, n_seq, 2 * SUBLANES, ATT_WIDTH), F32),
        ],
    )
    return pl.pallas_call(
        functools.partial(_mix_out_moba_kernel, tt=tt, n_pages=n_pages, n_seq=n_seq),
        grid_spec=grid_spec,
        out_shape=[
            jax.ShapeDtypeStruct((bsz * t, D_MODEL), F32),
            jax.ShapeDtypeStruct((bsz, CONV_HALO, CONV_WIDTH), F32),
            jax.ShapeDtypeStruct((dbsz, rows, ATT_WIDTH), F32),
        ],
        compiler_params=_params(2),
        name="mix_out_moba",
    )(page_table.reshape(-1), x, u, ag, y_att, pool_w, pool_s, dw_w, dw_b, ln_g, ln_b, pw_w, pw_b,
      w_out, w_out, w_out, q_s, k_new, v_new, _block_expander(past_len),
      *([cache_k] * (n_seq * n_pages)), *([cache_v] * (n_seq * n_pages)))


def _poolconv_sample_kernel(sp_ref, u_ref, sc_ref, ag_ref, pw_ref, ps_ref, dw_ref, db_ref, lg_ref, lb_ref,
                            cw_ref, cb_ref, yp_ref, yc_ref, h_ref, *, n_new, pos0):
    def pool_row(r):
        return sp_ref[r] if r < POOL_BUF else u_ref[r - POOL_BUF]

    for t in range(n_new):
        h_ref[t] = ag_ref[t, :, 0:CONV_WIDTH] * jax.nn.sigmoid(ag_ref[t, :, CONV_WIDTH:2 * CONV_WIDTH])

    def conv_row(r):
        return sc_ref[r] if r < CONV_BUF else h_ref[r - CONV_BUF]

    for t in range(n_new):
        cur = u_ref[t]
        sums, counts = [], []
        run = cur
        taken = 1
        for win in POOL_WINDOWS:
            while taken < win:
                run = run + pool_row(POOL_BUF + t - taken)
                taken += 1
            sums.append(run)
            counts.append(float(min(pos0 + t + 1, win)))
        d = _pool_windows_to_lanes(sums, counts, cur)
        yp_ref[t] = jnp.dot(d.astype(BF16), pw_ref[...], preferred_element_type=F32) * ps_ref[...]

        y = jnp.zeros(cur.shape, F32) + db_ref[...]
        for j in range(CONV_K):
            y = y + conv_row(t + j) * dw_ref[j:j + 1, :]
        yc_ref[t] = _conv_tail(y, lg_ref[...], lb_ref[...], cw_ref[...], cb_ref[...])


def _poolconv_sample(state_pool_t, u_t, state_conv_t, ag_t, pos0, weights, l):
    pool_w, pool_s, dw_w, dw_b, ln_g, ln_b, pw_w, pw_b = weights
    n_new, bsz, _ = u_t.shape
    full = lambda a: pl.BlockSpec(a.shape, lambda i: (0,) * a.ndim)
    out = jax.ShapeDtypeStruct((n_new, bsz, CONV_WIDTH), F32)
    return pl.pallas_call(
        functools.partial(_poolconv_sample_kernel, n_new=n_new, pos0=pos0),
        grid=(1,),
        in_specs=[
            full(state_pool_t), full(u_t), full(state_conv_t), full(ag_t),
            _layer_spec((POOL_WIDTH, POOL_WIDTH), l, 1),
            _layer_spec((1, POOL_WIDTH), l, 1),
            _layer_spec((CONV_HALO, CONV_WIDTH), l, 1),
            _layer_spec((1, CONV_WIDTH), l, 1),
            _layer_spec((1, CONV_WIDTH), l, 1),
            _layer_spec((1, CONV_WIDTH), l, 1),
            _layer_spec((CONV_WIDTH, CONV_WIDTH), l, 1),
            _layer_spec((1, CONV_WIDTH), l, 1),
        ],
        out_specs=[pl.BlockSpec(out.shape, lambda i: (0, 0, 0))] * 3,
        out_shape=[out, out, out],
        compiler_params=_params(1),
        name="poolconv_sample",
    )(state_pool_t, u_t, state_conv_t, ag_t, pool_w, pool_s, dw_w, dw_b, ln_g, ln_b, pw_w, pw_b)


def _cross_kernel(g_ref, wq_ref, hg_ref, wo_ref, xa_ref, ka_ref, va_ref, xb_ref, kb_ref, vb_ref, oa_ref, ob_ref,
                  xb_pad_ref):
    weights = (g_ref, wq_ref, hg_ref, wo_ref)
    n_new = xb_ref.shape[1]
    xb_pad_ref[...] = jnp.zeros(xb_pad_ref.shape, F32)
    xb_pad_ref[:, 0:n_new, :] = xb_ref[...]
    ob_ref[...] = _cross_rows(xb_pad_ref, kb_ref, vb_ref, *weights)[:, 0:n_new, :]
    oa_ref[...] = _cross_rows(xa_ref, ka_ref, va_ref, *weights)


def _cross_rows(x_ref, k_ref, v_ref, g_ref, wq_ref, hg_ref, wo_ref):
    hd = MEM_HEAD_DIM
    n_seq, rows, _ = x_ref.shape
    x = x_ref[...].reshape(n_seq * rows, D_MODEL)
    xn = _rms(x, g_ref[...]).astype(BF16)
    q_all = jnp.dot(xn, wq_ref[...], preferred_element_type=F32)
    halves = range(hd // LANES)

    def head_slab(ref, g, h):
        return jnp.concatenate(
            [ref[g, pl.ds(half * MEM_HEADS + h, N_MEM, stride=len(halves) * MEM_HEADS), :] for half in halves],
            axis=1)

    pairs = [(g, h) for g in range(n_seq) for h in range(MEM_HEADS)]
    scores = []
    for g, h in pairs:
        q = _rms(q_all[g * rows:(g + 1) * rows, h * hd:(h + 1) * hd], hg_ref[...])
        q = (q * (hd ** -0.5)).astype(BF16)
        scores.append(lax.dot_general(q, head_slab(k_ref, g, h).astype(BF16), _NT, preferred_element_type=F32))
    s = jnp.concatenate(scores, axis=0)
    p = jnp.exp(s - jnp.max(s, axis=-1, keepdims=True))
    inv = 1.0 / jnp.sum(p, axis=-1, keepdims=True)
    o_seqs = []
    for g in range(n_seq):
        o_heads = []
        for h in range(MEM_HEADS):
            r0 = (g * MEM_HEADS + h) * rows
            o = jnp.dot(p[r0:r0 + rows].astype(BF16), head_slab(v_ref, g, h).astype(BF16),
                        preferred_element_type=F32)
            o_heads.append((o * inv[r0:r0 + rows]).astype(BF16))
        o_seqs.append(jnp.concatenate(o_heads, axis=1))
    o_all = o_seqs[0] if n_seq == 1 else jnp.concatenate(o_seqs, axis=0)
    y = x + jnp.dot(o_all, wo_ref[...], preferred_element_type=F32)
    return y.reshape(n_seq, rows, D_MODEL)


def _interleaved_memory(cache):
    depth, s, n_mem, heads, hd = cache.shape
    halves = hd // LANES
    c = cache.reshape(depth, s, n_mem, heads, halves, LANES).transpose(0, 1, 2, 4, 3, 5)
    return c.reshape(depth * s, n_mem * halves * heads, LANES)


def _cross(ln, w_q, head_gain, w_o, l, xa, mem_ka, mem_va, seq0_a, tq, xb, mem_kb, mem_vb, seq0_b):
    sa, ra, _ = xa.shape
    sb, rb, _ = xb.shape
    nt = ra // tq
    n_b = sb // (sa * nt)
    assert n_b * sa * nt == sb and seq0_b % n_b == 0
    xa_spec = pl.BlockSpec((1, tq, D_MODEL), lambda s, i: (s, i, 0))
    kva_spec = pl.BlockSpec((1,) + mem_ka.shape[1:], lambda s, i: (seq0_a + s, 0, 0))
    xb_spec = pl.BlockSpec((n_b, rb, D_MODEL), lambda s, i: (s * nt + i, 0, 0))
    kvb_spec = pl.BlockSpec((n_b,) + mem_kb.shape[1:], lambda s, i: (seq0_b // n_b + s * nt + i, 0, 0))
    return pl.pallas_call(
        _cross_kernel,
        grid=(sa, nt),
        in_specs=[_layer_spec((1, D_MODEL), l, 2), _layer_spec((D_MODEL, D_MODEL), l, 2),
                  _layer_spec((1, MEM_HEAD_DIM), l, 2), _layer_spec((D_MODEL, D_MODEL), l, 2),
                  xa_spec, kva_spec, kva_spec, xb_spec, kvb_spec, kvb_spec],
        out_specs=[xa_spec, xb_spec],
        out_shape=[jax.ShapeDtypeStruct(xa.shape, F32), jax.ShapeDtypeStruct(xb.shape, F32)],
        scratch_shapes=[pltpu.VMEM((n_b, -(-rb // SUBLANES) * SUBLANES, D_MODEL), F32)],
        compiler_params=_params(2),
        name="cross",
    )(ln, w_q, head_gain, w_o, xa, mem_ka, mem_va, xb, mem_kb, mem_vb)


def kernel(x_prompt, x_sample, cache_att_k, cache_att_v, cache_mem_k, cache_mem_v, state_pool, state_conv, page_table, mem_prompt, ln_ffn1, w_ffn1_in, w_ffn1_out, ln_mix, w_in, pool_w, pool_scale, conv_dw_w, conv_dw_b, conv_ln_g, conv_ln_b, conv_pw_w, conv_pw_b, att_q_norm, att_k_norm, w_out, ln_cross, ln_mem, w_cq, w_ckv, cq_norm, ck_norm, w_co, ln_ffn2, w_ffn2_in, w_ffn2_out):
    bsz, seq, _ = x_prompt.shape
    dbsz, dseq, _ = x_sample.shape
    n_pages = page_table.shape[1]
    past_len = n_pages * PAGE_SIZE
    n_phys = cache_att_k.shape[1]

    bf = lambda w: w.astype(BF16)
    rowv = lambda v: v[:, None, :]
    w1i, w1o, w2i, w2o = bf(w_ffn1_in), bf(w_ffn1_out), bf(w_ffn2_in), bf(w_ffn2_out)
    w_in_b, w_out_b, w_cq_b, w_ckv_b, w_co_b = bf(w_in), bf(w_out), bf(w_cq), bf(w_ckv), bf(w_co)
    groups = len(POOL_WINDOWS)
    eye = jnp.eye(groups, dtype=F32)
    pool_bd = bf((pool_w[:, :, :, None, :] * eye[None, :, None, :, None])
                 .reshape(DEPTH, POOL_WIDTH, POOL_WIDTH))
    dw_pad = jnp.pad(conv_dw_w, ((0, 0), (0, CONV_HALO - CONV_K), (0, 0)))
    mixer_w = (pool_bd, rowv(pool_scale), dw_pad, rowv(conv_dw_b), rowv(conv_ln_g), rowv(conv_ln_b),
               bf(conv_pw_w), rowv(conv_pw_b))
    ln1, lnm, lnc, lnmem, ln2 = rowv(ln_ffn1), rowv(ln_mix), rowv(ln_cross), rowv(ln_mem), rowv(ln_ffn2)

    tables_p = _rope_tables(jnp.arange(seq, dtype=jnp.int32))
    tm_s = dbsz * dseq
    tables_s = _rope_tables(past_len + jnp.arange(tm_s, dtype=jnp.int32) % dseq)

    c1 = POOL_WIDTH
    c3 = c1 + 2 * CONV_WIDTH
    c4 = c3 + ATT_WIDTH
    c5 = c4 + ATT_WIDTH
    mix_segs = [(c3, ATT_WIDTH, ATT_HEAD_DIM, 0, True, False), (c4, ATT_WIDTH, ATT_HEAD_DIM, 1, True, True),
                (c5, ATT_WIDTH, ATT_HEAD_DIM, None, False, True),
                (0, c1, None, None, False, False), (c1, 2 * CONV_WIDTH, None, None, False, False)]
    mem_segs = [(0, D_MODEL, MEM_HEAD_DIM, 0, False, True), (D_MODEL, D_MODEL, MEM_HEAD_DIM, None, False, True)]
    out_rows = (0, c1, c1 + CONV_WIDTH)

    cache_k = cache_att_k.reshape(DEPTH * n_phys, PAGE_SIZE * ATT_HEADS, ATT_HEAD_DIM)
    cache_v = cache_att_v.reshape(DEPTH * n_phys, PAGE_SIZE * ATT_HEADS, ATT_HEAD_DIM)
    mem_k_s = _interleaved_memory(cache_mem_k)
    mem_v_s = _interleaved_memory(cache_mem_v)
    mem_rows = mem_prompt.reshape(bsz * N_MEM, D_MODEL)

    xp = x_prompt.reshape(bsz * seq, D_MODEL)
    xs = x_sample.reshape(dbsz * dseq, D_MODEL)
    outs = {name: [] for name in ("p_pool", "p_conv", "s_pool", "s_conv")}
    cq_gain = rowv(cq_norm)
    kv_p = kv_s = mem_p = None

    for l in range(DEPTH):
        att_gains = jnp.stack([att_q_norm[l], att_k_norm[l]])

        mem_p = _norm_proj(mem_rows, lnmem, w_ckv_b, l, ck_norm[l][None], mem_segs, stacked=mem_p)
        mk, mv = (a.reshape(DEPTH * bsz, -1, LANES) for a in mem_p)
        xp, xs = _ffn(xp, xs, ln1, w1i, w1o, l)
        q, k, v, u, ag = _norm_proj(xp, lnm, w_in_b, l, att_gains, mix_segs, tables=tables_p, stacked=kv_p,
                                    tm=256)
        kv_p = (k, v)
        y_att = _moba_prompt(q, k, v, bsz, seq, l)
        p_pool = u.reshape(bsz, seq, POOL_WIDTH)[:, seq - POOL_BUF:]

        q_s, k_s, v_s, u_s, ag_s = _norm_proj(xs, lnm, w_in_b, l, att_gains, mix_segs, tables=tables_s, tm=tm_s,
                                              stacked=kv_s)
        kv_s = (k_s, v_s)
        pad3 = lambda a: a.reshape(dbsz, dseq, ATT_WIDTH)
        xp, h_tail, y_att_s = _mix_out_prompt_moba_sample(
            xp, u, ag, y_att, bsz, seq, mixer_w, w_out_b, pad3(q_s), pad3(k_s[l]), pad3(v_s[l]),
            cache_k, cache_v, page_table, l)
        outs["p_pool"].append(p_pool)
        outs["p_conv"].append(h_tail[:, CONV_HALO - CONV_BUF:])

        y_att = y_att_s.reshape(dbsz * dseq, ATT_WIDTH)
        tmaj = lambda a: jnp.swapaxes(a, 0, 1)
        u3 = u_s.reshape(dbsz, dseq, POOL_WIDTH)
        y_pool, y_conv, h_new = _poolconv_sample(
            tmaj(state_pool[l]), tmaj(u3), tmaj(state_conv[l]), tmaj(ag_s.reshape(dbsz, dseq, 2 * CONV_WIDTH)),
            past_len, mixer_w, l)
        bmaj = lambda a: jnp.swapaxes(a, 0, 1).reshape(dbsz * dseq, a.shape[-1])
        xs = _out_proj(xs, [bmaj(y_pool), bmaj(y_conv), y_att], w_out_b, l, out_rows, tm=tm_s)
        xp, xs = _cross(lnc, w_cq_b, cq_gain, w_co_b, l, xp.reshape(bsz, seq, D_MODEL), mk, mv, l * bsz, 512,
                        xs.reshape(dbsz, dseq, D_MODEL), mem_k_s, mem_v_s, l * dbsz)
        xp = xp.reshape(bsz * seq, D_MODEL)
        xs = xs.reshape(dbsz * dseq, D_MODEL)
        xp, xs = _ffn(xp, xs, ln2, w2i, w2o, l)
        outs["s_pool"].append(jnp.concatenate([state_pool[l][:, dseq:], u3], axis=1))
        outs["s_conv"].append(jnp.concatenate([state_conv[l][:, dseq:], jnp.swapaxes(h_new, 0, 1)], axis=1))

    st = lambda name: jnp.stack(outs[name])
    heads5 = lambda a, b, t: a.reshape(DEPTH, b, t, ATT_HEADS, ATT_HEAD_DIM)
    tiles = MEM_HEAD_DIM // LANES
    mem5 = lambda a: (a.reshape(DEPTH, bsz, N_MEM, tiles, MEM_HEADS, LANES).transpose(0, 1, 2, 4, 3, 5)
                      .reshape(DEPTH, bsz, N_MEM, MEM_HEADS, MEM_HEAD_DIM))
    return (xp.reshape(bsz, seq, D_MODEL), xs.reshape(dbsz, dseq, D_MODEL),
            heads5(kv_p[0], bsz, seq), heads5(kv_p[1], bsz, seq), mem5(mem_p[0]), mem5(mem_p[1]), st("p_pool"),
            st("p_conv"), heads5(kv_s[0], dbsz, dseq), heads5(kv_s[1], dbsz, dseq), st("s_pool"), st("s_conv"))
```

```python
import functools

import jax
import jax.numpy as jnp
import numpy as np
from jax import lax
from jax.experimental import pallas as pl
from jax.experimental.pallas import tpu as pltpu

D_MODEL = 1024
DEPTH = 2
PAGE_SIZE = 128

POOL_WIDTH = D_MODEL // 4
POOL_WINDOWS = (2, 4, 8, 16)
POOL_GROUP_WIDTH = POOL_WIDTH // len(POOL_WINDOWS)
POOL_BUF = max(POOL_WINDOWS) - 1

CONV_WIDTH = D_MODEL // 4
CONV_K = 31
CONV_BUF = CONV_K - 1

ATT_HEADS = 4
ATT_HEAD_DIM = D_MODEL // 8
ATT_WIDTH = ATT_HEADS * ATT_HEAD_DIM
ROT_DIM = ATT_HEAD_DIM // 4
ROPE_THETA = 500000.0
MOBA_BLOCK = 256
MOBA_TOPK = 3

N_MEM = 256
MEM_HEADS = 4
MEM_HEAD_DIM = D_MODEL // MEM_HEADS

D_FF = ((8 * D_MODEL // 3 + 127) // 128) * 128
EPS = 1e-6

LANES = 128
SUBLANES = 8
VMEM_LIMIT_BYTES = 56 * 1024 * 1024

F32 = jnp.float32
BF16 = jnp.bfloat16
NEG_INF = float("-inf")

_NT = (((1,), (1,)), ((), ()))


def _params(n_axes):
    return pltpu.CompilerParams(dimension_semantics=("arbitrary",) * n_axes,
                                vmem_limit_bytes=VMEM_LIMIT_BYTES)


def _rms(x, g):
    return x * lax.rsqrt(jnp.mean(x * x, axis=-1, keepdims=True) + EPS) * g


def _layer_spec(shape, l, n_grid):
    zeros = (0,) * len(shape)
    once = pl.Buffered(1)
    if n_grid == 1:
        return pl.BlockSpec((None,) + tuple(shape), lambda i: (l,) + zeros, pipeline_mode=once)
    if n_grid == 2:
        return pl.BlockSpec((None,) + tuple(shape), lambda i, j: (l,) + zeros, pipeline_mode=once)
    return pl.BlockSpec((None,) + tuple(shape), lambda i, j, k: (l,) + zeros, pipeline_mode=once)


FFN_CHUNK = 256


def _ffn_kernel(xa_ref, xb_ref, g_ref, wi_ref, wo_ref, oa_ref, ob_ref):
    i = pl.program_id(0)
    x = jnp.where(i == 0, xb_ref[...], xa_ref[...])
    xn = _rms(x, g_ref[...]).astype(BF16)
    acc = None
    for c in range(D_FF // FFN_CHUNK):
        lo = c * FFN_CHUNK
        gate = jnp.dot(xn, wi_ref[:, lo:lo + FFN_CHUNK], preferred_element_type=F32)
        up = jnp.dot(xn, wi_ref[:, D_FF + lo:D_FF + lo + FFN_CHUNK], preferred_element_type=F32)
        act = (gate * jax.nn.sigmoid(gate) * up).astype(BF16)
        part = jnp.dot(act, wo_ref[lo:lo + FFN_CHUNK, :], preferred_element_type=F32)
        acc = part if acc is None else acc + part
    oa_ref[...] = x + 0.5 * acc

    @pl.when(i == 0)
    def _():
        ob_ref[...] = oa_ref[...]


def _ffn(xa, xb, ln, w_in, w_out, l, tm=512):
    n_a = xa.shape[0] // tm
    assert xa.shape[0] == n_a * tm and xb.shape[0] == tm
    a_spec = pl.BlockSpec((tm, D_MODEL), lambda i: (jnp.maximum(i - 1, 0), 0))
    b_spec = pl.BlockSpec((tm, D_MODEL), lambda i: (0, 0))
    return pl.pallas_call(
        _ffn_kernel,
        grid=(n_a + 1,),
        in_specs=[
            a_spec, b_spec,
            _layer_spec((1, D_MODEL), l, 1),
            _layer_spec((D_MODEL, 2 * D_FF), l, 1),
            _layer_spec((D_FF, D_MODEL), l, 1),
        ],
        out_specs=[a_spec, b_spec],
        out_shape=[jax.ShapeDtypeStruct(xa.shape, F32), jax.ShapeDtypeStruct(xb.shape, F32)],
        compiler_params=_params(1),
        name="ffn",
    )(xa, xb, ln, w_in, w_out)


def _norm_proj_kernel(*refs, segs, rope):
    x_ref, g_ref, w_ref, hg_ref = refs[:4]
    if rope:
        cos_ref, sin_lo_ref, sin_hi_ref = refs[4:7]
    out_refs = refs[len(refs) - len(segs):]
    tm = x_ref.shape[0]
    xn = _rms(x_ref[...], g_ref[...]).astype(BF16)
    for (c0, width, head_dim, gain_row, use_rope, head_rows), o_ref in zip(segs, out_refs):
        z = jnp.dot(xn, w_ref[:, c0:c0 + width], preferred_element_type=F32)
        if head_dim is None:
            o_ref[...] = z
            continue
        n_heads = width // head_dim
        for h in range(n_heads):
            zh = z[:, h * head_dim:(h + 1) * head_dim]
            if gain_row is not None:
                zh = _rms(zh, hg_ref[gain_row:gain_row + 1, :])
            if use_rope:
                half = ROT_DIM // 2
                zh = (zh * cos_ref[...]
                      + pltpu.roll(zh, half, 1) * sin_hi_ref[...]
                      + pltpu.roll(zh, LANES - half, 1) * sin_lo_ref[...])
            if not head_rows:
                o_ref[:, h * head_dim:(h + 1) * head_dim] = zh
                continue
            tiles = head_dim // LANES
            for tile in range(tiles):
                rows = pl.ds(tile * n_heads + h, tm, stride=n_heads * tiles)
                piece = zh[:, tile * LANES:(tile + 1) * LANES]
                if len(o_ref.shape) == 2:
                    o_ref[rows, :] = piece
                else:
                    for d in range(o_ref.shape[0]):
                        o_ref[d, rows, :] = piece


def _norm_proj(x, ln, w, l, head_gains, segs, tables=None, tm=512, stacked=None):
    m = x.shape[0]
    n_cols = w.shape[-1]
    rope = tables is not None
    in_specs = [
        pl.BlockSpec((tm, D_MODEL), lambda i: (i, 0)),
        _layer_spec((1, D_MODEL), l, 1),
        _layer_spec((D_MODEL, n_cols), l, 1),
        pl.BlockSpec(head_gains.shape, lambda i: (0, 0)),
    ]
    args = [x, ln, w, head_gains]
    if rope:
        period = tables[0].shape[0] // tm
        for t in tables:
            in_specs.append(pl.BlockSpec((tm, LANES), lambda i: (i % period, 0)))
            args.append(t)
    out_specs, out_shape, aliases = [], [], {}
    for k, s in enumerate(segs):
        width, head_dim, head_rows = s[1], s[2], s[5]
        if not head_rows:
            out_specs.append(pl.BlockSpec((tm, width), lambda i: (i, 0)))
            out_shape.append(jax.ShapeDtypeStruct((m, width), F32))
            continue
        per_token = width // LANES
        out_shape.append(jax.ShapeDtypeStruct((DEPTH, m * per_token, LANES), F32))
        if stacked is None:
            out_specs.append(pl.BlockSpec((DEPTH, tm * per_token, LANES), lambda i: (0, i, 0)))
        else:
            out_specs.append(pl.BlockSpec((None, tm * per_token, LANES), lambda i: (l, i, 0)))
            aliases[len(args)] = k
            in_specs.append(pl.BlockSpec(memory_space=pl.ANY))
            args.append(stacked[len(aliases) - 1])
    return pl.pallas_call(
        functools.partial(_norm_proj_kernel, segs=tuple(segs), rope=rope),
        grid=(m // tm,),
        in_specs=in_specs,
        out_specs=out_specs,
        out_shape=out_shape,
        input_output_aliases=aliases,
        compiler_params=_params(1),
        name="norm_proj",
    )(*args)


def _rope_tables(positions):
    half = ROT_DIM // 2
    inv_freq = jnp.power(jnp.float32(ROPE_THETA), -jnp.arange(half, dtype=F32) / half)
    ang = positions.astype(F32)[:, None] * inv_freq[None, :]
    cos, sin = jnp.cos(ang), jnp.sin(ang)
    n = positions.shape[0]
    rest = ATT_HEAD_DIM - ROT_DIM
    cos_t = jnp.concatenate([cos, cos, jnp.ones((n, rest), F32)], axis=1)
    sin_lo = jnp.concatenate([-sin, jnp.zeros((n, half + rest), F32)], axis=1)
    sin_hi = jnp.concatenate([jnp.zeros((n, half), F32), sin, jnp.zeros((n, rest), F32)], axis=1)
    return cos_t, sin_lo, sin_hi


def _out_proj_kernel(*refs, n_terms):
    x_ref = refs[0]
    y_refs = refs[1:1 + n_terms]
    w_refs = refs[1 + n_terms:1 + 2 * n_terms]
    o_ref = refs[1 + 2 * n_terms]
    acc = x_ref[...]
    for y_ref, w_ref in zip(y_refs, w_refs):
        acc = acc + jnp.dot(y_ref[...].astype(BF16), w_ref[...], preferred_element_type=F32)
    o_ref[...] = acc


def _out_proj(x, ys, w, l, row_starts, tm=512):
    m = x.shape[0]
    in_specs = [pl.BlockSpec((tm, D_MODEL), lambda i: (i, 0))]
    for y in ys:
        in_specs.append(pl.BlockSpec((tm, y.shape[1]), lambda i: (i, 0)))
    for y, r0 in zip(ys, row_starts):
        width = y.shape[1]
        blk = r0 // width
        in_specs.append(pl.BlockSpec((None, width, D_MODEL), lambda i, blk=blk: (l, blk, 0)))
    return pl.pallas_call(
        functools.partial(_out_proj_kernel, n_terms=len(ys)),
        grid=(m // tm,),
        in_specs=in_specs,
        out_specs=pl.BlockSpec((tm, D_MODEL), lambda i: (i, 0)),
        out_shape=jax.ShapeDtypeStruct((m, D_MODEL), F32),
        compiler_params=_params(1),
        name="out_proj",
    )(x, *ys, *([w] * len(ys)))


def _select_blocks(gate, n_allowed):
    lane = lax.broadcasted_iota(jnp.int32, gate.shape, 1)
    allowed = lane < n_allowed
    g = jnp.where(allowed, gate, NEG_INF)
    rank = jnp.zeros(gate.shape, F32)
    for r in range(1, SUBLANES):
        lower = pltpu.roll(g, r, 1)
        higher = pltpu.roll(g, LANES - r, 1)
        rank = rank + jnp.where(lower >= g, 1.0, 0.0) + jnp.where(higher > g, 1.0, 0.0)
    return jnp.where(allowed, jnp.where(rank < MOBA_TOPK, 1.0, 0.0), 0.0)


def _dot_nt_split(a, b):
    a_hi = a.astype(BF16)
    a_lo = (a - a_hi.astype(F32)).astype(BF16)
    b_hi = b.astype(BF16)
    b_lo = (b - b_hi.astype(F32)).astype(BF16)
    m = a.shape[0]
    by_hi = lax.dot_general(jnp.concatenate([a_hi, a_lo], axis=0), b_hi, _NT, preferred_element_type=F32)
    return by_hi[0:m] + by_hi[m:2 * m] + lax.dot_general(a_hi, b_lo, _NT, preferred_element_type=F32)


def _block_expander(n_keys):
    blk = np.arange(n_keys) // MOBA_BLOCK
    return jnp.asarray((np.arange(LANES)[:, None] == blk[None, :]).astype(np.float32), dtype=BF16)


MASK_BIAS = -1e30


def _select_blocks_t(gate_t, n_allowed):
    blk = lax.broadcasted_iota(jnp.int32, gate_t.shape, 0)
    allowed = blk < n_allowed
    g = jnp.where(allowed, gate_t, NEG_INF)
    rank = jnp.zeros(gate_t.shape, F32)
    for m in range(SUBLANES):
        gm = g[m:m + 1, :]
        tie = jnp.where(gm == g, jnp.where(blk > m, 1.0, 0.0), 0.0)
        rank = rank + jnp.where(gm > g, 1.0, tie)
    return jnp.where(allowed, jnp.where(rank < MOBA_TOPK, 1.0, 0.0), 0.0)


def _moba_prompt_kernel(q_ref, k_ref, v_ref, kbias_ref, o_ref, ka_ref, vb_ref, km_ref, *, n_blocks, per_step):
    qi = pl.program_id(1)
    blk = MOBA_BLOCK
    dh = ATT_HEAD_DIM
    heads = range(ATT_HEADS)

    @pl.when(qi == 0)
    def _():
        km_ref[...] = jnp.zeros(km_ref.shape, F32)
        for h in heads:
            vb_ref[:, h * dh:(h + 1) * dh] = v_ref[pl.ds(h, n_blocks * blk, stride=ATT_HEADS), :].astype(BF16)
            ka_ref[h, :, dh:2 * dh] = kbias_ref[h]
            for n in range(n_blocks):
                kb = k_ref[pl.ds(n * blk * ATT_HEADS + h, blk, stride=ATT_HEADS), :]
                ka_ref[h, n * blk:(n + 1) * blk, 0:dh] = kb.astype(BF16)
                r = h * SUBLANES + n
                km_ref[r:r + 1, h * dh:(h + 1) * dh] = jnp.mean(kb, axis=0, keepdims=True)

    row = lax.broadcasted_iota(jnp.int32, (blk, blk), 0)
    col = lax.broadcasted_iota(jnp.int32, (blk, blk), 1)

    def step(c, r0):
        n_keys = (c + 1) * blk
        gated = c > MOBA_TOPK
        if gated:
            gate_all = _dot_nt_split(km_ref[...], q_ref[r0:r0 + blk, :])
            n_rows = ATT_HEADS * SUBLANES
            flags = []
            for h in heads:
                gate_t = gate_all[h * SUBLANES:(h + 1) * SUBLANES]
                blk_id = lax.broadcasted_iota(jnp.int32, gate_t.shape, 0)
                flags.append(jnp.where(blk_id == c, 0.0, 1.0 - _select_blocks_t(gate_t, c)))
            flags.append(jnp.zeros((LANES - n_rows, blk), F32))
            unchosen = jnp.concatenate(flags, axis=0).T.astype(BF16)
        for h in heads:
            q = q_ref[r0:r0 + blk, h * dh:(h + 1) * dh]
            qs = (q * (dh ** -0.5)).astype(BF16)
            if gated:
                qa = jnp.concatenate([qs, unchosen], axis=1)
                s = lax.dot_general(qa, ka_ref[h, 0:n_keys, :], _NT, preferred_element_type=F32)
            else:
                s = lax.dot_general(qs, ka_ref[h, 0:n_keys, 0:dh], _NT, preferred_element_type=F32)
            parts = [s[:, 0:c * blk]] if c else []
            parts.append(jnp.where(row >= col, s[:, c * blk:n_keys], NEG_INF))
            s = jnp.concatenate(parts, axis=1)
            p = jnp.exp(s - jnp.max(s, axis=-1, keepdims=True))
            inv = 1.0 / jnp.sum(p, axis=-1, keepdims=True)
            o = jnp.dot(p.astype(BF16), vb_ref[0:n_keys, h * dh:(h + 1) * dh], preferred_element_type=F32)
            o_ref[r0:r0 + blk, h * dh:(h + 1) * dh] = o * inv

    def tile(j):
        for r in range(per_step):
            step(j * per_step + r, r * blk)

    for j in range(n_blocks // per_step):
        pl.when(qi == j)(functools.partial(tile, j))


def _moba_prompt(q, k, v, bsz, t, l):
    assert t % MOBA_BLOCK == 0 and t // MOBA_BLOCK <= SUBLANES
    n_blocks = t // MOBA_BLOCK
    dh = ATT_HEAD_DIM
    key_blk = np.arange(t) // MOBA_BLOCK
    lane_of = np.arange(ATT_HEADS)[:, None, None] * SUBLANES + key_blk[None, :, None]
    kbias = jnp.asarray(np.where(lane_of == np.arange(LANES)[None, None, :], MASK_BIAS, 0.0), dtype=BF16)
    per_step = 2 if n_blocks % 2 == 0 else 1
    nt = n_blocks // per_step
    tile_spec = pl.BlockSpec((per_step * MOBA_BLOCK, ATT_WIDTH), lambda b, i: (b * nt + i, 0))
    return pl.pallas_call(
        functools.partial(_moba_prompt_kernel, n_blocks=n_blocks, per_step=per_step),
        grid=(bsz, nt),
        in_specs=[
            tile_spec,
            pl.BlockSpec((None, t * ATT_HEADS, dh), lambda b, i: (l, b, 0)),
            pl.BlockSpec((None, t * ATT_HEADS, dh), lambda b, i: (l, b, 0)),
            pl.BlockSpec((ATT_HEADS, t, LANES), lambda b, i: (0, 0, 0)),
        ],
        out_specs=tile_spec,
        out_shape=jax.ShapeDtypeStruct((bsz * t, ATT_WIDTH), F32),
        scratch_shapes=[
            pltpu.VMEM((ATT_HEADS, t, 2 * dh), BF16),
            pltpu.VMEM((t, ATT_WIDTH), BF16),
            pltpu.VMEM((ATT_HEADS * SUBLANES, ATT_WIDTH), F32),
        ],
        compiler_params=_params(2),
        name="moba_prompt",
    )(q, k, v, kbias)


def _moba_sample_kernel(pt_ref, q_ref, kn_ref, vn_ref, e_ref, *refs, n_pages, n_seq):
    del pt_ref
    k_refs = refs[:n_seq * n_pages]
    v_refs = refs[n_seq * n_pages:2 * n_seq * n_pages]
    o_ref, tok_ref = refs[2 * n_seq * n_pages:]
    n_new = q_ref.shape[1]
    rows = tok_ref.shape[2]
    tok_ref[...] = jnp.zeros(tok_ref.shape, F32)
    tok_ref[0, :, 0:n_new, :] = q_ref[...]
    tok_ref[1, :, 0:n_new, :] = kn_ref[...]
    tok_ref[2, :, 0:n_new, :] = vn_ref[...]
    q_ref, kn_ref, vn_ref = tok_ref.at[0], tok_ref.at[1], tok_ref.at[2]
    pages_per_block = MOBA_BLOCK // PAGE_SIZE
    n_blocks = n_pages // pages_per_block
    dh = ATT_HEAD_DIM
    pairs = [(g, h) for g in range(n_seq) for h in range(ATT_HEADS)]

    def block_rows(page_refs, g, n, h):
        first = g * n_pages + n * pages_per_block
        return jnp.concatenate(
            [page_refs[j][0, pl.ds(h, PAGE_SIZE, stride=ATT_HEADS), :] for j in range(first, first + pages_per_block)],
            axis=0)

    qs, gate_rows, s_rows = [], [], []
    for g, h in pairs:
        q = q_ref[g, :, h * dh:(h + 1) * dh]
        qs.append((q * (dh ** -0.5)).astype(BF16))
        mean_rows, s_parts = [], []
        for n in range(n_blocks):
            kb = block_rows(k_refs, g, n, h)
            mean_rows.append(jnp.sum(kb, axis=0, keepdims=True) / MOBA_BLOCK)
            s_parts.append(lax.dot_general(qs[-1], kb.astype(BF16), _NT, preferred_element_type=F32))
        mean_rows.append(jnp.zeros((LANES - n_blocks, dh), F32))
        gate_rows.append(_dot_nt_split(q, jnp.concatenate(mean_rows, axis=0)))
        s_rows.append(jnp.concatenate(s_parts, axis=1))
    sel = _select_blocks(jnp.concatenate(gate_rows, axis=0), n_blocks)
    sel_keys = jnp.dot(sel.astype(BF16), e_ref[...], preferred_element_type=F32)
    s = jnp.where(sel_keys > 0.5, jnp.concatenate(s_rows, axis=0), NEG_INF)

    pad = jnp.zeros((LANES - rows, dh), F32)
    padded = lambda ref, g, h: jnp.concatenate([ref[g, :, h * dh:(h + 1) * dh], pad], axis=0).astype(BF16)
    s_own = jnp.concatenate(
        [lax.dot_general(qs[i], padded(kn_ref, g, h), _NT, preferred_element_type=F32)
         for i, (g, h) in enumerate(pairs)], axis=0)
    t_row = lax.broadcasted_iota(jnp.int32, s_own.shape, 0) % rows
    key = lax.broadcasted_iota(jnp.int32, s_own.shape, 1)
    s_own = jnp.where(key <= t_row, s_own, NEG_INF)

    m = jnp.maximum(jnp.max(s, axis=-1, keepdims=True), jnp.max(s_own, axis=-1, keepdims=True))
    p = jnp.exp(s - m)
    p_own = jnp.exp(s_own - m)
    inv = 1.0 / (jnp.sum(p, axis=-1, keepdims=True) + jnp.sum(p_own, axis=-1, keepdims=True))
    pb = p.astype(BF16)
    pb_own = p_own.astype(BF16)
    outs = []
    for i, (g, h) in enumerate(pairs):
        r0 = i * rows
        o = jnp.dot(pb_own[r0:r0 + rows], padded(vn_ref, g, h), preferred_element_type=F32)
        for n in range(n_blocks):
            o = o + jnp.dot(pb[r0:r0 + rows, n * MOBA_BLOCK:(n + 1) * MOBA_BLOCK],
                            block_rows(v_refs, g, n, h).astype(BF16), preferred_element_type=F32)
        outs.append(o * inv[r0:r0 + rows])
    for g in range(n_seq):
        o_ref[g] = jnp.concatenate(outs[g * ATT_HEADS:(g + 1) * ATT_HEADS], axis=1)[0:n_new]


def _pool_windows_to_lanes(sums, counts, cur):
    lane = lax.broadcasted_iota(jnp.int32, cur.shape, 1)
    d = sums[-1] / counts[-1]
    for g in range(len(POOL_WINDOWS) - 2, -1, -1):
        d = jnp.where(lane < (g + 1) * POOL_GROUP_WIDTH, sums[g] / counts[g], d)
    return d - cur


def _conv_tail(y, ln_g, ln_b, pw_w, pw_b):
    yc = y - jnp.mean(y, axis=-1, keepdims=True)
    yn = yc * lax.rsqrt(jnp.mean(yc * yc, axis=-1, keepdims=True) + EPS) * ln_g + ln_b
    act = yn * jax.nn.sigmoid(yn)
    return jnp.dot(act.astype(BF16), pw_w, preferred_element_type=F32) + pw_b


POOL_HALO = 16
CONV_HALO = 32


def _poolconv_prompt_kernel(x_ref, u_ref, ag_ref, ya_ref, pw_ref, ps_ref, dw_ref, db_ref, lg_ref, lb_ref,
                            cw_ref, cb_ref, wp_ref, wc_ref, wa_ref, o_ref, ht_ref, e_ref, h_ref, hs_ref, *, tt,
                            side_work=None):
    ti = pl.program_id(1)

    @pl.when(ti == 0)
    def _():
        e_ref[0:POOL_HALO, :] = jnp.zeros((POOL_HALO, POOL_WIDTH), F32)
        h_ref[0:CONV_HALO, :] = jnp.zeros((CONV_HALO, CONV_WIDTH), F32)

    if side_work is not None:
        side_work()

    u = u_ref[...]
    e_ref[POOL_HALO:POOL_HALO + tt, :] = u
    a = ag_ref[:, 0:CONV_WIDTH]
    gate = ag_ref[:, CONV_WIDTH:2 * CONV_WIDTH]
    h_ref[CONV_HALO:CONV_HALO + tt, :] = a * jax.nn.sigmoid(gate)

    pos = ti * tt + lax.broadcasted_iota(jnp.int32, (tt, 1), 0)
    sums, counts = [], []
    run = u
    taken = 1
    for win in POOL_WINDOWS:
        while taken < win:
            run = run + e_ref[POOL_HALO - taken:POOL_HALO - taken + tt, :]
            taken += 1
        sums.append(run)
        counts.append(jnp.minimum(pos + 1, win).astype(F32))
    d = _pool_windows_to_lanes(sums, counts, u)
    y_pool = jnp.dot(d.astype(BF16), pw_ref[...], preferred_element_type=F32) * ps_ref[...]

    span = tt + CONV_HALO - SUBLANES
    for s in range(1, SUBLANES):
        hs_ref[s - 1, 0:span, :] = h_ref[s:s + span, :]
    y = jnp.zeros((tt, CONV_WIDTH), F32) + db_ref[...]
    for j in range(CONV_K):
        r0 = CONV_HALO - CONV_BUF + j
        s = r0 % SUBLANES
        rows = h_ref[r0:r0 + tt, :] if s == 0 else hs_ref[s - 1, r0 - s:r0 - s + tt, :]
        y = y + rows * dw_ref[j:j + 1, :]
    y_conv = _conv_tail(y, lg_ref[...], lb_ref[...], cw_ref[...], cb_ref[...])

    out = x_ref[...] + jnp.dot(y_pool.astype(BF16), wp_ref[...], preferred_element_type=F32)
    out = out + jnp.dot(y_conv.astype(BF16), wc_ref[...], preferred_element_type=F32)
    o_ref[...] = out + jnp.dot(ya_ref[...].astype(BF16), wa_ref[...], preferred_element_type=F32)

    ht_ref[0] = h_ref[tt:tt + CONV_HALO, :]
    e_ref[0:POOL_HALO, :] = e_ref[tt:tt + POOL_HALO, :]
    h_ref[0:CONV_HALO, :] = h_ref[tt:tt + CONV_HALO, :]


N_MIX_IN = 15


def _mix_out_moba_kernel(pt_ref, *refs, tt, n_pages, n_seq):
    n_moba_in = 4 + 2 * n_seq * n_pages
    mix_in = refs[:N_MIX_IN]
    moba_in = refs[N_MIX_IN:N_MIX_IN + n_moba_in]
    o_ref, ht_ref, att_ref = refs[N_MIX_IN + n_moba_in:N_MIX_IN + n_moba_in + 3]
    *mix_scratch, tok_ref = refs[N_MIX_IN + n_moba_in + 3:]
    side = functools.partial(_moba_sample_kernel, pt_ref, *moba_in, att_ref, tok_ref, n_pages=n_pages,
                             n_seq=n_seq)
    _poolconv_prompt_kernel(*mix_in, o_ref, ht_ref, *mix_scratch, tt=tt, side_work=side)


def _mix_out_prompt_moba_sample(x, u, ag, y_att, bsz, t, weights, w_out, q_s, k_new, v_new, cache_k, cache_v,
                                page_table, l, tt=256):
    pool_w, pool_s, dw_w, dw_b, ln_g, ln_b, pw_w, pw_b = weights
    nt = t // tt
    dbsz, rows, _ = q_s.shape
    n_seq = dbsz // (bsz * nt)
    assert n_seq * bsz * nt == dbsz
    n_pages = page_table.shape[1]
    n_phys = cache_k.shape[0] // DEPTH
    past_len = n_pages * PAGE_SIZE
    assert past_len % MOBA_BLOCK == 0 and past_len // MOBA_BLOCK <= SUBLANES
    base = l * n_phys
    row = lambda b, i, pt: (b * nt + i, 0)
    w_rows = lambda width, blk: pl.BlockSpec((None, width, D_MODEL), lambda b, i, pt: (l, blk, 0))
    tok_spec = pl.BlockSpec((n_seq, rows, ATT_WIDTH), lambda b, i, pt: (b * nt + i, 0, 0))
    page_specs = [
        pl.BlockSpec((1, PAGE_SIZE * ATT_HEADS, ATT_HEAD_DIM),
                     lambda b, i, pt, g=g, j=j: (base + pt[((b * nt + i) * n_seq + g) * n_pages + j], 0, 0))
        for g in range(n_seq) for j in range(n_pages)
    ]
    grid_spec = pltpu.PrefetchScalarGridSpec(
        num_scalar_prefetch=1,
        grid=(bsz, nt),
        in_specs=[
            pl.BlockSpec((tt, D_MODEL), row),
            pl.BlockSpec((tt, POOL_WIDTH), row),
            pl.BlockSpec((tt, 2 * CONV_WIDTH), row),
            pl.BlockSpec((tt, ATT_WIDTH), row),
            _layer_spec((POOL_WIDTH, POOL_WIDTH), l, 3),
            _layer_spec((1, POOL_WIDTH), l, 3),
            _layer_spec((CONV_HALO, CONV_WIDTH), l, 3),
            _layer_spec((1, CONV_WIDTH), l, 3),
            _layer_spec((1, CONV_WIDTH), l, 3),
            _layer_spec((1, CONV_WIDTH), l, 3),
            _layer_spec((CONV_WIDTH, CONV_WIDTH), l, 3),
            _layer_spec((1, CONV_WIDTH), l, 3),
            w_rows(POOL_WIDTH, 0),
            w_rows(CONV_WIDTH, POOL_WIDTH // CONV_WIDTH),
            w_rows(ATT_WIDTH, (POOL_WIDTH + CONV_WIDTH) // ATT_WIDTH),
            tok_spec, tok_spec, tok_spec,
            pl.BlockSpec((LANES, past_len), lambda b, i, pt: (0, 0)),
        ] + page_specs + page_specs,
        out_specs=[
            pl.BlockSpec((tt, D_MODEL), row),
            pl.BlockSpec((1, CONV_HALO, CONV_WIDTH), lambda b, i, pt: (b, 0, 0)),
            tok_spec,
        ],
        scratch_shapes=[
            pltpu.VMEM((POOL_HALO + tt, POOL_WIDTH), F32),
            pltpu.VMEM((CONV_HALO + tt, CONV_WIDTH), F32),
            pltpu.VMEM((SUBLANES - 1, CONV_HALO + tt, CONV_WIDTH), F32),
            pltpu.VMEM((3, n_seq, 2 * SUBLANES, ATT_WIDTH), F32),
        ],
    )
    return pl.pallas_call(
        functools.partial(_mix_out_moba_kernel, tt=tt, n_pages=n_pages, n_seq=n_seq),
        grid_spec=grid_spec,
        out_shape=[
            jax.ShapeDtypeStruct((bsz * t, D_MODEL), F32),
            jax.ShapeDtypeStruct((bsz, CONV_HALO, CONV_WIDTH), F32),
            jax.ShapeDtypeStruct((dbsz, rows, ATT_WIDTH), F32),
        ],
        compiler_params=_params(2),
        name="mix_out_moba",
    )(page_table.reshape(-1), x, u, ag, y_att, pool_w, pool_s, dw_w, dw_b, ln_g, ln_b, pw_w, pw_b,
      w_out, w_out, w_out, q_s, k_new, v_new, _block_expander(past_len),
      *([cache_k] * (n_seq * n_pages)), *([cache_v] * (n_seq * n_pages)))


def _poolconv_sample_kernel(sp_ref, u_ref, sc_ref, ag_ref, pw_ref, ps_ref, dw_ref, db_ref, lg_ref, lb_ref,
                            cw_ref, cb_ref, yp_ref, yc_ref, h_ref, *, n_new, pos0):
    def pool_row(r):
        return sp_ref[:, r, :] if r < POOL_BUF else u_ref[:, r - POOL_BUF, :]

    for t in range(n_new):
        h_ref[:, t, :] = ag_ref[:, t, 0:CONV_WIDTH] * jax.nn.sigmoid(ag_ref[:, t, CONV_WIDTH:2 * CONV_WIDTH])

    def conv_row(r):
        return sc_ref[:, r, :] if r < CONV_BUF else h_ref[:, r - CONV_BUF, :]

    for t in range(n_new):
        cur = u_ref[:, t, :]
        sums, counts = [], []
        run = cur
        taken = 1
        for win in POOL_WINDOWS:
            while taken < win:
                run = run + pool_row(POOL_BUF + t - taken)
                taken += 1
            sums.append(run)
            counts.append(float(min(pos0 + t + 1, win)))
        d = _pool_windows_to_lanes(sums, counts, cur)
        yp_ref[:, t, :] = jnp.dot(d.astype(BF16), pw_ref[...], preferred_element_type=F32) * ps_ref[...]

        y = jnp.zeros(cur.shape, F32) + db_ref[...]
        for j in range(CONV_K):
            y = y + conv_row(t + j) * dw_ref[j:j + 1, :]
        yc_ref[:, t, :] = _conv_tail(y, lg_ref[...], lb_ref[...], cw_ref[...], cb_ref[...])


def _poolconv_sample(state_pool, u, state_conv, ag, pos0, weights, l):
    pool_w, pool_s, dw_w, dw_b, ln_g, ln_b, pw_w, pw_b = weights
    bsz, n_new, _ = u.shape
    full = lambda a: pl.BlockSpec(a.shape, lambda i: (0,) * a.ndim)
    out = jax.ShapeDtypeStruct((bsz, n_new, CONV_WIDTH), F32)
    return pl.pallas_call(
        functools.partial(_poolconv_sample_kernel, n_new=n_new, pos0=pos0),
        grid=(1,),
        in_specs=[
            _layer_spec(state_pool.shape[1:], l, 1), full(u), _layer_spec(state_conv.shape[1:], l, 1), full(ag),
            _layer_spec((POOL_WIDTH, POOL_WIDTH), l, 1),
            _layer_spec((1, POOL_WIDTH), l, 1),
            _layer_spec((CONV_HALO, CONV_WIDTH), l, 1),
            _layer_spec((1, CONV_WIDTH), l, 1),
            _layer_spec((1, CONV_WIDTH), l, 1),
            _layer_spec((1, CONV_WIDTH), l, 1),
            _layer_spec((CONV_WIDTH, CONV_WIDTH), l, 1),
            _layer_spec((1, CONV_WIDTH), l, 1),
        ],
        out_specs=[pl.BlockSpec(out.shape, lambda i: (0, 0, 0))] * 3,
        out_shape=[out, out, out],
        compiler_params=_params(1),
        name="poolconv_sample",
    )(state_pool, u, state_conv, ag, pool_w, pool_s, dw_w, dw_b, ln_g, ln_b, pw_w, pw_b)


def _cross_kernel(g_ref, wq_ref, hg_ref, wo_ref, xa_ref, ka_ref, va_ref, xb_ref, kb_ref, vb_ref, oa_ref, ob_ref,
                  xb_pad_ref):
    weights = (g_ref, wq_ref, hg_ref, wo_ref)
    n_new = xb_ref.shape[1]
    xb_pad_ref[...] = jnp.zeros(xb_pad_ref.shape, F32)
    xb_pad_ref[:, 0:n_new, :] = xb_ref[...]
    ob_ref[...] = _cross_rows(xb_pad_ref, kb_ref, vb_ref, *weights)[:, 0:n_new, :]
    oa_ref[...] = _cross_rows(xa_ref, ka_ref, va_ref, *weights)


def _cross_rows(x_ref, k_ref, v_ref, g_ref, wq_ref, hg_ref, wo_ref):
    hd = MEM_HEAD_DIM
    n_seq, rows, _ = x_ref.shape
    x = x_ref[...].reshape(n_seq * rows, D_MODEL)
    xn = _rms(x, g_ref[...]).astype(BF16)
    q_all = jnp.dot(xn, wq_ref[...], preferred_element_type=F32)
    halves = range(hd // LANES)

    def head_slab(ref, g, h):
        return jnp.concatenate(
            [ref[g, pl.ds(half * MEM_HEADS + h, N_MEM, stride=len(halves) * MEM_HEADS), :] for half in halves],
            axis=1)

    pairs = [(g, h) for g in range(n_seq) for h in range(MEM_HEADS)]
    scores = []
    for g, h in pairs:
        q = _rms(q_all[g * rows:(g + 1) * rows, h * hd:(h + 1) * hd], hg_ref[...])
        q = (q * (hd ** -0.5)).astype(BF16)
        scores.append(lax.dot_general(q, head_slab(k_ref, g, h).astype(BF16), _NT, preferred_element_type=F32))
    s = jnp.concatenate(scores, axis=0)
    p = jnp.exp(s - jnp.max(s, axis=-1, keepdims=True))
    inv = 1.0 / jnp.sum(p, axis=-1, keepdims=True)
    o_seqs = []
    for g in range(n_seq):
        o_heads = []
        for h in range(MEM_HEADS):
            r0 = (g * MEM_HEADS + h) * rows
            o = jnp.dot(p[r0:r0 + rows].astype(BF16), head_slab(v_ref, g, h).astype(BF16),
                        preferred_element_type=F32)
            o_heads.append((o * inv[r0:r0 + rows]).astype(BF16))
        o_seqs.append(jnp.concatenate(o_heads, axis=1))
    o_all = o_seqs[0] if n_seq == 1 else jnp.concatenate(o_seqs, axis=0)
    y = x + jnp.dot(o_all, wo_ref[...], preferred_element_type=F32)
    return y.reshape(n_seq, rows, D_MODEL)


def _interleaved_memory(cache):
    depth, s, n_mem, heads, hd = cache.shape
    halves = hd // LANES
    c = cache.reshape(depth, s, n_mem, heads, halves, LANES).transpose(0, 1, 2, 4, 3, 5)
    return c.reshape(depth * s, n_mem * halves * heads, LANES)


def _cross(ln, w_q, head_gain, w_o, l, xa, mem_ka, mem_va, seq0_a, tq, xb, mem_kb, mem_vb, seq0_b):
    sa, ra, _ = xa.shape
    sb, rb, _ = xb.shape
    nt = ra // tq
    n_b = sb // (sa * nt)
    assert n_b * sa * nt == sb and seq0_b % n_b == 0
    xa_spec = pl.BlockSpec((1, tq, D_MODEL), lambda s, i: (s, i, 0))
    kva_spec = pl.BlockSpec((1,) + mem_ka.shape[1:], lambda s, i: (seq0_a + s, 0, 0))
    xb_spec = pl.BlockSpec((n_b, rb, D_MODEL), lambda s, i: (s * nt + i, 0, 0))
    kvb_spec = pl.BlockSpec((n_b,) + mem_kb.shape[1:], lambda s, i: (seq0_b // n_b + s * nt + i, 0, 0))
    return pl.pallas_call(
        _cross_kernel,
        grid=(sa, nt),
        in_specs=[_layer_spec((1, D_MODEL), l, 2), _layer_spec((D_MODEL, D_MODEL), l, 2),
                  _layer_spec((1, MEM_HEAD_DIM), l, 2), _layer_spec((D_MODEL, D_MODEL), l, 2),
                  xa_spec, kva_spec, kva_spec, xb_spec, kvb_spec, kvb_spec],
        out_specs=[xa_spec, xb_spec],
        out_shape=[jax.ShapeDtypeStruct(xa.shape, F32), jax.ShapeDtypeStruct(xb.shape, F32)],
        scratch_shapes=[pltpu.VMEM((n_b, -(-rb // SUBLANES) * SUBLANES, D_MODEL), F32)],
        compiler_params=_params(2),
        name="cross",
    )(ln, w_q, head_gain, w_o, xa, mem_ka, mem_va, xb, mem_kb, mem_vb)


def kernel(x_prompt, x_sample, cache_att_k, cache_att_v, cache_mem_k, cache_mem_v, state_pool, state_conv, page_table, mem_prompt, ln_ffn1, w_ffn1_in, w_ffn1_out, ln_mix, w_in, pool_w, pool_scale, conv_dw_w, conv_dw_b, conv_ln_g, conv_ln_b, conv_pw_w, conv_pw_b, att_q_norm, att_k_norm, w_out, ln_cross, ln_mem, w_cq, w_ckv, cq_norm, ck_norm, w_co, ln_ffn2, w_ffn2_in, w_ffn2_out):
    bsz, seq, _ = x_prompt.shape
    dbsz, dseq, _ = x_sample.shape
    n_pages = page_table.shape[1]
    past_len = n_pages * PAGE_SIZE
    n_phys = cache_att_k.shape[1]

    bf = lambda w: w.astype(BF16)
    rowv = lambda v: v[:, None, :]
    w1i, w1o, w2i, w2o = bf(w_ffn1_in), bf(w_ffn1_out), bf(w_ffn2_in), bf(w_ffn2_out)
    w_in_b, w_out_b, w_cq_b, w_ckv_b, w_co_b = bf(w_in), bf(w_out), bf(w_cq), bf(w_ckv), bf(w_co)
    groups = len(POOL_WINDOWS)
    eye = jnp.eye(groups, dtype=F32)
    pool_bd = bf((pool_w[:, :, :, None, :] * eye[None, :, None, :, None])
                 .reshape(DEPTH, POOL_WIDTH, POOL_WIDTH))
    dw_pad = jnp.pad(conv_dw_w, ((0, 0), (0, CONV_HALO - CONV_K), (0, 0)))
    mixer_w = (pool_bd, rowv(pool_scale), dw_pad, rowv(conv_dw_b), rowv(conv_ln_g), rowv(conv_ln_b),
               bf(conv_pw_w), rowv(conv_pw_b))
    ln1, lnm, lnc, lnmem, ln2 = rowv(ln_ffn1), rowv(ln_mix), rowv(ln_cross), rowv(ln_mem), rowv(ln_ffn2)

    tables_p = _rope_tables(jnp.arange(seq, dtype=jnp.int32))
    tm_s = dbsz * dseq
    tables_s = _rope_tables(past_len + jnp.arange(tm_s, dtype=jnp.int32) % dseq)

    c1 = POOL_WIDTH
    c3 = c1 + 2 * CONV_WIDTH
    c4 = c3 + ATT_WIDTH
    c5 = c4 + ATT_WIDTH
    mix_segs = [(c3, ATT_WIDTH, ATT_HEAD_DIM, 0, True, False), (c4, ATT_WIDTH, ATT_HEAD_DIM, 1, True, True),
                (c5, ATT_WIDTH, ATT_HEAD_DIM, None, False, True),
                (0, c1, None, None, False, False), (c1, 2 * CONV_WIDTH, None, None, False, False)]
    mem_segs = [(0, D_MODEL, MEM_HEAD_DIM, 0, False, True), (D_MODEL, D_MODEL, MEM_HEAD_DIM, None, False, True)]
    out_rows = (0, c1, c1 + CONV_WIDTH)

    cache_k = cache_att_k.reshape(DEPTH * n_phys, PAGE_SIZE * ATT_HEADS, ATT_HEAD_DIM)
    cache_v = cache_att_v.reshape(DEPTH * n_phys, PAGE_SIZE * ATT_HEADS, ATT_HEAD_DIM)
    mem_k_s = _interleaved_memory(cache_mem_k)
    mem_v_s = _interleaved_memory(cache_mem_v)
    mem_rows = mem_prompt.reshape(bsz * N_MEM, D_MODEL)

    xp = x_prompt.reshape(bsz * seq, D_MODEL)
    xs = x_sample.reshape(dbsz * dseq, D_MODEL)
    outs = {name: [] for name in ("p_pool", "p_conv", "s_pool", "s_conv")}
    cq_gain = rowv(cq_norm)
    kv_p = kv_s = mem_p = None

    for l in range(DEPTH):
        att_gains = jnp.stack([att_q_norm[l], att_k_norm[l]])

        mem_p = _norm_proj(mem_rows, lnmem, w_ckv_b, l, ck_norm[l][None], mem_segs, stacked=mem_p)
        mk, mv = (a.reshape(DEPTH * bsz, -1, LANES) for a in mem_p)
        xp, xs = _ffn(xp, xs, ln1, w1i, w1o, l)
        q, k, v, u, ag = _norm_proj(xp, lnm, w_in_b, l, att_gains, mix_segs, tables=tables_p, stacked=kv_p,
                                    tm=256)
        kv_p = (k, v)
        y_att = _moba_prompt(q, k, v, bsz, seq, l)
        p_pool = u.reshape(bsz, seq, POOL_WIDTH)[:, seq - POOL_BUF:]

        q_s, k_s, v_s, u_s, ag_s = _norm_proj(xs, lnm, w_in_b, l, att_gains, mix_segs, tables=tables_s, tm=tm_s,
                                              stacked=kv_s)
        kv_s = (k_s, v_s)
        pad3 = lambda a: a.reshape(dbsz, dseq, ATT_WIDTH)
        xp, h_tail, y_att_s = _mix_out_prompt_moba_sample(
            xp, u, ag, y_att, bsz, seq, mixer_w, w_out_b, pad3(q_s), pad3(k_s[l]), pad3(v_s[l]),
            cache_k, cache_v, page_table, l)
        outs["p_pool"].append(p_pool)
        outs["p_conv"].append(h_tail[:, CONV_HALO - CONV_BUF:])

        y_att = y_att_s.reshape(dbsz * dseq, ATT_WIDTH)
        u3 = u_s.reshape(dbsz, dseq, POOL_WIDTH)
        y_pool, y_conv, h_new = _poolconv_sample(
            state_pool, u3, state_conv, ag_s.reshape(dbsz, dseq, 2 * CONV_WIDTH), past_len, mixer_w, l)
        rows2 = lambda a: a.reshape(dbsz * dseq, a.shape[-1])
        xs = _out_proj(xs, [rows2(y_pool), rows2(y_conv), y_att], w_out_b, l, out_rows, tm=tm_s)
        xp, xs = _cross(lnc, w_cq_b, cq_gain, w_co_b, l, xp.reshape(bsz, seq, D_MODEL), mk, mv, l * bsz, 512,
                        xs.reshape(dbsz, dseq, D_MODEL), mem_k_s, mem_v_s, l * dbsz)
        xp = xp.reshape(bsz * seq, D_MODEL)
        xs = xs.reshape(dbsz * dseq, D_MODEL)
        xp, xs = _ffn(xp, xs, ln2, w2i, w2o, l)
        outs["s_pool"].append(jnp.concatenate([state_pool[l][:, dseq:], u3], axis=1))
        outs["s_conv"].append(jnp.concatenate([state_conv[l][:, dseq:], h_new], axis=1))

    st = lambda name: jnp.stack(outs[name])
    heads5 = lambda a, b, t: a.reshape(DEPTH, b, t, ATT_HEADS, ATT_HEAD_DIM)
    tiles = MEM_HEAD_DIM // LANES
    mem5 = lambda a: (a.reshape(DEPTH, bsz, N_MEM, tiles, MEM_HEADS, LANES).transpose(0, 1, 2, 4, 3, 5)
                      .reshape(DEPTH, bsz, N_MEM, MEM_HEADS, MEM_HEAD_DIM))
    return (xp.reshape(bsz, seq, D_MODEL), xs.reshape(dbsz, dseq, D_MODEL),
            heads5(kv_p[0], bsz, seq), heads5(kv_p[1], bsz, seq), mem5(mem_p[0]), mem5(mem_p[1]), st("p_pool"),
            st("p_conv"), heads5(kv_s[0], dbsz, dseq), heads5(kv_s[1], dbsz, dseq), st("s_pool"), st("s_conv"))
```

```python
import functools

import jax
import jax.numpy as jnp
import numpy as np
from jax import lax
from jax.experimental import pallas as pl
from jax.experimental.pallas import tpu as pltpu

D_MODEL = 1024
DEPTH = 2
PAGE_SIZE = 128

POOL_WIDTH = D_MODEL // 4
POOL_WINDOWS = (2, 4, 8, 16)
POOL_GROUP_WIDTH = POOL_WIDTH // len(POOL_WINDOWS)
POOL_BUF = max(POOL_WINDOWS) - 1

CONV_WIDTH = D_MODEL // 4
CONV_K = 31
CONV_BUF = CONV_K - 1

ATT_HEADS = 4
ATT_HEAD_DIM = D_MODEL // 8
ATT_WIDTH = ATT_HEADS * ATT_HEAD_DIM
ROT_DIM = ATT_HEAD_DIM // 4
ROPE_THETA = 500000.0
MOBA_BLOCK = 256
MOBA_TOPK = 3

N_MEM = 256
MEM_HEADS = 4
MEM_HEAD_DIM = D_MODEL // MEM_HEADS

D_FF = ((8 * D_MODEL // 3 + 127) // 128) * 128
EPS = 1e-6

LANES = 128
SUBLANES = 8
VMEM_LIMIT_BYTES = 56 * 1024 * 1024

F32 = jnp.float32
BF16 = jnp.bfloat16
NEG_INF = float("-inf")

_NT = (((1,), (1,)), ((), ()))


def _params(n_axes):
    return pltpu.CompilerParams(dimension_semantics=("arbitrary",) * n_axes,
                                vmem_limit_bytes=VMEM_LIMIT_BYTES)


def _rms(x, g):
    return x * lax.rsqrt(jnp.mean(x * x, axis=-1, keepdims=True) + EPS) * g


def _layer_spec(shape, l, n_grid):
    zeros = (0,) * len(shape)
    once = pl.Buffered(1)
    if n_grid == 1:
        return pl.BlockSpec((None,) + tuple(shape), lambda i: (l,) + zeros, pipeline_mode=once)
    if n_grid == 2:
        return pl.BlockSpec((None,) + tuple(shape), lambda i, j: (l,) + zeros, pipeline_mode=once)
    return pl.BlockSpec((None,) + tuple(shape), lambda i, j, k: (l,) + zeros, pipeline_mode=once)


FFN_CHUNK = 256


def _ffn_kernel(xa_ref, xb_ref, g_ref, wi_ref, wo_ref, oa_ref, ob_ref):
    i = pl.program_id(0)
    x = jnp.where(i == 0, xb_ref[...], xa_ref[...])
    xn = _rms(x, g_ref[...]).astype(BF16)
    acc = None
    for c in range(D_FF // FFN_CHUNK):
        lo = c * FFN_CHUNK
        gate = jnp.dot(xn, wi_ref[:, lo:lo + FFN_CHUNK], preferred_element_type=F32)
        up = jnp.dot(xn, wi_ref[:, D_FF + lo:D_FF + lo + FFN_CHUNK], preferred_element_type=F32)
        act = (gate * jax.nn.sigmoid(gate) * up).astype(BF16)
        part = jnp.dot(act, wo_ref[lo:lo + FFN_CHUNK, :], preferred_element_type=F32)
        acc = part if acc is None else acc + part
    oa_ref[...] = x + 0.5 * acc

    @pl.when(i == 0)
    def _():
        ob_ref[...] = oa_ref[...]


def _ffn(xa, xb, ln, w_in, w_out, l, tm=512):
    n_a = xa.shape[0] // tm
    assert xa.shape[0] == n_a * tm and xb.shape[0] == tm
    a_spec = pl.BlockSpec((tm, D_MODEL), lambda i: (jnp.maximum(i - 1, 0), 0))
    b_spec = pl.BlockSpec((tm, D_MODEL), lambda i: (0, 0))
    return pl.pallas_call(
        _ffn_kernel,
        grid=(n_a + 1,),
        in_specs=[
            a_spec, b_spec,
            _layer_spec((1, D_MODEL), l, 1),
            _layer_spec((D_MODEL, 2 * D_FF), l, 1),
            _layer_spec((D_FF, D_MODEL), l, 1),
        ],
        out_specs=[a_spec, b_spec],
        out_shape=[jax.ShapeDtypeStruct(xa.shape, F32), jax.ShapeDtypeStruct(xb.shape, F32)],
        compiler_params=_params(1),
        name="ffn",
    )(xa, xb, ln, w_in, w_out)


def _norm_proj_kernel(*refs, segs, rope):
    x_ref, g_ref, w_ref, hg_ref = refs[:4]
    if rope:
        cos_ref, sin_lo_ref, sin_hi_ref = refs[4:7]
    out_refs = refs[len(refs) - len(segs):]
    tm = x_ref.shape[0]
    xn = _rms(x_ref[...], g_ref[...]).astype(BF16)
    for (c0, width, head_dim, gain_row, use_rope, head_rows), o_ref in zip(segs, out_refs):
        z = jnp.dot(xn, w_ref[:, c0:c0 + width], preferred_element_type=F32)
        if head_dim is None:
            o_ref[...] = z
            continue
        n_heads = width // head_dim
        for h in range(n_heads):
            zh = z[:, h * head_dim:(h + 1) * head_dim]
            if gain_row is not None:
                zh = _rms(zh, hg_ref[gain_row:gain_row + 1, :])
            if use_rope:
                half = ROT_DIM // 2
                zh = (zh * cos_ref[...]
                      + pltpu.roll(zh, half, 1) * sin_hi_ref[...]
                      + pltpu.roll(zh, LANES - half, 1) * sin_lo_ref[...])
            if not head_rows:
                o_ref[:, h * head_dim:(h + 1) * head_dim] = zh
                continue
            tiles = head_dim // LANES
            for tile in range(tiles):
                rows = pl.ds(tile * n_heads + h, tm, stride=n_heads * tiles)
                piece = zh[:, tile * LANES:(tile + 1) * LANES]
                if len(o_ref.shape) == 2:
                    o_ref[rows, :] = piece
                else:
                    for d in range(o_ref.shape[0]):
                        o_ref[d, rows, :] = piece


def _norm_proj(x, ln, w, l, head_gains, segs, tables=None, tm=512, stacked=None):
    m = x.shape[0]
    n_cols = w.shape[-1]
    rope = tables is not None
    in_specs = [
        pl.BlockSpec((tm, D_MODEL), lambda i: (i, 0)),
        _layer_spec((1, D_MODEL), l, 1),
        _layer_spec((D_MODEL, n_cols), l, 1),
        _layer_spec(head_gains.shape[1:], l, 1),
    ]
    args = [x, ln, w, head_gains]
    if rope:
        period = tables[0].shape[0] // tm
        for t in tables:
            in_specs.append(pl.BlockSpec((tm, LANES), lambda i: (i % period, 0)))
            args.append(t)
    out_specs, out_shape, aliases = [], [], {}
    for k, s in enumerate(segs):
        width, head_dim, head_rows = s[1], s[2], s[5]
        if not head_rows:
            out_specs.append(pl.BlockSpec((tm, width), lambda i: (i, 0)))
            out_shape.append(jax.ShapeDtypeStruct((m, width), F32))
            continue
        per_token = width // LANES
        out_shape.append(jax.ShapeDtypeStruct((DEPTH, m * per_token, LANES), F32))
        if stacked is None:
            out_specs.append(pl.BlockSpec((DEPTH, tm * per_token, LANES), lambda i: (0, i, 0)))
        else:
            out_specs.append(pl.BlockSpec((None, tm * per_token, LANES), lambda i: (l, i, 0)))
            aliases[len(args)] = k
            in_specs.append(pl.BlockSpec(memory_space=pl.ANY))
            args.append(stacked[len(aliases) - 1])
    return pl.pallas_call(
        functools.partial(_norm_proj_kernel, segs=tuple(segs), rope=rope),
        grid=(m // tm,),
        in_specs=in_specs,
        out_specs=out_specs,
        out_shape=out_shape,
        input_output_aliases=aliases,
        compiler_params=_params(1),
        name="norm_proj",
    )(*args)


def _rope_tables(positions):
    half = ROT_DIM // 2
    inv_freq = jnp.power(jnp.float32(ROPE_THETA), -jnp.arange(half, dtype=F32) / half)
    ang = positions.astype(F32)[:, None] * inv_freq[None, :]
    cos, sin = jnp.cos(ang), jnp.sin(ang)
    n = positions.shape[0]
    rest = ATT_HEAD_DIM - ROT_DIM
    cos_t = jnp.concatenate([cos, cos, jnp.ones((n, rest), F32)], axis=1)
    sin_lo = jnp.concatenate([-sin, jnp.zeros((n, half + rest), F32)], axis=1)
    sin_hi = jnp.concatenate([jnp.zeros((n, half), F32), sin, jnp.zeros((n, rest), F32)], axis=1)
    return cos_t, sin_lo, sin_hi


def _out_proj_kernel(*refs, n_terms):
    x_ref = refs[0]
    y_refs = refs[1:1 + n_terms]
    w_refs = refs[1 + n_terms:1 + 2 * n_terms]
    o_ref = refs[1 + 2 * n_terms]
    acc = x_ref[...]
    for y_ref, w_ref in zip(y_refs, w_refs):
        acc = acc + jnp.dot(y_ref[...].astype(BF16), w_ref[...], preferred_element_type=F32)
    o_ref[...] = acc


def _out_proj(x, ys, w, l, row_starts, tm=512):
    m = x.shape[0]
    in_specs = [pl.BlockSpec((tm, D_MODEL), lambda i: (i, 0))]
    for y in ys:
        in_specs.append(pl.BlockSpec((tm, y.shape[1]), lambda i: (i, 0)))
    for y, r0 in zip(ys, row_starts):
        width = y.shape[1]
        blk = r0 // width
        in_specs.append(pl.BlockSpec((None, width, D_MODEL), lambda i, blk=blk: (l, blk, 0)))
    return pl.pallas_call(
        functools.partial(_out_proj_kernel, n_terms=len(ys)),
        grid=(m // tm,),
        in_specs=in_specs,
        out_specs=pl.BlockSpec((tm, D_MODEL), lambda i: (i, 0)),
        out_shape=jax.ShapeDtypeStruct((m, D_MODEL), F32),
        compiler_params=_params(1),
        name="out_proj",
    )(x, *ys, *([w] * len(ys)))


def _select_blocks(gate, n_allowed):
    lane = lax.broadcasted_iota(jnp.int32, gate.shape, 1)
    allowed = lane < n_allowed
    g = jnp.where(allowed, gate, NEG_INF)
    rank = jnp.zeros(gate.shape, F32)
    for r in range(1, SUBLANES):
        lower = pltpu.roll(g, r, 1)
        higher = pltpu.roll(g, LANES - r, 1)
        rank = rank + jnp.where(lower >= g, 1.0, 0.0) + jnp.where(higher > g, 1.0, 0.0)
    return jnp.where(allowed, jnp.where(rank < MOBA_TOPK, 1.0, 0.0), 0.0)


def _dot_nt_split(a, b):
    a_hi = a.astype(BF16)
    a_lo = (a - a_hi.astype(F32)).astype(BF16)
    b_hi = b.astype(BF16)
    b_lo = (b - b_hi.astype(F32)).astype(BF16)
    m = a.shape[0]
    by_hi = lax.dot_general(jnp.concatenate([a_hi, a_lo], axis=0), b_hi, _NT, preferred_element_type=F32)
    return by_hi[0:m] + by_hi[m:2 * m] + lax.dot_general(a_hi, b_lo, _NT, preferred_element_type=F32)


def _block_expander(n_keys):
    blk = np.arange(n_keys) // MOBA_BLOCK
    return jnp.asarray((np.arange(LANES)[:, None] == blk[None, :]).astype(np.float32), dtype=BF16)


MASK_BIAS = -1e30


def _select_blocks_t(gate_t, n_allowed):
    blk = lax.broadcasted_iota(jnp.int32, gate_t.shape, 0)
    allowed = blk < n_allowed
    g = jnp.where(allowed, gate_t, NEG_INF)
    rank = jnp.zeros(gate_t.shape, F32)
    for m in range(SUBLANES):
        gm = g[m:m + 1, :]
        tie = jnp.where(gm == g, jnp.where(blk > m, 1.0, 0.0), 0.0)
        rank = rank + jnp.where(gm > g, 1.0, tie)
    return jnp.where(allowed, jnp.where(rank < MOBA_TOPK, 1.0, 0.0), 0.0)


def _moba_prompt_kernel(q_ref, k_ref, v_ref, kbias_ref, o_ref, ka_ref, vb_ref, km_ref, *, n_blocks, per_step):
    qi = pl.program_id(1)
    blk = MOBA_BLOCK
    dh = ATT_HEAD_DIM
    heads = range(ATT_HEADS)

    @pl.when(qi == 0)
    def _():
        km_ref[...] = jnp.zeros(km_ref.shape, F32)
        for h in heads:
            vb_ref[:, h * dh:(h + 1) * dh] = v_ref[pl.ds(h, n_blocks * blk, stride=ATT_HEADS), :].astype(BF16)
            ka_ref[h, :, dh:2 * dh] = kbias_ref[h]
            for n in range(n_blocks):
                kb = k_ref[pl.ds(n * blk * ATT_HEADS + h, blk, stride=ATT_HEADS), :]
                ka_ref[h, n * blk:(n + 1) * blk, 0:dh] = kb.astype(BF16)
                r = h * SUBLANES + n
                km_ref[r:r + 1, h * dh:(h + 1) * dh] = jnp.mean(kb, axis=0, keepdims=True)

    row = lax.broadcasted_iota(jnp.int32, (blk, blk), 0)
    col = lax.broadcasted_iota(jnp.int32, (blk, blk), 1)

    def step(c, r0):
        n_keys = (c + 1) * blk
        gated = c > MOBA_TOPK
        if gated:
            gate_all = _dot_nt_split(km_ref[...], q_ref[r0:r0 + blk, :])
            n_rows = ATT_HEADS * SUBLANES
            flags = []
            for h in heads:
                gate_t = gate_all[h * SUBLANES:(h + 1) * SUBLANES]
                blk_id = lax.broadcasted_iota(jnp.int32, gate_t.shape, 0)
                flags.append(jnp.where(blk_id == c, 0.0, 1.0 - _select_blocks_t(gate_t, c)))
            flags.append(jnp.zeros((LANES - n_rows, blk), F32))
            unchosen = jnp.concatenate(flags, axis=0).T.astype(BF16)
        for h in heads:
            q = q_ref[r0:r0 + blk, h * dh:(h + 1) * dh]
            qs = (q * (dh ** -0.5)).astype(BF16)
            if gated:
                qa = jnp.concatenate([qs, unchosen], axis=1)
                s = lax.dot_general(qa, ka_ref[h, 0:n_keys, :], _NT, preferred_element_type=F32)
            else:
                s = lax.dot_general(qs, ka_ref[h, 0:n_keys, 0:dh], _NT, preferred_element_type=F32)
            parts = [s[:, 0:c * blk]] if c else []
            parts.append(jnp.where(row >= col, s[:, c * blk:n_keys], NEG_INF))
            s = jnp.concatenate(parts, axis=1)
            p = jnp.exp(s - jnp.max(s, axis=-1, keepdims=True))
            inv = 1.0 / jnp.sum(p, axis=-1, keepdims=True)
            o = jnp.dot(p.astype(BF16), vb_ref[0:n_keys, h * dh:(h + 1) * dh], preferred_element_type=F32)
            o_ref[r0:r0 + blk, h * dh:(h + 1) * dh] = o * inv

    def tile(j):
        for r in range(per_step):
            step(j * per_step + r, r * blk)

    for j in range(n_blocks // per_step):
        pl.when(qi == j)(functools.partial(tile, j))


def _moba_prompt(q, k, v, bsz, t, l):
    assert t % MOBA_BLOCK == 0 and t // MOBA_BLOCK <= SUBLANES
    n_blocks = t // MOBA_BLOCK
    dh = ATT_HEAD_DIM
    key_blk = np.arange(t) // MOBA_BLOCK
    lane_of = np.arange(ATT_HEADS)[:, None, None] * SUBLANES + key_blk[None, :, None]
    kbias = jnp.asarray(np.where(lane_of == np.arange(LANES)[None, None, :], MASK_BIAS, 0.0), dtype=BF16)
    per_step = 2 if n_blocks % 2 == 0 else 1
    nt = n_blocks // per_step
    tile_spec = pl.BlockSpec((per_step * MOBA_BLOCK, ATT_WIDTH), lambda b, i: (b * nt + i, 0))
    return pl.pallas_call(
        functools.partial(_moba_prompt_kernel, n_blocks=n_blocks, per_step=per_step),
        grid=(bsz, nt),
        in_specs=[
            tile_spec,
            pl.BlockSpec((None, t * ATT_HEADS, dh), lambda b, i: (l, b, 0)),
            pl.BlockSpec((None, t * ATT_HEADS, dh), lambda b, i: (l, b, 0)),
            pl.BlockSpec((ATT_HEADS, t, LANES), lambda b, i: (0, 0, 0)),
        ],
        out_specs=tile_spec,
        out_shape=jax.ShapeDtypeStruct((bsz * t, ATT_WIDTH), F32),
        scratch_shapes=[
            pltpu.VMEM((ATT_HEADS, t, 2 * dh), BF16),
            pltpu.VMEM((t, ATT_WIDTH), BF16),
            pltpu.VMEM((ATT_HEADS * SUBLANES, ATT_WIDTH), F32),
        ],
        compiler_params=_params(2),
        name="moba_prompt",
    )(q, k, v, kbias)


def _moba_sample_kernel(pt_ref, q_ref, kn_ref, vn_ref, e_ref, *refs, n_pages, n_seq):
    del pt_ref
    k_refs = refs[:n_seq * n_pages]
    v_refs = refs[n_seq * n_pages:2 * n_seq * n_pages]
    o_ref, tok_ref = refs[2 * n_seq * n_pages:]
    n_new = q_ref.shape[1]
    rows = tok_ref.shape[2]
    tok_ref[...] = jnp.zeros(tok_ref.shape, F32)
    tok_ref[0, :, 0:n_new, :] = q_ref[...]
    tok_ref[1, :, 0:n_new, :] = kn_ref[...]
    tok_ref[2, :, 0:n_new, :] = vn_ref[...]
    q_ref, kn_ref, vn_ref = tok_ref.at[0], tok_ref.at[1], tok_ref.at[2]
    pages_per_block = MOBA_BLOCK // PAGE_SIZE
    n_blocks = n_pages // pages_per_block
    dh = ATT_HEAD_DIM
    pairs = [(g, h) for g in range(n_seq) for h in range(ATT_HEADS)]

    def block_rows(page_refs, g, n, h):
        first = g * n_pages + n * pages_per_block
        return jnp.concatenate(
            [page_refs[j][0, pl.ds(h, PAGE_SIZE, stride=ATT_HEADS), :] for j in range(first, first + pages_per_block)],
            axis=0)

    qs, gate_rows, s_rows = [], [], []
    for g, h in pairs:
        q = q_ref[g, :, h * dh:(h + 1) * dh]
        qs.append((q * (dh ** -0.5)).astype(BF16))
        mean_rows, s_parts = [], []
        for n in range(n_blocks):
            kb = block_rows(k_refs, g, n, h)
            mean_rows.append(jnp.sum(kb, axis=0, keepdims=True) / MOBA_BLOCK)
            s_parts.append(lax.dot_general(qs[-1], kb.astype(BF16), _NT, preferred_element_type=F32))
        mean_rows.append(jnp.zeros((LANES - n_blocks, dh), F32))
        gate_rows.append(_dot_nt_split(q, jnp.concatenate(mean_rows, axis=0)))
        s_rows.append(jnp.concatenate(s_parts, axis=1))
    sel = _select_blocks(jnp.concatenate(gate_rows, axis=0), n_blocks)
    sel_keys = jnp.dot(sel.astype(BF16), e_ref[...], preferred_element_type=F32)
    s = jnp.where(sel_keys > 0.5, jnp.concatenate(s_rows, axis=0), NEG_INF)

    pad = jnp.zeros((LANES - rows, dh), F32)
    padded = lambda ref, g, h: jnp.concatenate([ref[g, :, h * dh:(h + 1) * dh], pad], axis=0).astype(BF16)
    s_own = jnp.concatenate(
        [lax.dot_general(qs[i], padded(kn_ref, g, h), _NT, preferred_element_type=F32)
         for i, (g, h) in enumerate(pairs)], axis=0)
    t_row = lax.broadcasted_iota(jnp.int32, s_own.shape, 0) % rows
    key = lax.broadcasted_iota(jnp.int32, s_own.shape, 1)
    s_own = jnp.where(key <= t_row, s_own, NEG_INF)

    m = jnp.maximum(jnp.max(s, axis=-1, keepdims=True), jnp.max(s_own, axis=-1, keepdims=True))
    p = jnp.exp(s - m)
    p_own = jnp.exp(s_own - m)
    inv = 1.0 / (jnp.sum(p, axis=-1, keepdims=True) + jnp.sum(p_own, axis=-1, keepdims=True))
    pb = p.astype(BF16)
    pb_own = p_own.astype(BF16)
    outs = []
    for i, (g, h) in enumerate(pairs):
        r0 = i * rows
        o = jnp.dot(pb_own[r0:r0 + rows], padded(vn_ref, g, h), preferred_element_type=F32)
        for n in range(n_blocks):
            o = o + jnp.dot(pb[r0:r0 + rows, n * MOBA_BLOCK:(n + 1) * MOBA_BLOCK],
                            block_rows(v_refs, g, n, h).astype(BF16), preferred_element_type=F32)
        outs.append(o * inv[r0:r0 + rows])
    for g in range(n_seq):
        o_ref[g] = jnp.concatenate(outs[g * ATT_HEADS:(g + 1) * ATT_HEADS], axis=1)[0:n_new]


def _pool_windows_to_lanes(sums, counts, cur):
    lane = lax.broadcasted_iota(jnp.int32, cur.shape, 1)
    d = sums[-1] / counts[-1]
    for g in range(len(POOL_WINDOWS) - 2, -1, -1):
        d = jnp.where(lane < (g + 1) * POOL_GROUP_WIDTH, sums[g] / counts[g], d)
    return d - cur


def _conv_tail(y, ln_g, ln_b, pw_w, pw_b):
    yc = y - jnp.mean(y, axis=-1, keepdims=True)
    yn = yc * lax.rsqrt(jnp.mean(yc * yc, axis=-1, keepdims=True) + EPS) * ln_g + ln_b
    act = yn * jax.nn.sigmoid(yn)
    return jnp.dot(act.astype(BF16), pw_w, preferred_element_type=F32) + pw_b


POOL_HALO = 16
CONV_HALO = 32


def _poolconv_prompt_kernel(x_ref, u_ref, ag_ref, ya_ref, pw_ref, ps_ref, dw_ref, db_ref, lg_ref, lb_ref,
                            cw_ref, cb_ref, wp_ref, wc_ref, wa_ref, o_ref, ht_ref, e_ref, h_ref, hs_ref, *, tt,
                            side_work=None):
    ti = pl.program_id(1)

    @pl.when(ti == 0)
    def _():
        e_ref[0:POOL_HALO, :] = jnp.zeros((POOL_HALO, POOL_WIDTH), F32)
        h_ref[0:CONV_HALO, :] = jnp.zeros((CONV_HALO, CONV_WIDTH), F32)

    if side_work is not None:
        side_work()

    u = u_ref[...]
    e_ref[POOL_HALO:POOL_HALO + tt, :] = u
    a = ag_ref[:, 0:CONV_WIDTH]
    gate = ag_ref[:, CONV_WIDTH:2 * CONV_WIDTH]
    h_ref[CONV_HALO:CONV_HALO + tt, :] = a * jax.nn.sigmoid(gate)

    pos = ti * tt + lax.broadcasted_iota(jnp.int32, (tt, 1), 0)
    sums, counts = [], []
    run = u
    taken = 1
    for win in POOL_WINDOWS:
        while taken < win:
            run = run + e_ref[POOL_HALO - taken:POOL_HALO - taken + tt, :]
            taken += 1
        sums.append(run)
        counts.append(jnp.minimum(pos + 1, win).astype(F32))
    d = _pool_windows_to_lanes(sums, counts, u)
    y_pool = jnp.dot(d.astype(BF16), pw_ref[...], preferred_element_type=F32) * ps_ref[...]

    span = tt + CONV_HALO - SUBLANES
    for s in range(1, SUBLANES):
        hs_ref[s - 1, 0:span, :] = h_ref[s:s + span, :]
    y = jnp.zeros((tt, CONV_WIDTH), F32) + db_ref[...]
    for j in range(CONV_K):
        r0 = CONV_HALO - CONV_BUF + j
        s = r0 % SUBLANES
        rows = h_ref[r0:r0 + tt, :] if s == 0 else hs_ref[s - 1, r0 - s:r0 - s + tt, :]
        y = y + rows * dw_ref[j:j + 1, :]
    y_conv = _conv_tail(y, lg_ref[...], lb_ref[...], cw_ref[...], cb_ref[...])

    out = x_ref[...] + jnp.dot(y_pool.astype(BF16), wp_ref[...], preferred_element_type=F32)
    out = out + jnp.dot(y_conv.astype(BF16), wc_ref[...], preferred_element_type=F32)
    o_ref[...] = out + jnp.dot(ya_ref[...].astype(BF16), wa_ref[...], preferred_element_type=F32)

    ht_ref[0] = h_ref[tt:tt + CONV_HALO, :]
    e_ref[0:POOL_HALO, :] = e_ref[tt:tt + POOL_HALO, :]
    h_ref[0:CONV_HALO, :] = h_ref[tt:tt + CONV_HALO, :]


N_MIX_IN = 15


def _mix_out_moba_kernel(pt_ref, *refs, tt, n_pages, n_seq):
    n_moba_in = 4 + 2 * n_seq * n_pages
    mix_in = refs[:N_MIX_IN]
    moba_in = refs[N_MIX_IN:N_MIX_IN + n_moba_in]
    o_ref, ht_ref, att_ref = refs[N_MIX_IN + n_moba_in:N_MIX_IN + n_moba_in + 3]
    *mix_scratch, tok_ref = refs[N_MIX_IN + n_moba_in + 3:]
    side = functools.partial(_moba_sample_kernel, pt_ref, *moba_in, att_ref, tok_ref, n_pages=n_pages,
                             n_seq=n_seq)
    _poolconv_prompt_kernel(*mix_in, o_ref, ht_ref, *mix_scratch, tt=tt, side_work=side)


def _mix_out_prompt_moba_sample(x, u, ag, y_att, bsz, t, weights, w_out, q_s, k_new, v_new, cache_k, cache_v,
                                page_table, l, tt=256):
    pool_w, pool_s, dw_w, dw_b, ln_g, ln_b, pw_w, pw_b = weights
    nt = t // tt
    dbsz, rows, _ = q_s.shape
    n_seq = dbsz // (bsz * nt)
    assert n_seq * bsz * nt == dbsz
    n_pages = page_table.shape[1]
    n_phys = cache_k.shape[0] // DEPTH
    past_len = n_pages * PAGE_SIZE
    assert past_len % MOBA_BLOCK == 0 and past_len // MOBA_BLOCK <= SUBLANES
    base = l * n_phys
    row = lambda b, i, pt: (b * nt + i, 0)
    w_rows = lambda width, blk: pl.BlockSpec((None, width, D_MODEL), lambda b, i, pt: (l, blk, 0))
    tok_spec = pl.BlockSpec((n_seq, rows, ATT_WIDTH), lambda b, i, pt: (b * nt + i, 0, 0))
    page_specs = [
        pl.BlockSpec((1, PAGE_SIZE * ATT_HEADS, ATT_HEAD_DIM),
                     lambda b, i, pt, g=g, j=j: (base + pt[((b * nt + i) * n_seq + g) * n_pages + j], 0, 0))
        for g in range(n_seq) for j in range(n_pages)
    ]
    grid_spec = pltpu.PrefetchScalarGridSpec(
        num_scalar_prefetch=1,
        grid=(bsz, nt),
        in_specs=[
            pl.BlockSpec((tt, D_MODEL), row),
            pl.BlockSpec((tt, POOL_WIDTH), row),
            pl.BlockSpec((tt, 2 * CONV_WIDTH), row),
            pl.BlockSpec((tt, ATT_WIDTH), row),
            _layer_spec((POOL_WIDTH, POOL_WIDTH), l, 3),
            _layer_spec((1, POOL_WIDTH), l, 3),
            _layer_spec((CONV_HALO, CONV_WIDTH), l, 3),
            _layer_spec((1, CONV_WIDTH), l, 3),
            _layer_spec((1, CONV_WIDTH), l, 3),
            _layer_spec((1, CONV_WIDTH), l, 3),
            _layer_spec((CONV_WIDTH, CONV_WIDTH), l, 3),
            _layer_spec((1, CONV_WIDTH), l, 3),
            w_rows(POOL_WIDTH, 0),
            w_rows(CONV_WIDTH, POOL_WIDTH // CONV_WIDTH),
            w_rows(ATT_WIDTH, (POOL_WIDTH + CONV_WIDTH) // ATT_WIDTH),
            tok_spec, tok_spec, tok_spec,
            pl.BlockSpec((LANES, past_len), lambda b, i, pt: (0, 0)),
        ] + page_specs + page_specs,
        out_specs=[
            pl.BlockSpec((tt, D_MODEL), row),
            pl.BlockSpec((1, CONV_HALO, CONV_WIDTH), lambda b, i, pt: (b, 0, 0)),
            tok_spec,
        ],
        scratch_shapes=[
            pltpu.VMEM((POOL_HALO + tt, POOL_WIDTH), F32),
            pltpu.VMEM((CONV_HALO + tt, CONV_WIDTH), F32),
            pltpu.VMEM((SUBLANES - 1, CONV_HALO + tt, CONV_WIDTH), F32),
            pltpu.VMEM((3, n_seq, 2 * SUBLANES, ATT_WIDTH), F32),
        ],
    )
    return pl.pallas_call(
        functools.partial(_mix_out_moba_kernel, tt=tt, n_pages=n_pages, n_seq=n_seq),
        grid_spec=grid_spec,
        out_shape=[
            jax.ShapeDtypeStruct((bsz * t, D_MODEL), F32),
            jax.ShapeDtypeStruct((bsz, CONV_HALO, CONV_WIDTH), F32),
            jax.ShapeDtypeStruct((dbsz, rows, ATT_WIDTH), F32),
        ],
        compiler_params=_params(2),
        name="mix_out_moba",
    )(page_table.reshape(-1), x, u, ag, y_att, pool_w, pool_s, dw_w, dw_b, ln_g, ln_b, pw_w, pw_b,
      w_out, w_out, w_out, q_s, k_new, v_new, _block_expander(past_len),
      *([cache_k] * (n_seq * n_pages)), *([cache_v] * (n_seq * n_pages)))


def _poolconv_sample_kernel(sp_ref, u_ref, sc_ref, ag_ref, pw_ref, ps_ref, dw_ref, db_ref, lg_ref, lb_ref,
                            cw_ref, cb_ref, yp_ref, yc_ref, h_ref, *, n_new, pos0):
    def pool_row(r):
        return sp_ref[r] if r < POOL_BUF else u_ref[r - POOL_BUF]

    for t in range(n_new):
        h_ref[t] = ag_ref[t, :, 0:CONV_WIDTH] * jax.nn.sigmoid(ag_ref[t, :, CONV_WIDTH:2 * CONV_WIDTH])

    def conv_row(r):
        return sc_ref[r] if r < CONV_BUF else h_ref[r - CONV_BUF]

    for t in range(n_new):
        cur = u_ref[t]
        sums, counts = [], []
        run = cur
        taken = 1
        for win in POOL_WINDOWS:
            while taken < win:
                run = run + pool_row(POOL_BUF + t - taken)
                taken += 1
            sums.append(run)
            counts.append(float(min(pos0 + t + 1, win)))
        d = _pool_windows_to_lanes(sums, counts, cur)
        yp_ref[t] = jnp.dot(d.astype(BF16), pw_ref[...], preferred_element_type=F32) * ps_ref[...]

        y = jnp.zeros(cur.shape, F32) + db_ref[...]
        for j in range(CONV_K):
            y = y + conv_row(t + j) * dw_ref[j:j + 1, :]
        yc_ref[t] = _conv_tail(y, lg_ref[...], lb_ref[...], cw_ref[...], cb_ref[...])


def _poolconv_sample(state_pool_t, u_t, state_conv_t, ag_t, pos0, weights, l):
    pool_w, pool_s, dw_w, dw_b, ln_g, ln_b, pw_w, pw_b = weights
    n_new, bsz, _ = u_t.shape
    full = lambda a: pl.BlockSpec(a.shape, lambda i: (0,) * a.ndim)
    out = jax.ShapeDtypeStruct((n_new, bsz, CONV_WIDTH), F32)
    return pl.pallas_call(
        functools.partial(_poolconv_sample_kernel, n_new=n_new, pos0=pos0),
        grid=(1,),
        in_specs=[
            full(state_pool_t), full(u_t), full(state_conv_t), full(ag_t),
            _layer_spec((POOL_WIDTH, POOL_WIDTH), l, 1),
            _layer_spec((1, POOL_WIDTH), l, 1),
            _layer_spec((CONV_HALO, CONV_WIDTH), l, 1),
            _layer_spec((1, CONV_WIDTH), l, 1),
            _layer_spec((1, CONV_WIDTH), l, 1),
            _layer_spec((1, CONV_WIDTH), l, 1),
            _layer_spec((CONV_WIDTH, CONV_WIDTH), l, 1),
            _layer_spec((1, CONV_WIDTH), l, 1),
        ],
        out_specs=[pl.BlockSpec(out.shape, lambda i: (0, 0, 0))] * 3,
        out_shape=[out, out, out],
        compiler_params=_params(1),
        name="poolconv_sample",
    )(state_pool_t, u_t, state_conv_t, ag_t, pool_w, pool_s, dw_w, dw_b, ln_g, ln_b, pw_w, pw_b)


def _cross_kernel(g_ref, wq_ref, hg_ref, wo_ref, xa_ref, ka_ref, va_ref, xb_ref, kb_ref, vb_ref, oa_ref, ob_ref,
                  xb_pad_ref):
    weights = (g_ref, wq_ref, hg_ref, wo_ref)
    n_new = xb_ref.shape[1]
    xb_pad_ref[...] = jnp.zeros(xb_pad_ref.shape, F32)
    xb_pad_ref[:, 0:n_new, :] = xb_ref[...]
    ob_ref[...] = _cross_rows(xb_pad_ref, kb_ref, vb_ref, *weights)[:, 0:n_new, :]
    oa_ref[...] = _cross_rows(xa_ref, ka_ref, va_ref, *weights)


def _cross_rows(x_ref, k_ref, v_ref, g_ref, wq_ref, hg_ref, wo_ref):
    hd = MEM_HEAD_DIM
    n_seq, rows, _ = x_ref.shape
    x = x_ref[...].reshape(n_seq * rows, D_MODEL)
    xn = _rms(x, g_ref[...]).astype(BF16)
    q_all = jnp.dot(xn, wq_ref[...], preferred_element_type=F32)
    halves = range(hd // LANES)

    def head_slab(ref, g, h):
        return jnp.concatenate(
            [ref[g, pl.ds(half * MEM_HEADS + h, N_MEM, stride=len(halves) * MEM_HEADS), :] for half in halves],
            axis=1)

    pairs = [(g, h) for g in range(n_seq) for h in range(MEM_HEADS)]
    scores = []
    for g, h in pairs:
        q = _rms(q_all[g * rows:(g + 1) * rows, h * hd:(h + 1) * hd], hg_ref[...])
        q = (q * (hd ** -0.5)).astype(BF16)
        scores.append(lax.dot_general(q, head_slab(k_ref, g, h).astype(BF16), _NT, preferred_element_type=F32))
    s = jnp.concatenate(scores, axis=0)
    p = jnp.exp(s - jnp.max(s, axis=-1, keepdims=True))
    inv = 1.0 / jnp.sum(p, axis=-1, keepdims=True)
    o_seqs = []
    for g in range(n_seq):
        o_heads = []
        for h in range(MEM_HEADS):
            r0 = (g * MEM_HEADS + h) * rows
            o = jnp.dot(p[r0:r0 + rows].astype(BF16), head_slab(v_ref, g, h).astype(BF16),
                        preferred_element_type=F32)
            o_heads.append((o * inv[r0:r0 + rows]).astype(BF16))
        o_seqs.append(jnp.concatenate(o_heads, axis=1))
    o_all = o_seqs[0] if n_seq == 1 else jnp.concatenate(o_seqs, axis=0)
    y = x + jnp.dot(o_all, wo_ref[...], preferred_element_type=F32)
    return y.reshape(n_seq, rows, D_MODEL)


def _interleaved_memory(cache):
    depth, s, n_mem, heads, hd = cache.shape
    halves = hd // LANES
    c = cache.reshape(depth, s, n_mem, heads, halves, LANES).transpose(0, 1, 2, 4, 3, 5)
    return c.reshape(depth * s, n_mem * halves * heads, LANES)


def _cross(ln, w_q, head_gain, w_o, l, xa, mem_ka, mem_va, seq0_a, tq, xb, mem_kb, mem_vb, seq0_b):
    sa, ra, _ = xa.shape
    sb, rb, _ = xb.shape
    nt = ra // tq
    n_b = sb // (sa * nt)
    assert n_b * sa * nt == sb and seq0_b % n_b == 0
    xa_spec = pl.BlockSpec((1, tq, D_MODEL), lambda s, i: (s, i, 0))
    kva_spec = pl.BlockSpec((1,) + mem_ka.shape[1:], lambda s, i: (seq0_a + s, 0, 0))
    xb_spec = pl.BlockSpec((n_b, rb, D_MODEL), lambda s, i: (s * nt + i, 0, 0))
    kvb_spec = pl.BlockSpec((n_b,) + mem_kb.shape[1:], lambda s, i: (seq0_b // n_b + s * nt + i, 0, 0))
    return pl.pallas_call(
        _cross_kernel,
        grid=(sa, nt),
        in_specs=[_layer_spec((1, D_MODEL), l, 2), _layer_spec((D_MODEL, D_MODEL), l, 2),
                  _layer_spec((1, MEM_HEAD_DIM), l, 2), _layer_spec((D_MODEL, D_MODEL), l, 2),
                  xa_spec, kva_spec, kva_spec, xb_spec, kvb_spec, kvb_spec],
        out_specs=[xa_spec, xb_spec],
        out_shape=[jax.ShapeDtypeStruct(xa.shape, F32), jax.ShapeDtypeStruct(xb.shape, F32)],
        scratch_shapes=[pltpu.VMEM((n_b, -(-rb // SUBLANES) * SUBLANES, D_MODEL), F32)],
        compiler_params=_params(2),
        name="cross",
    )(ln, w_q, head_gain, w_o, xa, mem_ka, mem_va, xb, mem_kb, mem_vb)


def kernel(x_prompt, x_sample, cache_att_k, cache_att_v, cache_mem_k, cache_mem_v, state_pool, state_conv, page_table, mem_prompt, ln_ffn1, w_ffn1_in, w_ffn1_out, ln_mix, w_in, pool_w, pool_scale, conv_dw_w, conv_dw_b, conv_ln_g, conv_ln_b, conv_pw_w, conv_pw_b, att_q_norm, att_k_norm, w_out, ln_cross, ln_mem, w_cq, w_ckv, cq_norm, ck_norm, w_co, ln_ffn2, w_ffn2_in, w_ffn2_out):
    bsz, seq, _ = x_prompt.shape
    dbsz, dseq, _ = x_sample.shape
    n_pages = page_table.shape[1]
    past_len = n_pages * PAGE_SIZE
    n_phys = cache_att_k.shape[1]

    bf = lambda w: w.astype(BF16)
    rowv = lambda v: v[:, None, :]
    w1i, w1o, w2i, w2o = bf(w_ffn1_in), bf(w_ffn1_out), bf(w_ffn2_in), bf(w_ffn2_out)
    w_in_b, w_out_b, w_cq_b, w_ckv_b, w_co_b = bf(w_in), bf(w_out), bf(w_cq), bf(w_ckv), bf(w_co)
    groups = len(POOL_WINDOWS)
    eye = jnp.eye(groups, dtype=F32)
    pool_bd = bf((pool_w[:, :, :, None, :] * eye[None, :, None, :, None])
                 .reshape(DEPTH, POOL_WIDTH, POOL_WIDTH))
    dw_pad = jnp.pad(conv_dw_w, ((0, 0), (0, CONV_HALO - CONV_K), (0, 0)))
    mixer_w = (pool_bd, rowv(pool_scale), dw_pad, rowv(conv_dw_b), rowv(conv_ln_g), rowv(conv_ln_b),
               bf(conv_pw_w), rowv(conv_pw_b))
    ln1, lnm, lnc, lnmem, ln2 = rowv(ln_ffn1), rowv(ln_mix), rowv(ln_cross), rowv(ln_mem), rowv(ln_ffn2)

    tables_p = _rope_tables(jnp.arange(seq, dtype=jnp.int32))
    tm_s = dbsz * dseq
    tables_s = _rope_tables(past_len + jnp.arange(tm_s, dtype=jnp.int32) % dseq)

    c1 = POOL_WIDTH
    c3 = c1 + 2 * CONV_WIDTH
    c4 = c3 + ATT_WIDTH
    c5 = c4 + ATT_WIDTH
    mix_segs = [(c3, ATT_WIDTH, ATT_HEAD_DIM, 0, True, False), (c4, ATT_WIDTH, ATT_HEAD_DIM, 1, True, True),
                (c5, ATT_WIDTH, ATT_HEAD_DIM, None, False, True),
                (0, c1, None, None, False, False), (c1, 2 * CONV_WIDTH, None, None, False, False)]
    mem_segs = [(0, D_MODEL, MEM_HEAD_DIM, 0, False, True), (D_MODEL, D_MODEL, MEM_HEAD_DIM, None, False, True)]
    out_rows = (0, c1, c1 + CONV_WIDTH)

    cache_k = cache_att_k.reshape(DEPTH * n_phys, PAGE_SIZE * ATT_HEADS, ATT_HEAD_DIM)
    cache_v = cache_att_v.reshape(DEPTH * n_phys, PAGE_SIZE * ATT_HEADS, ATT_HEAD_DIM)
    mem_k_s = _interleaved_memory(cache_mem_k)
    mem_v_s = _interleaved_memory(cache_mem_v)
    mem_rows = mem_prompt.reshape(bsz * N_MEM, D_MODEL)

    xp = x_prompt.reshape(bsz * seq, D_MODEL)
    xs = x_sample.reshape(dbsz * dseq, D_MODEL)
    outs = {name: [] for name in ("p_pool", "p_conv", "s_pool", "s_conv")}
    cq_gain = rowv(cq_norm)
    ck_gain = rowv(ck_norm)
    att_gains_all = jnp.stack([att_q_norm, att_k_norm], axis=1)
    kv_p = kv_s = mem_p = None

    for l in range(DEPTH):
        att_gains = att_gains_all

        mem_p = _norm_proj(mem_rows, lnmem, w_ckv_b, l, ck_gain, mem_segs, stacked=mem_p)
        mk, mv = (a.reshape(DEPTH * bsz, -1, LANES) for a in mem_p)
        xp, xs = _ffn(xp, xs, ln1, w1i, w1o, l)
        q, k, v, u, ag = _norm_proj(xp, lnm, w_in_b, l, att_gains, mix_segs, tables=tables_p, stacked=kv_p,
                                    tm=256)
        kv_p = (k, v)
        y_att = _moba_prompt(q, k, v, bsz, seq, l)
        p_pool = u.reshape(bsz, seq, POOL_WIDTH)[:, seq - POOL_BUF:]

        q_s, k_s, v_s, u_s, ag_s = _norm_proj(xs, lnm, w_in_b, l, att_gains, mix_segs, tables=tables_s, tm=tm_s,
                                              stacked=kv_s)
        kv_s = (k_s, v_s)
        pad3 = lambda a: a.reshape(dbsz, dseq, ATT_WIDTH)
        xp, h_tail, y_att_s = _mix_out_prompt_moba_sample(
            xp, u, ag, y_att, bsz, seq, mixer_w, w_out_b, pad3(q_s), pad3(k_s[l]), pad3(v_s[l]),
            cache_k, cache_v, page_table, l)
        outs["p_pool"].append(p_pool)
        outs["p_conv"].append(h_tail[:, CONV_HALO - CONV_BUF:])

        y_att = y_att_s.reshape(dbsz * dseq, ATT_WIDTH)
        tmaj = lambda a: jnp.swapaxes(a, 0, 1)
        u3 = u_s.reshape(dbsz, dseq, POOL_WIDTH)
        y_pool, y_conv, h_new = _poolconv_sample(
            tmaj(state_pool[l]), tmaj(u3), tmaj(state_conv[l]), tmaj(ag_s.reshape(dbsz, dseq, 2 * CONV_WIDTH)),
            past_len, mixer_w, l)
        bmaj = lambda a: jnp.swapaxes(a, 0, 1).reshape(dbsz * dseq, a.shape[-1])
        xs = _out_proj(xs, [bmaj(y_pool), bmaj(y_conv), y_att], w_out_b, l, out_rows, tm=tm_s)
        xp, xs = _cross(lnc, w_cq_b, cq_gain, w_co_b, l, xp.reshape(bsz, seq, D_MODEL), mk, mv, l * bsz, 512,
                        xs.reshape(dbsz, dseq, D_MODEL), mem_k_s, mem_v_s, l * dbsz)
        xp = xp.reshape(bsz * seq, D_MODEL)
        xs = xs.reshape(dbsz * dseq, D_MODEL)
        xp, xs = _ffn(xp, xs, ln2, w2i, w2o, l)
        outs["s_pool"].append(jnp.concatenate([state_pool[l][:, dseq:], u3], axis=1))
        outs["s_conv"].append(jnp.concatenate([state_conv[l][:, dseq:], jnp.swapaxes(h_new, 0, 1)], axis=1))

    st = lambda name: jnp.stack(outs[name])
    heads5 = lambda a, b, t: a.reshape(DEPTH, b, t, ATT_HEADS, ATT_HEAD_DIM)
    tiles = MEM_HEAD_DIM // LANES
    mem5 = lambda a: (a.reshape(DEPTH, bsz, N_MEM, tiles, MEM_HEADS, LANES).transpose(0, 1, 2, 4, 3, 5)
                      .reshape(DEPTH, bsz, N_MEM, MEM_HEADS, MEM_HEAD_DIM))
    return (xp.reshape(bsz, seq, D_MODEL), xs.reshape(dbsz, dseq, D_MODEL),
            heads5(kv_p[0], bsz, seq), heads5(kv_p[1], bsz, seq), mem5(mem_p[0]), mem5(mem_p[1]), st("p_pool"),
            st("p_conv"), heads5(kv_s[0], dbsz, dseq), heads5(kv_s[1], dbsz, dseq), st("s_pool"), st("s_conv"))
```
